```python
import math
import jax
import jax.numpy as jnp
from jax import lax
import numpy as np

D_MODEL = 2048
BATCH = 4
SEQ = 4096
DEPTH = 4

GRID_W = 64
CTX_LEN = 256
EPS = 1e-6

HEAD_DIM = 128
N_Q_HEADS = 8
N_KV_HEADS = 2
Q_PER_KV = N_Q_HEADS // N_KV_HEADS
ATTN_WIDTH = N_Q_HEADS * HEAD_DIM
KV_WIDTH = N_KV_HEADS * HEAD_DIM
WINDOW = 128
ATTN_BLOCK = 128
ROPE_THETA = 10000.0
ROPE_PAIRS_PER_AXIS = HEAD_DIM // 4

HYENA_WIDTH = D_MODEL // 4
HYENA_ORDER = 2
HYENA_SHORT_W = 3
HYENA_BANDS = 16
HYENA_EMB = 1 + 2 * HYENA_BANDS
HYENA_FILTER_HIDDEN = 64
HYENA_DECAY_TARGET = 1e-2
HYENA_FAST_DECAY = 0.3
HYENA_SLOW_DECAY = 1.5

LRU_WIDTH = D_MODEL // 4
LRU_BLOCKS = 8
LRU_BLOCK = LRU_WIDTH // LRU_BLOCKS
LRU_CONV_W = 4
LRU_C = 8.0

N_BRANCH = 3
CTX_STATE_COLS = 2 * KV_WIDTH + LRU_WIDTH
Q_OFF = CTX_STATE_COLS
HY_OFF = Q_OFF + ATTN_WIDTH
LRU_Y_OFF = HY_OFF + 3 * HYENA_WIDTH
GATE_OFF = LRU_Y_OFF + LRU_WIDTH
D_IN = GATE_OFF + N_BRANCH * D_MODEL

N_EXPERTS = 16
N_GROUPS = 4
EXPERTS_PER_GROUP = N_EXPERTS // N_GROUPS
TOP_K = 2
D_EXPERT = D_MODEL // 4

kernel_name = 'hybrid_dit_hyena_swa_rglru_moe'

F32 = jnp.float32


def rmsnorm(x, g):
    xf = x.astype(F32)
    y = xf * lax.rsqrt(jnp.mean(xf * xf, axis=-1, keepdims=True) + EPS)
    return (y * g.astype(F32)).astype(x.dtype)


def modulate(x, shift, scale):
    return x * (1.0 + scale) + shift


def dwconv(u, w, b):
    width, ch = w.shape
    left = width // 2
    y = lax.conv_general_dilated(u, w[:, None, :].astype(u.dtype), window_strides=(1,),
                                 padding=[(left, width - 1 - left)],
                                 dimension_numbers=('NWC', 'WIO', 'NWC'),
                                 feature_group_count=ch)
    return y + b.astype(u.dtype)


def axial_rope_tables(n_tokens):
    rows = n_tokens // GRID_W
    row = jnp.repeat(jnp.arange(rows), GRID_W).astype(F32)
    col = jnp.tile(jnp.arange(GRID_W), rows).astype(F32)
    inv = ROPE_THETA ** (-jnp.arange(ROPE_PAIRS_PER_AXIS, dtype=F32) / ROPE_PAIRS_PER_AXIS)
    ang = jnp.concatenate([row[:, None] * inv, col[:, None] * inv], axis=-1)
    return jnp.cos(ang), jnp.sin(ang)


def apply_rope(x, cos, sin):
    xf = x.astype(F32)
    x1, x2 = xf[..., :HEAD_DIM // 2], xf[..., HEAD_DIM // 2:]
    c = cos[None, :, None, :]
    s = sin[None, :, None, :]
    return jnp.concatenate([x1 * c - x2 * s, x2 * c + x1 * s], axis=-1).astype(x.dtype)


def split_state(p):
    return p[..., :KV_WIDTH], p[..., KV_WIDTH:2 * KV_WIDTH], p[..., 2 * KV_WIDTH:CTX_STATE_COLS]


def split_rest(p):
    return p[..., Q_OFF:HY_OFF], p[..., HY_OFF:LRU_Y_OFF], p[..., LRU_Y_OFF:GATE_OFF], p[..., GATE_OFF:]


def sink_softmax(s, sink):
    s = s.astype(F32)
    sk = jnp.broadcast_to(sink.astype(F32), s.shape[:-1] + (1,))
    return jax.nn.softmax(jnp.concatenate([s, sk], axis=-1), axis=-1)[..., :-1]


def latent_attention(q, k, v, k_ctx, v_ctx, sink):
    B, N = q.shape[:2]
    nb = N // ATTN_BLOCK
    scale = HEAD_DIM ** -0.5
    qb = q.reshape(B, nb, ATTN_BLOCK, N_KV_HEADS, Q_PER_KV, HEAD_DIM)
    pad = ((0, 0), (ATTN_BLOCK, ATTN_BLOCK), (0, 0), (0, 0))
    kp = jnp.pad(k, pad).reshape(B, nb + 2, ATTN_BLOCK, N_KV_HEADS, HEAD_DIM)
    vp = jnp.pad(v, pad).reshape(B, nb + 2, ATTN_BLOCK, N_KV_HEADS, HEAD_DIM)
    kw = jnp.concatenate([kp[:, :-2], kp[:, 1:-1], kp[:, 2:]], axis=2)
    vw = jnp.concatenate([vp[:, :-2], vp[:, 1:-1], vp[:, 2:]], axis=2)
    s_loc = jnp.einsum('bnqhgd,bnkhd->bnhgqk', qb, kw).astype(F32) * scale
    qi = jnp.arange(ATTN_BLOCK)
    kj = jnp.arange(3 * ATTN_BLOCK)
    band = jnp.abs(qi[:, None] + ATTN_BLOCK - kj[None, :]) <= WINDOW
    key_pos = (jnp.arange(nb)[:, None] - 1) * ATTN_BLOCK + kj[None, :]
    in_range = (key_pos >= 0) & (key_pos < N)
    mask = band[None, :, :] & in_range[:, None, :]
    s_loc = jnp.where(mask[None, :, None, None, :, :], s_loc, -jnp.inf)
    s_ctx = jnp.einsum('bnqhgd,bchd->bnhgqc', qb, k_ctx).astype(F32) * scale
    p = sink_softmax(jnp.concatenate([s_loc, s_ctx], axis=-1),
                     sink.reshape(N_KV_HEADS, Q_PER_KV)[None, None, :, :, None, None])
    p_loc = p[..., :3 * ATTN_BLOCK].astype(q.dtype)
    p_ctx = p[..., 3 * ATTN_BLOCK:].astype(q.dtype)
    o = (jnp.einsum('bnhgqk,bnkhd->bnqhgd', p_loc, vw)
         + jnp.einsum('bnhgqc,bchd->bnqhgd', p_ctx, v_ctx))
    return o.reshape(B, N, ATTN_WIDTH)


def context_attention(q, k, v, sink):
    B, C = q.shape[:2]
    qg = q.reshape(B, C, N_KV_HEADS, Q_PER_KV, HEAD_DIM)
    s = jnp.einsum('bqhgd,bkhd->bhgqk', qg, k).astype(F32) * HEAD_DIM ** -0.5
    p = sink_softmax(s, sink.reshape(N_KV_HEADS, Q_PER_KV)[None, :, :, None, None]).astype(q.dtype)
    return jnp.einsum('bhgqk,bkhd->bqhgd', p, v).reshape(B, C, ATTN_WIDTH)


def hyena_filter_spectrum(length, w1, b1, freq, w2, b2, w3):
    t = jnp.linspace(0.0, 1.0, length, dtype=F32)[:, None]
    w = 2.0 * math.pi * jnp.arange(length, dtype=F32)[:, None] / length
    f = jnp.linspace(1e-4, HYENA_BANDS - 1, HYENA_BANDS, dtype=F32)[None, :]
    z = jnp.concatenate([t, jnp.cos(f * w), -jnp.sin(f * w)], axis=-1)
    fr = freq.astype(F32)
    h = jnp.sin(fr * (z @ w1.astype(F32) + b1.astype(F32)))
    h = jnp.sin(fr * (h @ w2.astype(F32) + b2.astype(F32)))
    h = h @ w3.astype(F32)
    deltas = jnp.abs(jnp.linspace(math.log(HYENA_DECAY_TARGET) / HYENA_FAST_DECAY,
                                  math.log(HYENA_DECAY_TARGET) / HYENA_SLOW_DECAY,
                                  HYENA_WIDTH, dtype=F32))
    window = jnp.exp(-t * deltas[None, :])
    h = h.reshape(length, HYENA_ORDER, 2, HYENA_WIDTH) * window[:, None, None, :]
    fwd = h[:, :, 0]
    bwd = h[1:, :, 1][::-1]
    taps = jnp.concatenate([fwd, jnp.zeros_like(fwd[:1]), bwd], axis=0)
    taps = taps * lax.rsqrt(jnp.sum(taps * taps, axis=0, keepdims=True) + EPS)
    return jnp.fft.rfft(taps, axis=0)


def fft_long_conv(z, k_f):
    length = z.shape[1]
    zf = jnp.fft.rfft(z, n=2 * length, axis=1)
    return jnp.fft.irfft(zf * k_f[None], n=2 * length, axis=1)[:, :length]


def hyena_branch(u, short_w, short_b, w1, b1, freq, w2, b2, w3, skip):
    length = u.shape[1]
    u = dwconv(u, short_w, short_b)
    v, x1, x2 = jnp.split(u, 3, axis=-1)
    k_f = hyena_filter_spectrum(length, w1, b1, freq, w2, b2, w3)
    z = v.astype(F32)
    sk = skip.astype(F32)
    for o, gate in enumerate((x1, x2)):
        z = gate.astype(F32) * (fft_long_conv(z, k_f[:, o]) + sk[o] * z)
    return z.astype(u.dtype)


def blockdiag(x, w):
    xb = x.reshape(x.shape[:-1] + (LRU_BLOCKS, LRU_BLOCK))
    return jnp.einsum('blnd,nde->blne', xb, w.astype(x.dtype)).reshape(x.shape)


def linear_scan(a, b, h0, reverse):
    def combine(left, right):
        a_l, b_l = left
        a_r, b_r = right
        return a_l * a_r, a_r * b_l + b_r
    a_cum, b_cum = lax.associative_scan(combine, (a, b), axis=1, reverse=reverse)
    return a_cum * h0[:, None, :] + b_cum


def rg_lru_bidir(xc, wa, ba, wx, bx, lam, h0_f, h0_b):
    xf = xc.astype(F32)
    outs = []
    for d, (h0, rev) in enumerate(((h0_f, False), (h0_b, True))):
        r = jax.nn.sigmoid(blockdiag(xf, wa[d]) + ba[d].astype(F32))
        i = jax.nn.sigmoid(blockdiag(xf, wx[d]) + bx[d].astype(F32))
        log_a = -LRU_C * r * jax.nn.softplus(-lam[d].astype(F32))
        b = jnp.sqrt(-jnp.expm1(2.0 * log_a)) * (i * xf)
        outs.append(linear_scan(jnp.exp(log_a), b, h0, rev))
    return outs[0], outs[1]


def merge_branches(o_att, o_hy, o_lru, gate_logits, w_a, w_h, w_l, w_o):
    g = jax.nn.sigmoid(gate_logits).reshape(gate_logits.shape[:-1] + (N_BRANCH, D_MODEL))
    m = g[..., 0, :] * (o_att @ w_a) + g[..., 1, :] * (o_hy @ w_h) + g[..., 2, :] * (o_lru @ w_l)
    return m @ w_o


def moe_ffn(h, router_w, router_b, w_gate, w_up, w_down):
    t = h.shape[0]
    scores = jax.nn.softmax((h @ router_w).astype(F32) + router_b.astype(F32), axis=-1)
    grouped = scores.reshape(t, N_GROUPS, EXPERTS_PER_GROUP)
    best_group = jnp.argmax(jnp.max(grouped, axis=-1), axis=-1)
    in_group = jnp.take_along_axis(grouped, best_group[:, None, None], axis=1)[:, 0]
    top_w, top_i = lax.top_k(in_group, TOP_K)
    top_w = top_w / jnp.sum(top_w, axis=-1, keepdims=True)
    expert_id = best_group[:, None] * EXPERTS_PER_GROUP + top_i
    combine = jnp.sum(jax.nn.one_hot(expert_id, N_EXPERTS, dtype=F32) * top_w[..., None], axis=1)
    y = jnp.zeros_like(h)
    for e in range(N_EXPERTS):
        act = jax.nn.silu(h @ w_gate[e]) * (h @ w_up[e])
        y = y + combine[:, e:e + 1].astype(h.dtype) * (act @ w_down[e])
    return y


def setup_inputs(seed: int = 0) -> dict:
    key = jax.random.key(seed)
    ks = iter(jax.random.split(key, 40))

    def nrm(shape, scale):
        return scale * jax.random.normal(next(ks), shape, F32)

    D = D_MODEL
    FH = HYENA_FILTER_HIDDEN
    a_pow_c = jax.random.uniform(next(ks), (DEPTH, 2, LRU_WIDTH), F32, 0.9, 0.999)
    a_init = a_pow_c ** (1.0 / LRU_C)
    return {
        'x': nrm((BATCH, SEQ, D), 1.0),
        'c': nrm((BATCH, D), 1.0),
        'ctx': nrm((BATCH, CTX_LEN, D), 1.0),
        'c_ctx': nrm((D,), 1.0),
        'w_ada': nrm((DEPTH, D, 6 * D), 0.5 * D ** -0.5),
        'b_ada': nrm((DEPTH, 6 * D), 0.02),
        'g_mix': 1.0 + nrm((DEPTH, D), 0.02),
        'g_ffn': 1.0 + nrm((DEPTH, D), 0.02),
        'w_in': nrm((DEPTH, D, D_IN), D ** -0.5),
        'attn_sink': nrm((DEPTH, N_Q_HEADS), 0.5),
        'hy_short_w': nrm((DEPTH, HYENA_SHORT_W, 3 * HYENA_WIDTH), HYENA_SHORT_W ** -0.5),
        'hy_short_b': nrm((DEPTH, 3 * HYENA_WIDTH), 0.02),
        'hy_w1': nrm((DEPTH, HYENA_EMB, FH), HYENA_EMB ** -0.5),
        'hy_b1': nrm((DEPTH, FH), 0.02),
        'hy_freq': 1.0 + nrm((DEPTH, FH), 0.1),
        'hy_w2': nrm((DEPTH, FH, FH), FH ** -0.5),
        'hy_b2': nrm((DEPTH, FH), 0.02),
        'hy_w3': nrm((DEPTH, FH, HYENA_ORDER * 2 * HYENA_WIDTH), FH ** -0.5),
        'hy_skip': nrm((DEPTH, HYENA_ORDER, HYENA_WIDTH), 0.5),
        'lru_conv_w': nrm((DEPTH, LRU_CONV_W, LRU_WIDTH), LRU_CONV_W ** -0.5),
        'lru_conv_b': nrm((DEPTH, LRU_WIDTH), 0.02),
        'lru_wa': nrm((DEPTH, 2, LRU_BLOCKS, LRU_BLOCK, LRU_BLOCK), LRU_BLOCK ** -0.5),
        'lru_ba': nrm((DEPTH, 2, LRU_WIDTH), 0.02),
        'lru_wx': nrm((DEPTH, 2, LRU_BLOCKS, LRU_BLOCK, LRU_BLOCK), LRU_BLOCK ** -0.5),
        'lru_bx': nrm((DEPTH, 2, LRU_WIDTH), 0.02),
        'lru_lambda': jnp.log(a_init) - jnp.log1p(-a_init),
        'w_br_attn': nrm((DEPTH, ATTN_WIDTH, D), ATTN_WIDTH ** -0.5),
        'w_br_hy': nrm((DEPTH, HYENA_WIDTH, D), HYENA_WIDTH ** -0.5),
        'w_br_lru': nrm((DEPTH, LRU_WIDTH, D), LRU_WIDTH ** -0.5),
        'w_out': nrm((DEPTH, D, D), D ** -0.5),
        'router_w': nrm((D, N_EXPERTS), D ** -0.5),
        'router_b': nrm((N_EXPERTS,), 0.01),
        'exp_w_gate': nrm((DEPTH, N_EXPERTS, D, D_EXPERT), D ** -0.5),
        'exp_w_up': nrm((DEPTH, N_EXPERTS, D, D_EXPERT), D ** -0.5),
        'exp_w_down': nrm((DEPTH, N_EXPERTS, D_EXPERT, D), D_EXPERT ** -0.5),
        'final_g': 1.0 + nrm((D,), 0.02),
    }


def reference(x, c, ctx, c_ctx, w_ada, b_ada, g_mix, g_ffn, w_in, attn_sink,
              hy_short_w, hy_short_b, hy_w1, hy_b1, hy_freq, hy_w2, hy_b2, hy_w3, hy_skip,
              lru_conv_w, lru_conv_b, lru_wa, lru_ba, lru_wx, lru_bx, lru_lambda,
              w_br_attn, w_br_hy, w_br_lru, w_out, router_w, router_b,
              exp_w_gate, exp_w_up, exp_w_down, final_g):
    B, N, D = x.shape
    C = ctx.shape[1]
    cos, sin = axial_rope_tables(N)
    sc = jax.nn.silu(c)
    scc = jax.nn.silu(c_ctx)
    h_zero = jnp.zeros((B, LRU_WIDTH), F32)
    x_lat, x_ctx = x, ctx
    for l in range(DEPTH):
        last = l == DEPTH - 1
        ml = [m[:, None, :] for m in jnp.split(sc @ w_ada[l] + b_ada[l], 6, axis=-1)]
        mc = jnp.split(scc @ w_ada[l] + b_ada[l], 6, axis=-1)
        hy_p = (hy_short_w[l], hy_short_b[l], hy_w1[l], hy_b1[l], hy_freq[l],
                hy_w2[l], hy_b2[l], hy_w3[l], hy_skip[l])
        lru_p = (lru_wa[l], lru_ba[l], lru_wx[l], lru_bx[l], lru_lambda[l])
        br_p = (w_br_attn[l], w_br_hy[l], w_br_lru[l], w_out[l])

        hc = modulate(rmsnorm(x_ctx, g_mix[l]), mc[0], mc[1])
        pc = hc @ (w_in[l][:, :CTX_STATE_COLS] if last else w_in[l])
        k_c, v_c, xr_c = split_state(pc)
        k_c = k_c.reshape(B, C, N_KV_HEADS, HEAD_DIM)
        v_c = v_c.reshape(B, C, N_KV_HEADS, HEAD_DIM)
        hf_c, hb_c = rg_lru_bidir(dwconv(xr_c, lru_conv_w[l], lru_conv_b[l]), *lru_p, h_zero, h_zero)

        hl = modulate(rmsnorm(x_lat, g_mix[l]), ml[0], ml[1])
        pl = hl @ w_in[l]
        k_l, v_l, xr_l = split_state(pl)
        q_l, hy_l, y_l, gate_l = split_rest(pl)
        q_l = apply_rope(q_l.reshape(B, N, N_Q_HEADS, HEAD_DIM), cos, sin)
        k_l = apply_rope(k_l.reshape(B, N, N_KV_HEADS, HEAD_DIM), cos, sin)
        v_l = v_l.reshape(B, N, N_KV_HEADS, HEAD_DIM)
        o_att = latent_attention(q_l, k_l, v_l, k_c, v_c, attn_sink[l])
        o_hy = hyena_branch(hy_l, *hy_p)
        hf_l, hb_l = rg_lru_bidir(dwconv(xr_l, lru_conv_w[l], lru_conv_b[l]), *lru_p,
                                  hf_c[:, -1], hb_c[:, 0])
        o_lru = jax.nn.gelu(y_l, approximate=True) * (hf_l + hb_l).astype(y_l.dtype)
        x_lat = x_lat + ml[2] * merge_branches(o_att, o_hy, o_lru, gate_l, *br_p)

        if not last:
            q_c, hy_c, y_c, gate_c = split_rest(pc)
            o_att_c = context_attention(q_c.reshape(B, C, N_Q_HEADS, HEAD_DIM), k_c, v_c, attn_sink[l])
            o_hy_c = hyena_branch(hy_c, *hy_p)
            o_lru_c = jax.nn.gelu(y_c, approximate=True) * (hf_c + hb_c).astype(y_c.dtype)
            x_ctx = x_ctx + mc[2] * merge_branches(o_att_c, o_hy_c, o_lru_c, gate_c, *br_p)

        moe_p = (router_w, router_b, exp_w_gate[l], exp_w_up[l], exp_w_down[l])
        fl = modulate(rmsnorm(x_lat, g_ffn[l]), ml[3], ml[4]).reshape(B * N, D)
        if last:
            x_lat = x_lat + ml[5] * moe_ffn(fl, *moe_p).reshape(B, N, D)
        else:
            fc = modulate(rmsnorm(x_ctx, g_ffn[l]), mc[3], mc[4]).reshape(B * C, D)
            y = moe_ffn(jnp.concatenate([fl, fc], axis=0), *moe_p)
            x_lat = x_lat + ml[5] * y[:B * N].reshape(B, N, D)
            x_ctx = x_ctx + mc[5] * y[B * N:].reshape(B, C, D)
    return rmsnorm(x_lat, final_g)
```

```python
import functools
import math

import jax
import jax.numpy as jnp
import numpy as np
from jax import lax
from jax.experimental import pallas as pl
from jax.experimental.pallas import tpu as pltpu

F32 = jnp.float32
BF16 = jnp.bfloat16
HIGHEST = lax.Precision.HIGHEST

LANES_V7X = 128
VMEM_LIMIT_V7X = 56 * 1024 * 1024

EPS = 1e-6
GRID_W = 64
HEAD_DIM = 128
N_Q_HEADS = 8
N_KV_HEADS = 2
Q_PER_KV = N_Q_HEADS // N_KV_HEADS
WINDOW = 128
ATTN_BLOCK = 128
ROPE_THETA = 10000.0
HYENA_BANDS = 16
HYENA_DECAY_TARGET = 1e-2
HYENA_FAST_DECAY = 0.3
HYENA_SLOW_DECAY = 1.5
LRU_C = 8.0
N_EXPERTS = 16
N_GROUPS = 4
EXPERTS_PER_GROUP = N_EXPERTS // N_GROUPS
N_MOD = 6
MOD_ROWS = 8


def _cparams(n_axes):
    return pltpu.CompilerParams(dimension_semantics=("arbitrary",) * n_axes,
                                vmem_limit_bytes=VMEM_LIMIT_V7X)


def _pick(n, cap):
    t = cap
    while n % t:
        t //= 2
    return t


def _ada_kernel(c_ref, w_ref, b_ref, o_ref):
    c = c_ref[...]
    s = (c * jax.nn.sigmoid(c)).astype(BF16)
    o_ref[0] = jnp.dot(s, w_ref[0].astype(BF16), preferred_element_type=F32) + b_ref[0]


def ada_tables(cc, w_ada, b_ada):
    depth, d, n6 = w_ada.shape
    tn = _pick(n6, 1024)
    return pl.pallas_call(
        _ada_kernel,
        grid=(depth, n6 // tn),
        in_specs=[pl.BlockSpec((MOD_ROWS, d), lambda l, j: (0, 0)),
                  pl.BlockSpec((1, d, tn), lambda l, j: (l, 0, j)),
                  pl.BlockSpec((1, 1, tn), lambda l, j: (l, 0, j))],
        out_specs=pl.BlockSpec((1, MOD_ROWS, tn), lambda l, j: (l, 0, j)),
        out_shape=jax.ShapeDtypeStruct((depth, MOD_ROWS, n6), F32),
        compiler_params=_cparams(2), name="ada_tables",
    )(cc, w_ada, b_ada.reshape(depth, 1, n6))


def _mod_spec(which, tm, n_lat, batch, d):
    return pl.BlockSpec((1, 1, 1, d),
                        lambda i, *_: (jnp.minimum((i * tm) // n_lat, batch), which, 0, 0))


def _normmod(x, g, shift, scale):
    y = x * lax.rsqrt(jnp.mean(x * x, axis=-1, keepdims=True) + EPS) * g
    return y * (1.0 + scale) + shift


def _normmod_kernel(x_ref, g_ref, sh_ref, sc_ref, o_ref):
    o_ref[...] = _normmod(x_ref[...], g_ref[...], sh_ref[0, 0], sc_ref[0, 0]).astype(o_ref.dtype)


def normmod(x, g, mods, which_shift, n_lat, batch, tm):
    t, d = x.shape
    return pl.pallas_call(
        _normmod_kernel,
        grid=(t // tm,),
        in_specs=[pl.BlockSpec((tm, d), lambda i: (i, 0)),
                  pl.BlockSpec((1, d), lambda i: (0, 0)),
                  _mod_spec(which_shift, tm, n_lat, batch, d),
                  _mod_spec(which_shift + 1, tm, n_lat, batch, d)],
        out_specs=pl.BlockSpec((tm, d), lambda i: (i, 0)),
        out_shape=jax.ShapeDtypeStruct((t, d), BF16),
        compiler_params=_cparams(1), name="normmod",
    )(x, g.reshape(1, d), mods, mods)


def _final_norm_kernel(x_ref, g_ref, o_ref):
    x = x_ref[...]
    o_ref[...] = x * lax.rsqrt(jnp.mean(x * x, axis=-1, keepdims=True) + EPS) * g_ref[...]


def final_norm(x, g, rows, tm):
    d = x.shape[1]
    return pl.pallas_call(
        _final_norm_kernel,
        grid=(rows // tm,),
        in_specs=[pl.BlockSpec((tm, d), lambda i: (i, 0)), pl.BlockSpec((1, d), lambda i: (0, 0))],
        out_specs=pl.BlockSpec((tm, d), lambda i: (i, 0)),
        out_shape=jax.ShapeDtypeStruct((rows, d), F32),
        compiler_params=_cparams(1), name="final_norm",
    )(x, g.reshape(1, d))


def _proj_kernel(a_ref, w_ref, o_ref, wb_ref):
    @pl.when(pl.program_id(1) == 0)
    def _():
        wb_ref[...] = w_ref[0].astype(BF16)

    o_ref[...] = jnp.dot(a_ref[...], wb_ref[...], preferred_element_type=F32)


def in_proj(a, w, layer, tm):
    t, k = a.shape
    n = w.shape[2]
    tn = _pick(n, 1024)
    return pl.pallas_call(
        _proj_kernel,
        grid=(n // tn, t // tm),
        in_specs=[pl.BlockSpec((tm, k), lambda j, i: (i, 0)),
                  pl.BlockSpec((1, k, tn), lambda j, i: (layer, 0, j))],
        out_specs=pl.BlockSpec((tm, tn), lambda j, i: (i, j)),
        out_shape=jax.ShapeDtypeStruct((t, n), F32),
        scratch_shapes=[pltpu.VMEM((k, tn), BF16)],
        compiler_params=_cparams(2), name="in_proj",
    )(a, w)


def _rope(x, tab):
    return x * tab[:, :HEAD_DIM] + pltpu.roll(x, HEAD_DIM // 2, axis=1) * tab[:, HEAD_DIM:]


def _sink_column(sink_ref, h, rows):
    r = lax.broadcasted_iota(jnp.int32, (Q_PER_KV * rows, 1), 0)
    col = jnp.full((Q_PER_KV * rows, 1), sink_ref[h * Q_PER_KV], F32)
    for g in range(1, Q_PER_KV):
        col = jnp.where(r >= g * rows, sink_ref[h * Q_PER_KV + g], col)
    return col


def _nt(a, b):
    return lax.dot_general(a, b, (((1,), (1,)), ((), ())), preferred_element_type=F32)


def _lat_attn_kernel(sink_ref, q_ref, kvm_ref, kv0_ref, kvp_ref, kvc_ref, tm_ref, t0_ref, tp_ref, o_ref,
                     *, n_lat):
    i = pl.program_id(1)
    blk = ATTN_BLOCK
    kv_w = N_KV_HEADS * HEAD_DIM
    scale = HEAD_DIM ** -0.5
    tabs = (tm_ref[...], t0_ref[...], tp_ref[...])
    kvs = (kvm_ref, kv0_ref, kvp_ref)
    qi = lax.broadcasted_iota(jnp.int32, (Q_PER_KV * blk, 3 * blk), 0) & (blk - 1)
    kj = lax.broadcasted_iota(jnp.int32, (Q_PER_KV * blk, 3 * blk), 1)
    key_pos = (i - 1) * blk + kj
    ok = (jnp.abs(qi + blk - kj) <= WINDOW) & (key_pos >= 0) & (key_pos < n_lat)
    for h in range(N_KV_HEADS):
        ks = slice(h * HEAD_DIM, (h + 1) * HEAD_DIM)
        vs = slice(kv_w + h * HEAD_DIM, kv_w + (h + 1) * HEAD_DIM)
        k_loc = jnp.concatenate([_rope(kvs[n][:, ks], tabs[n]) for n in range(3)], axis=0).astype(BF16)
        v_loc = jnp.concatenate([kvs[n][:, vs] for n in range(3)], axis=0).astype(BF16)
        k_ctx = kvc_ref[:, ks].astype(BF16)
        v_ctx = kvc_ref[:, vs].astype(BF16)
        q4 = jnp.concatenate(
            [_rope(q_ref[:, (h * Q_PER_KV + g) * HEAD_DIM:(h * Q_PER_KV + g + 1) * HEAD_DIM], tabs[1])
             for g in range(Q_PER_KV)], axis=0).astype(BF16)
        s_loc = jnp.where(ok, _nt(q4, k_loc) * scale, -jnp.inf)
        s_ctx = _nt(q4, k_ctx) * scale
        sink = _sink_column(sink_ref, h, blk)
        m = jnp.maximum(jnp.maximum(jnp.max(s_loc, axis=-1, keepdims=True),
                                    jnp.max(s_ctx, axis=-1, keepdims=True)), sink)
        p_loc = jnp.exp(s_loc - m)
        p_ctx = jnp.exp(s_ctx - m)
        den = (jnp.sum(p_loc, axis=-1, keepdims=True) + jnp.sum(p_ctx, axis=-1, keepdims=True)
               + jnp.exp(sink - m))
        o = (jnp.dot(p_loc.astype(BF16), v_loc, preferred_element_type=F32)
             + jnp.dot(p_ctx.astype(BF16), v_ctx, preferred_element_type=F32)) / den
        for g in range(Q_PER_KV):
            hq = h * Q_PER_KV + g
            o_ref[:, hq * HEAD_DIM:(hq + 1) * HEAD_DIM] = o[g * blk:(g + 1) * blk].astype(o_ref.dtype)


def latent_attention(p, sink, rope_tab, batch, n_lat, n_ctx, q_off):
    blk = ATTN_BLOCK
    nb = n_lat // blk
    aw = N_Q_HEADS * HEAD_DIM
    kvw = 2 * N_KV_HEADS * HEAD_DIM
    ctx_blk0 = (batch * n_lat) // n_ctx
    kv_spec = lambda off: pl.BlockSpec(
        (blk, kvw), lambda b, i: (b * nb + jnp.clip(i + off, 0, nb - 1), 0))
    tab_spec = lambda off: pl.BlockSpec(
        (blk, 2 * HEAD_DIM), lambda b, i: (jnp.clip(i + off, 0, nb - 1), 0))
    return pl.pallas_call(
        functools.partial(_lat_attn_kernel, n_lat=n_lat),
        grid=(batch, nb),
        in_specs=[pl.BlockSpec(memory_space=pltpu.SMEM),
                  pl.BlockSpec((blk, aw), lambda b, i: (b * nb + i, q_off // aw)),
                  kv_spec(-1), kv_spec(0), kv_spec(1),
                  pl.BlockSpec((n_ctx, kvw), lambda b, i: (ctx_blk0 + b, 0)),
                  tab_spec(-1), tab_spec(0), tab_spec(1)],
        out_specs=pl.BlockSpec((blk, aw), lambda b, i: (b * nb + i, 0)),
        out_shape=jax.ShapeDtypeStruct((batch * n_lat, aw), BF16),
        compiler_params=_cparams(2), name="latent_attention",
    )(sink, p, p, p, p, p, rope_tab, rope_tab, rope_tab)


def _ctx_attn_kernel(sink_ref, q_ref, kv_ref, o_ref):
    rows = q_ref.shape[0]
    kv_w = N_KV_HEADS * HEAD_DIM
    scale = HEAD_DIM ** -0.5
    for h in range(N_KV_HEADS):
        k = kv_ref[:, h * HEAD_DIM:(h + 1) * HEAD_DIM].astype(BF16)
        v = kv_ref[:, kv_w + h * HEAD_DIM:kv_w + (h + 1) * HEAD_DIM].astype(BF16)
        q4 = jnp.concatenate(
            [q_ref[:, (h * Q_PER_KV + g) * HEAD_DIM:(h * Q_PER_KV + g + 1) * HEAD_DIM]
             for g in range(Q_PER_KV)], axis=0).astype(BF16)
        s = _nt(q4, k) * scale
        sink = _sink_column(sink_ref, h, rows)
        m = jnp.maximum(jnp.max(s, axis=-1, keepdims=True), sink)
        p = jnp.exp(s - m)
        den = jnp.sum(p, axis=-1, keepdims=True) + jnp.exp(sink - m)
        o = jnp.dot(p.astype(BF16), v, preferred_element_type=F32) / den
        for g in range(Q_PER_KV):
            hq = h * Q_PER_KV + g
            o_ref[:, hq * HEAD_DIM:(hq + 1) * HEAD_DIM] = o[g * rows:(g + 1) * rows].astype(o_ref.dtype)


def context_attention(p, sink, batch, n_lat, n_ctx, q_off):
    aw = N_Q_HEADS * HEAD_DIM
    kvw = 2 * N_KV_HEADS * HEAD_DIM
    ctx_blk0 = (batch * n_lat) // n_ctx
    return pl.pallas_call(
        _ctx_attn_kernel,
        grid=(batch,),
        in_specs=[pl.BlockSpec(memory_space=pltpu.SMEM),
                  pl.BlockSpec((n_ctx, aw), lambda b: (ctx_blk0 + b, q_off // aw)),
                  pl.BlockSpec((n_ctx, kvw), lambda b: (ctx_blk0 + b, 0))],
        out_specs=pl.BlockSpec((n_ctx, aw), lambda b: (b, 0)),
        out_shape=jax.ShapeDtypeStruct((batch * n_ctx, aw), BF16),
        compiler_params=_cparams(1), name="context_attention",
    )(sink, p, p)


def _dwconv_kernel(u_ref, w_ref, b_ref, o_ref, *, width):
    x = u_ref[...]
    length = x.shape[0]
    left = width // 2
    t = lax.broadcasted_iota(jnp.int32, (length, 1), 0)
    acc = jnp.broadcast_to(b_ref[...], x.shape)
    for j in range(width):
        s = j - left
        if s == 0:
            xs = x
        else:
            xs = pltpu.roll(x, (-s) % length, axis=0)
            xs = jnp.where((t + s >= 0) & (t + s < length), xs, 0.0)
        acc = acc + xs * w_ref[j:j + 1, :]
    o_ref[...] = acc


def dwconv(p, w, b, n_seq, length, row0, col0, n_ch):
    width = w.shape[0]
    cb = _pick(n_ch, 256)
    return pl.pallas_call(
        functools.partial(_dwconv_kernel, width=width),
        grid=(n_seq, n_ch // cb),
        in_specs=[pl.BlockSpec((length, cb), lambda s, c: (row0 // length + s, col0 // cb + c)),
                  pl.BlockSpec((width, cb), lambda s, c: (0, c)),
                  pl.BlockSpec((1, cb), lambda s, c: (0, c))],
        out_specs=pl.BlockSpec((length, cb), lambda s, c: (s, c)),
        out_shape=jax.ShapeDtypeStruct((n_seq * length, n_ch), F32),
        compiler_params=_cparams(2), name="dwconv",
    )(p, w, b.reshape(1, n_ch))


def _lru_kernel(xf_ref, xb_ref, w_ref, bias_ref, sp_ref, h0_ref, hf_ref, hb_ref, hl_ref,
                af, bf, ab, bb, carry):
    j = pl.program_id(1)
    width = xf_ref.shape[1]

    @pl.when(j == 0)
    def _():
        carry[...] = h0_ref[0]

    def gates(x, d):
        g = jnp.dot(x.astype(BF16), w_ref[d], preferred_element_type=F32) + bias_ref[d]
        r = jax.nn.sigmoid(g[:, :width])
        gi = jax.nn.sigmoid(g[:, width:])
        a = jnp.exp(-LRU_C * r * sp_ref[d])
        return a, jnp.sqrt(1.0 - a * a) * (gi * x)

    af[...], bf[...] = gates(xf_ref[...], 0)
    ab[...], bb[...] = gates(xb_ref[...], 1)
    tl = af.shape[0]

    def body(t, hs):
        hf, hb = hs
        hf = af[pl.ds(t, 1), :] * hf + bf[pl.ds(t, 1), :]
        hf_ref[pl.ds(t, 1), :] = hf
        tb = tl - 1 - t
        hb = ab[pl.ds(tb, 1), :] * hb + bb[pl.ds(tb, 1), :]
        hb_ref[pl.ds(tb, 1), :] = hb
        return hf, hb

    hf, hb = lax.fori_loop(0, tl, body, (carry[0:1, :], carry[1:2, :]), unroll=8)
    carry[0:1, :] = hf
    carry[1:2, :] = hb
    hl_ref[0] = carry[...]


def rg_lru(xc, w_gates, b_gates, softplus_neg_lam, h0, n_seq, length):
    width = xc.shape[1]
    tl = _pick(length, 512)
    nc = length // tl
    return pl.pallas_call(
        _lru_kernel,
        grid=(n_seq, nc),
        in_specs=[pl.BlockSpec((tl, width), lambda s, j: (s * nc + j, 0)),
                  pl.BlockSpec((tl, width), lambda s, j: (s * nc + nc - 1 - j, 0)),
                  pl.BlockSpec((2, width, 2 * width), lambda s, j: (0, 0, 0)),
                  pl.BlockSpec((2, 1, 2 * width), lambda s, j: (0, 0, 0)),
                  pl.BlockSpec((2, 1, width), lambda s, j: (0, 0, 0)),
                  pl.BlockSpec((1, 2, width), lambda s, j: (s, 0, 0))],
        out_specs=[pl.BlockSpec((tl, width), lambda s, j: (s * nc + j, 0)),
                   pl.BlockSpec((tl, width), lambda s, j: (s * nc + nc - 1 - j, 0)),
                   pl.BlockSpec((1, 2, width), lambda s, j: (s, 0, 0))],
        out_shape=[jax.ShapeDtypeStruct((n_seq * length, width), F32),
                   jax.ShapeDtypeStruct((n_seq * length, width), F32),
                   jax.ShapeDtypeStruct((n_seq, 2, width), F32)],
        scratch_shapes=[pltpu.VMEM((tl, width), F32)] * 4 + [pltpu.VMEM((2, width), F32)],
        compiler_params=_cparams(2), name="rg_lru",
    )(xc, xc, w_gates, b_gates, softplus_neg_lam, h0)


def _softplus_kernel(x_ref, o_ref):
    x = -x_ref[...]
    o_ref[...] = jnp.maximum(x, 0.0) + jnp.log(1.0 + jnp.exp(-jnp.abs(x)))


def softplus_neg(lam):
    depth, two, width = lam.shape
    x = lam.reshape(depth * two, width)
    out = pl.pallas_call(
        _softplus_kernel,
        out_shape=jax.ShapeDtypeStruct(x.shape, F32), name="softplus_neg",
    )(x)
    return out.reshape(depth, two, 1, width)


def _hy_hidden_kernel(z_ref, w1_ref, b1_ref, fr_ref, w2_ref, b2_ref, o_ref):
    fr = fr_ref[...]
    h = jnp.sin(fr * (jnp.dot(z_ref[...], w1_ref[...], precision=HIGHEST,
                              preferred_element_type=F32) + b1_ref[...]))
    o_ref[...] = jnp.sin(fr * (jnp.dot(h, w2_ref[...], precision=HIGHEST,
                                       preferred_element_type=F32) + b2_ref[...]))


def hyena_hidden(z, w1, b1, freq, w2, b2):
    length = z.shape[0]
    fh = w2.shape[0]
    return pl.pallas_call(
        _hy_hidden_kernel,
        out_shape=jax.ShapeDtypeStruct((length, fh), F32), name="hyena_hidden",
    )(z, w1, b1.reshape(1, fh), freq.reshape(1, fh), w2, b2.reshape(1, fh))


def _hy_taps_kernel(hid_ref, wf_ref, wb_ref, delta_ref, u_ref, kn_ref):
    hid = hid_ref[...]
    length = hid.shape[0]
    hf = jnp.dot(hid, wf_ref[...], precision=HIGHEST, preferred_element_type=F32)
    hb = jnp.dot(hid, wb_ref[...], precision=HIGHEST, preferred_element_type=F32)
    ti = lax.broadcasted_iota(jnp.int32, (length, 1), 0)
    win = jnp.exp(-(ti.astype(F32) * (1.0 / (length - 1))) * delta_ref[...])
    f = hf * win
    b = jnp.where(ti >= 1, hb * win, 0.0)
    sc = lax.rsqrt(jnp.sum(f * f + b * b, axis=0, keepdims=True) + EPS)
    u1 = (f + b) * sc
    u2 = (f - b) * sc
    sign = jnp.where((ti & 1) == 0, 1.0, -1.0)
    kn_ref[...] = jnp.sum(u1 * sign, axis=0, keepdims=True)
    u_ref[0] = u1.astype(u_ref.dtype)
    u_ref[1] = u2.astype(u_ref.dtype)


def hyena_taps(hid, w3, deltas):
    length, fh = hid.shape
    hw = deltas.shape[1]
    n_ord = w3.shape[1] // (2 * hw)
    cw = _pick(hw, 128)
    nc = hw // cw
    return pl.pallas_call(
        _hy_taps_kernel,
        grid=(n_ord, nc),
        in_specs=[pl.BlockSpec((length, fh), lambda o, c: (0, 0)),
                  pl.BlockSpec((fh, cw), lambda o, c: (0, o * 2 * nc + c)),
                  pl.BlockSpec((fh, cw), lambda o, c: (0, o * 2 * nc + nc + c)),
                  pl.BlockSpec((1, cw), lambda o, c: (0, c))],
        out_specs=[pl.BlockSpec((2, length, cw), lambda o, c: (0, 0, o * nc + c)),
                   pl.BlockSpec((1, cw), lambda o, c: (0, o * nc + c))],
        out_shape=[jax.ShapeDtypeStruct((2, length, n_ord * hw), BF16),
                   jax.ShapeDtypeStruct((1, n_ord * hw), F32)],
        compiler_params=_cparams(2), name="hyena_taps",
    )(hid, w3, w3, deltas)


def _spectrum_kernel(fw_ref, u_ref, o_ref):
    o_ref[0] = jnp.dot(fw_ref[0], u_ref[0], preferred_element_type=F32)


def hyena_spectrum(fw, u):
    _, length, n = u.shape
    tm = _pick(length, 512)
    return pl.pallas_call(
        _spectrum_kernel,
        grid=(2, length // tm),
        in_specs=[pl.BlockSpec((1, tm, length), lambda h, i: (h, i, 0)),
                  pl.BlockSpec((1, length, n), lambda h, i: (h, 0, 0))],
        out_specs=pl.BlockSpec((1, tm, n), lambda h, i: (h, i, 0)),
        out_shape=jax.ShapeDtypeStruct((2, length, n), F32),
        compiler_params=_cparams(2), name="hyena_spectrum",
    )(fw, u)


def _conv_fwd_kernel(fw_ref, z_ref, k_ref, kn_ref, y_ref, zb_ref):
    i = pl.program_id(1)

    @pl.when(i == 0)
    def _():
        zb_ref[...] = z_ref[...].astype(BF16)

    z = zb_ref[...]
    zr = jnp.dot(fw_ref[0], z, preferred_element_type=F32)
    zi = jnp.dot(fw_ref[1], z, preferred_element_type=F32)
    kr = k_ref[0]
    ki = k_ref[1]
    yr = zr * kr - zi * ki
    yi = zr * ki + zi * kr
    first = (lax.broadcasted_iota(jnp.int32, (zr.shape[0], 1), 0) == 0) & (i == 0)
    y_ref[0, 0] = jnp.where(first, 0.5 * zr * kr, yr).astype(y_ref.dtype)
    y_ref[0, 1] = jnp.where(first, 0.5 * zi * kn_ref[...], yi).astype(y_ref.dtype)


def conv_forward(fw, z, col0, kspec, kn, order, n_seq, length, hw):
    tm = _pick(length, 512)
    return pl.pallas_call(
        _conv_fwd_kernel,
        grid=(n_seq, length // tm),
        in_specs=[pl.BlockSpec((2, tm, length), lambda s, i: (0, i, 0)),
                  pl.BlockSpec((length, hw), lambda s, i: (s, col0 // hw)),
                  pl.BlockSpec((2, tm, hw), lambda s, i: (0, i, order)),
                  pl.BlockSpec((1, hw), lambda s, i: (0, order))],
        out_specs=pl.BlockSpec((1, 2, tm, hw), lambda s, i: (s, 0, i, 0)),
        out_shape=jax.ShapeDtypeStruct((n_seq, 2, length, hw), BF16),
        scratch_shapes=[pltpu.VMEM((length, hw), BF16)],
        compiler_params=_cparams(2), name="hyena_conv_fwd",
    )(fw, z, kspec, kn)


def _conv_inv_kernel(g_ref, y_ref, z_ref, gate_ref, skip_ref, o_ref, *, inv_len):
    y = jnp.dot(g_ref[...], y_ref[0], preferred_element_type=F32) * inv_len
    o_ref[...] = (gate_ref[...] * (y + skip_ref[0] * z_ref[...])).astype(o_ref.dtype)


def conv_inverse(g, yp, z, z_col0, gates, gate_col0, skip, order, n_seq, length, out_dtype):
    hw = yp.shape[3]
    tm = _pick(length, 512)
    nt = length // tm
    return pl.pallas_call(
        functools.partial(_conv_inv_kernel, inv_len=1.0 / length),
        grid=(n_seq, nt),
        in_specs=[pl.BlockSpec((tm, 2 * length), lambda s, i: (i, 0)),
                  pl.BlockSpec((1, 2 * length, hw), lambda s, i: (s, 0, 0)),
                  pl.BlockSpec((tm, hw), lambda s, i: (s * nt + i, z_col0 // hw)),
                  pl.BlockSpec((tm, hw), lambda s, i: (s * nt + i, gate_col0 // hw)),
                  pl.BlockSpec((1, 1, hw), lambda s, i: (order, 0, 0))],
        out_specs=pl.BlockSpec((tm, hw), lambda s, i: (s * nt + i, 0)),
        out_shape=jax.ShapeDtypeStruct((n_seq * length, hw), out_dtype),
        compiler_params=_cparams(2), name="hyena_conv_inv",
    )(g, yp.reshape(n_seq, 2 * length, hw), z, gates, skip.reshape(skip.shape[0], 1, hw))


def _merge_kernel(oa_ref, oh_ref, y_ref, hf_ref, hb_ref, g0_ref, g1_ref, g2_ref,
                  wa_ref, wh_ref, wl_ref, o_ref):
    ol = (jax.nn.gelu(y_ref[...], approximate=True) * (hf_ref[...] + hb_ref[...])).astype(BF16)
    m = (jax.nn.sigmoid(g0_ref[...]) * jnp.dot(oa_ref[...], wa_ref[...], preferred_element_type=F32)
         + jax.nn.sigmoid(g1_ref[...]) * jnp.dot(oh_ref[...], wh_ref[...], preferred_element_type=F32)
         + jax.nn.sigmoid(g2_ref[...]) * jnp.dot(ol, wl_ref[...], preferred_element_type=F32))
    o_ref[...] = m.astype(o_ref.dtype)


def merge_branches(o_att, o_hy, p, hf, hb, wa, wh, wl, lru_y_off, gate_off, tm):
    t, aw = o_att.shape
    hw = o_hy.shape[1]
    lw = hf.shape[1]
    d = wa.shape[1]
    tn = _pick(d, 512)
    gate_spec = lambda k: pl.BlockSpec((tm, tn), lambda i, j: (i, (gate_off + k * d) // tn + j))
    return pl.pallas_call(
        _merge_kernel,
        grid=(t // tm, d // tn),
        in_specs=[pl.BlockSpec((tm, aw), lambda i, j: (i, 0)),
                  pl.BlockSpec((tm, hw), lambda i, j: (i, 0)),
                  pl.BlockSpec((tm, lw), lambda i, j: (i, lru_y_off // lw)),
                  pl.BlockSpec((tm, lw), lambda i, j: (i, 0)),
                  pl.BlockSpec((tm, lw), lambda i, j: (i, 0)),
                  gate_spec(0), gate_spec(1), gate_spec(2),
                  pl.BlockSpec((aw, tn), lambda i, j: (0, j)),
                  pl.BlockSpec((hw, tn), lambda i, j: (0, j)),
                  pl.BlockSpec((lw, tn), lambda i, j: (0, j))],
        out_specs=pl.BlockSpec((tm, tn), lambda i, j: (i, j)),
        out_shape=jax.ShapeDtypeStruct((t, d), BF16),
        compiler_params=_cparams(2), name="merge_branches",
    )(o_att, o_hy, p, hf, hb, p, p, p, wa, wh, wl)


def _out_proj_kernel(m_ref, w_ref, x_ref, gate_ref, o_ref):
    o_ref[...] = x_ref[...] + gate_ref[0, 0] * jnp.dot(m_ref[...], w_ref[...],
                                                       preferred_element_type=F32)


def out_proj_residual(m, w, x, mods, which_gate, n_lat, batch, tm):
    t, d = x.shape
    return pl.pallas_call(
        _out_proj_kernel,
        grid=(t // tm,),
        in_specs=[pl.BlockSpec((tm, d), lambda i: (i, 0)),
                  pl.BlockSpec((d, d), lambda i: (0, 0)),
                  pl.BlockSpec((tm, d), lambda i: (i, 0)),
                  _mod_spec(which_gate, tm, n_lat, batch, d)],
        out_specs=pl.BlockSpec((tm, d), lambda i: (i, 0)),
        out_shape=jax.ShapeDtypeStruct((t, d), F32),
        compiler_params=_cparams(1), name="out_proj_residual",
    )(m, w, x, mods)


def _router_kernel(x_ref, g_ref, sh_ref, sc_ref, rw_ref, rb_ref, h_ref, cmb_ref):
    h = _normmod(x_ref[...], g_ref[...], sh_ref[0, 0], sc_ref[0, 0])
    h_ref[...] = h.astype(h_ref.dtype)
    logits = jnp.dot(h, rw_ref[...], precision=HIGHEST, preferred_element_type=F32) + rb_ref[...]
    lane = lax.broadcasted_iota(jnp.int32, logits.shape, 1)
    valid = lane < N_EXPERTS
    logits = jnp.where(valid, logits, -jnp.inf)
    ex = jnp.exp(logits - jnp.max(logits, axis=-1, keepdims=True))
    scores = ex / jnp.sum(ex, axis=-1, keepdims=True)
    grp = lax.shift_right_logical(lane, int(math.log2(EXPERTS_PER_GROUP)))
    best = jnp.zeros((logits.shape[0], 1), jnp.int32)
    best_max = jnp.max(jnp.where((grp == 0) & valid, scores, -1.0), axis=-1, keepdims=True)
    for g in range(1, N_GROUPS):
        gm = jnp.max(jnp.where((grp == g) & valid, scores, -1.0), axis=-1, keepdims=True)
        better = gm > best_max
        best = jnp.where(better, g, best)
        best_max = jnp.where(better, gm, best_max)
    s1 = jnp.where((grp == best) & valid, scores, -1.0)
    m1 = jnp.max(s1, axis=-1, keepdims=True)
    i1 = jnp.min(jnp.where(s1 == m1, lane, LANES_V7X), axis=-1, keepdims=True)
    s2 = jnp.where(lane == i1, -1.0, s1)
    m2 = jnp.max(s2, axis=-1, keepdims=True)
    i2 = jnp.min(jnp.where(s2 == m2, lane, LANES_V7X), axis=-1, keepdims=True)
    tot = m1 + m2
    cmb_ref[...] = jnp.where(lane == i1, m1 / tot, 0.0) + jnp.where(lane == i2, m2 / tot, 0.0)


def router(x, g, mods, which_shift, rw, rb, n_lat, batch, tm):
    t, d = x.shape
    return pl.pallas_call(
        _router_kernel,
        grid=(t // tm,),
        in_specs=[pl.BlockSpec((tm, d), lambda i: (i, 0)),
                  pl.BlockSpec((1, d), lambda i: (0, 0)),
                  _mod_spec(which_shift, tm, n_lat, batch, d),
                  _mod_spec(which_shift + 1, tm, n_lat, batch, d),
                  pl.BlockSpec((d, LANES_V7X), lambda i: (0, 0)),
                  pl.BlockSpec((1, LANES_V7X), lambda i: (0, 0))],
        out_specs=[pl.BlockSpec((tm, d), lambda i: (i, 0)),
                   pl.BlockSpec((tm, LANES_V7X), lambda i: (i, 0))],
        out_shape=[jax.ShapeDtypeStruct((t, d), BF16),
                   jax.ShapeDtypeStruct((t, LANES_V7X), F32)],
        compiler_params=_cparams(1), name="router",
    )(x, g.reshape(1, d), mods, mods, rw, rb)


def _moe_kernel(h_ref, cmb_ref, wg_ref, wu_ref, wd_ref, x_ref, gate_ref, o_ref):
    e = pl.program_id(1)

    @pl.when(e == 0)
    def _():
        o_ref[...] = jnp.zeros_like(o_ref)

    h = h_ref[...]
    gt = jnp.dot(h, wg_ref[0], preferred_element_type=F32)
    act = (gt * jax.nn.sigmoid(gt)) * jnp.dot(h, wu_ref[0], preferred_element_type=F32)
    lane = lax.broadcasted_iota(jnp.int32, cmb_ref.shape, 1)
    w = jnp.sum(jnp.where(lane == e, cmb_ref[...], 0.0), axis=-1, keepdims=True)
    o_ref[...] += w * jnp.dot(act.astype(BF16), wd_ref[0], preferred_element_type=F32)

    @pl.when(e == pl.num_programs(1) - 1)
    def _():
        o_ref[...] = x_ref[...] + gate_ref[0, 0] * o_ref[...]


def moe_residual(h, cmb, wg, wu, wd, x, mods, which_gate, n_lat, batch, tm):
    t, d = x.shape
    n_exp, _, de = wg.shape
    return pl.pallas_call(
        _moe_kernel,
        grid=(t // tm, n_exp),
        in_specs=[pl.BlockSpec((tm, d), lambda i, e: (i, 0)),
                  pl.BlockSpec((tm, LANES_V7X), lambda i, e: (i, 0)),
                  pl.BlockSpec((1, d, de), lambda i, e: (e, 0, 0)),
                  pl.BlockSpec((1, d, de), lambda i, e: (e, 0, 0)),
                  pl.BlockSpec((1, de, d), lambda i, e: (e, 0, 0)),
                  pl.BlockSpec((tm, d), lambda i, e: (i, 0)),
                  _mod_spec(which_gate, tm, n_lat, batch, d)],
        out_specs=pl.BlockSpec((tm, d), lambda i, e: (i, 0)),
        out_shape=jax.ShapeDtypeStruct((t, d), F32),
        compiler_params=_cparams(2), name="moe_residual",
    )(h, cmb, wg, wu, wd, x, mods)


def _rope_table(n_tokens):
    rows = n_tokens // GRID_W
    row = jnp.repeat(jnp.arange(rows), GRID_W).astype(F32)
    col = jnp.tile(jnp.arange(GRID_W), rows).astype(F32)
    pairs = HEAD_DIM // 4
    inv = ROPE_THETA ** (-jnp.arange(pairs, dtype=F32) / pairs)
    ang = jnp.concatenate([row[:, None] * inv, col[:, None] * inv], axis=-1)
    cos, sin = jnp.cos(ang), jnp.sin(ang)
    return jnp.concatenate([cos, cos, -sin, sin], axis=-1)


def _dft_tables(length):
    k = jnp.arange(length, dtype=jnp.int32)
    phase = (k[:, None] * k[None, :]) % (2 * length)
    ang = phase.astype(F32) * (math.pi / length)
    cm = jnp.cos(ang)
    alt = jnp.where(k % 2 == 0, 1.0, -1.0).astype(F32)
    fim = (-jnp.sin(ang)).at[0, :].set(alt)
    fw = jnp.stack([cm, fim]).astype(BF16)
    g = jnp.concatenate([cm, fim.T], axis=1).astype(BF16)
    return fw, g


def _hyena_features(length, k_pad):
    t = jnp.linspace(0.0, 1.0, length, dtype=F32)[:, None]
    w = 2.0 * math.pi * jnp.arange(length, dtype=F32)[:, None] / length
    f = jnp.linspace(1e-4, HYENA_BANDS - 1, HYENA_BANDS, dtype=F32)[None, :]
    z = jnp.concatenate([t, jnp.cos(f * w), -jnp.sin(f * w)], axis=-1)
    return jnp.pad(z, ((0, 0), (0, k_pad - z.shape[1])))


def _blockdiag_dense(w):
    nb, bs = w.shape[-3], w.shape[-2]
    eye = jnp.eye(nb, dtype=w.dtype)
    dense = jnp.einsum('...nde,nm->...ndme', w, eye)
    return dense.reshape(w.shape[:-3] + (nb * bs, nb * bs))


def kernel(x, c, ctx, c_ctx, w_ada, b_ada, g_mix, g_ffn, w_in, attn_sink, hy_short_w, hy_short_b, hy_w1, hy_b1, hy_freq, hy_w2, hy_b2, hy_w3, hy_skip, lru_conv_w, lru_conv_b, lru_wa, lru_ba, lru_wx, lru_bx, lru_lambda, w_br_attn, w_br_hy, w_br_lru, w_out, router_w, router_b, exp_w_gate, exp_w_up, exp_w_down, final_g):
    batch, n_lat, d = x.shape
    n_ctx = ctx.shape[1]
    depth = w_ada.shape[0]
    t_lat, t_ctx = batch * n_lat, batch * n_ctx
    hw = hy_skip.shape[2]
    lw = lru_lambda.shape[2]
    fh = hy_w2.shape[1]
    aw = N_Q_HEADS * HEAD_DIM
    kvw = N_KV_HEADS * HEAD_DIM
    q_off = 2 * kvw + lw
    hy_off = q_off + aw
    lru_y_off = hy_off + 3 * hw
    gate_off = lru_y_off + lw
    assert batch + 1 <= MOD_ROWS and t_lat % n_ctx == 0 and n_lat % ATTN_BLOCK == 0
    tile_base = math.gcd(n_lat, t_ctx)
    tm = _pick(tile_base, 512)
    tm_big = _pick(tile_base, 1024)

    xu = jnp.concatenate([x.reshape(t_lat, d), ctx.reshape(t_ctx, d)], axis=0)
    cc = jnp.concatenate([c, c_ctx[None, :], jnp.zeros((MOD_ROWS - batch - 1, d), F32)], axis=0)
    mods_all = ada_tables(cc, w_ada, b_ada).reshape(depth, MOD_ROWS, N_MOD, 1, d)

    rope_tab = _rope_table(n_lat)
    deltas = jnp.abs(jnp.linspace(math.log(HYENA_DECAY_TARGET) / HYENA_FAST_DECAY,
                                  math.log(HYENA_DECAY_TARGET) / HYENA_SLOW_DECAY, hw, dtype=F32))[None, :]
    seqs = []
    for length, row0 in ((n_lat, 0), (n_ctx, t_lat)):
        fw, g = _dft_tables(length)
        seqs.append((length, row0, fw, g, _hyena_features(length, fh)))
    hy_w1p = jnp.pad(hy_w1, ((0, 0), (0, fh - hy_w1.shape[1]), (0, 0)))

    sp = softplus_neg(lru_lambda)
    lru_w = jnp.concatenate([_blockdiag_dense(lru_wa), _blockdiag_dense(lru_wx)], axis=-1).astype(BF16)
    lru_b = jnp.concatenate([lru_ba, lru_bx], axis=-1)[:, :, None, :]
    wa_b, wh_b, wl_b, wo_b = (w.astype(BF16) for w in (w_br_attn, w_br_hy, w_br_lru, w_out))
    wg_b, wu_b, wd_b = (w.astype(BF16) for w in (exp_w_gate, exp_w_up, exp_w_down))
    rw = jnp.pad(router_w, ((0, 0), (0, LANES_V7X - router_w.shape[1])))
    rb = jnp.pad(router_b, (0, LANES_V7X - router_b.shape[0]))[None, :]
    h_zero = jnp.zeros((batch, 2, lw), F32)

    for l in range(depth):
        mods = mods_all[l]
        h = normmod(xu, g_mix[l], mods, 0, n_lat, batch, tm)
        p = in_proj(h, w_in, l, tm_big)

        o_att = jnp.concatenate([
            latent_attention(p, attn_sink[l], rope_tab, batch, n_lat, n_ctx, q_off),
            context_attention(p, attn_sink[l], batch, n_lat, n_ctx, q_off)], axis=0)

        xc_c = dwconv(p, lru_conv_w[l], lru_conv_b[l], batch, n_ctx, t_lat, 2 * kvw, lw)
        hf_c, hb_c, h_end = rg_lru(xc_c, lru_w[l], lru_b[l], sp[l], h_zero, batch, n_ctx)
        xc_l = dwconv(p, lru_conv_w[l], lru_conv_b[l], batch, n_lat, 0, 2 * kvw, lw)
        hf_l, hb_l, _ = rg_lru(xc_l, lru_w[l], lru_b[l], sp[l], h_end, batch, n_lat)
        hf = jnp.concatenate([hf_l, hf_c], axis=0)
        hb = jnp.concatenate([hb_l, hb_c], axis=0)

        o_hy = []
        for length, row0, fw, g, feats in seqs:
            u = dwconv(p, hy_short_w[l], hy_short_b[l], batch, length, row0, hy_off, 3 * hw)
            hid = hyena_hidden(feats, hy_w1p[l], hy_b1[l], hy_freq[l], hy_w2[l], hy_b2[l])
            taps, kn = hyena_taps(hid, hy_w3[l], deltas)
            kspec = hyena_spectrum(fw, taps)
            yp = conv_forward(fw, u, 0, kspec, kn, 0, batch, length, hw)
            z1 = conv_inverse(g, yp, u, 0, u, hw, hy_skip[l], 0, batch, length, F32)
            yp = conv_forward(fw, z1, 0, kspec, kn, 1, batch, length, hw)
            o_hy.append(conv_inverse(g, yp, z1, 0, u, 2 * hw, hy_skip[l], 1, batch, length, BF16))
        o_hy = jnp.concatenate(o_hy, axis=0)

        m = merge_branches(o_att, o_hy, p, hf, hb, wa_b[l], wh_b[l], wl_b[l], lru_y_off, gate_off, tm)
        xu = out_proj_residual(m, wo_b[l], xu, mods, 2, n_lat, batch, tm)

        fl, cmb = router(xu, g_ffn[l], mods, 3, rw, rb, n_lat, batch, tm)
        xu = moe_residual(fl, cmb, wg_b[l], wu_b[l], wd_b[l], xu, mods, 5, n_lat, batch, tm)

    return final_norm(xu, final_g, t_lat, tm).reshape(batch, n_lat, d)
```

```python
import functools
import math

import jax
import jax.numpy as jnp
import numpy as np
from jax import lax
from jax.experimental import pallas as pl
from jax.experimental.pallas import tpu as pltpu

F32 = jnp.float32
BF16 = jnp.bfloat16
HIGHEST = lax.Precision.HIGHEST

LANES_V7X = 128
VMEM_LIMIT_V7X = 56 * 1024 * 1024

EPS = 1e-6
GRID_W = 64
HEAD_DIM = 128
N_Q_HEADS = 8
N_KV_HEADS = 2
Q_PER_KV = N_Q_HEADS // N_KV_HEADS
WINDOW = 128
ATTN_BLOCK = 128
ROPE_THETA = 10000.0
HYENA_BANDS = 16
HYENA_DECAY_TARGET = 1e-2
HYENA_FAST_DECAY = 0.3
HYENA_SLOW_DECAY = 1.5
LRU_C = 8.0
N_EXPERTS = 16
N_GROUPS = 4
EXPERTS_PER_GROUP = N_EXPERTS // N_GROUPS
N_MOD = 6
MOD_ROWS = 8


def _cparams(n_axes):
    return pltpu.CompilerParams(dimension_semantics=("arbitrary",) * n_axes,
                                vmem_limit_bytes=VMEM_LIMIT_V7X)


def _sigmoid(x):
    return 0.5 * (1.0 + jnp.tanh(0.5 * x))


def _pick(n, cap):
    t = cap
    while n % t:
        t //= 2
    return t


def _ada_kernel(c_ref, w_ref, b_ref, o_ref):
    c = c_ref[...]
    s = (c * jax.nn.sigmoid(c)).astype(BF16)
    o_ref[0] = jnp.dot(s, w_ref[0].astype(BF16), preferred_element_type=F32) + b_ref[0]


def ada_tables(cc, w_ada, b_ada):
    depth, d, n6 = w_ada.shape
    tn = _pick(n6, 1024)
    return pl.pallas_call(
        _ada_kernel,
        grid=(depth, n6 // tn),
        in_specs=[pl.BlockSpec((MOD_ROWS, d), lambda l, j: (0, 0)),
                  pl.BlockSpec((1, d, tn), lambda l, j: (l, 0, j)),
                  pl.BlockSpec((1, 1, tn), lambda l, j: (l, 0, j))],
        out_specs=pl.BlockSpec((1, MOD_ROWS, tn), lambda l, j: (l, 0, j)),
        out_shape=jax.ShapeDtypeStruct((depth, MOD_ROWS, n6), F32),
        compiler_params=_cparams(2), name="ada_tables",
    )(cc, w_ada, b_ada.reshape(depth, 1, n6))


def _mod_spec(which, tm, n_lat, batch, d):
    return pl.BlockSpec((1, 1, 1, d),
                        lambda i, *_: (jnp.minimum((i * tm) // n_lat, batch), which, 0, 0))


def _normmod(x, g, shift, scale):
    y = x * lax.rsqrt(jnp.mean(x * x, axis=-1, keepdims=True) + EPS) * g
    return y * (1.0 + scale) + shift


def _normmod_kernel(x_ref, g_ref, sh_ref, sc_ref, o_ref):
    o_ref[...] = _normmod(x_ref[...], g_ref[...], sh_ref[0, 0], sc_ref[0, 0]).astype(o_ref.dtype)


def normmod(x, g, mods, which_shift, n_lat, batch, tm):
    t, d = x.shape
    return pl.pallas_call(
        _normmod_kernel,
        grid=(t // tm,),
        in_specs=[pl.BlockSpec((tm, d), lambda i: (i, 0)),
                  pl.BlockSpec((1, d), lambda i: (0, 0)),
                  _mod_spec(which_shift, tm, n_lat, batch, d),
                  _mod_spec(which_shift + 1, tm, n_lat, batch, d)],
        out_specs=pl.BlockSpec((tm, d), lambda i: (i, 0)),
        out_shape=jax.ShapeDtypeStruct((t, d), BF16),
        compiler_params=_cparams(1), name="normmod",
    )(x, g.reshape(1, d), mods, mods)


def _final_norm_kernel(x_ref, g_ref, o_ref):
    x = x_ref[...]
    o_ref[...] = x * lax.rsqrt(jnp.mean(x * x, axis=-1, keepdims=True) + EPS) * g_ref[...]


def final_norm(x, g, rows, tm):
    d = x.shape[1]
    return pl.pallas_call(
        _final_norm_kernel,
        grid=(rows // tm,),
        in_specs=[pl.BlockSpec((tm, d), lambda i: (i, 0)), pl.BlockSpec((1, d), lambda i: (0, 0))],
        out_specs=pl.BlockSpec((tm, d), lambda i: (i, 0)),
        out_shape=jax.ShapeDtypeStruct((rows, d), F32),
        compiler_params=_cparams(1), name="final_norm",
    )(x, g.reshape(1, d))


def _proj_kernel(a_ref, w_ref, o_ref, wb_ref):
    @pl.when(pl.program_id(1) == 0)
    def _():
        wb_ref[...] = w_ref[0].astype(BF16)

    o_ref[...] = jnp.dot(a_ref[...], wb_ref[...], preferred_element_type=F32)


def in_proj(a, w, layer, tm):
    t, k = a.shape
    n = w.shape[2]
    tn = _pick(n, 1024)
    return pl.pallas_call(
        _proj_kernel,
        grid=(n // tn, t // tm),
        in_specs=[pl.BlockSpec((tm, k), lambda j, i: (i, 0)),
                  pl.BlockSpec((1, k, tn), lambda j, i: (layer, 0, j))],
        out_specs=pl.BlockSpec((tm, tn), lambda j, i: (i, j)),
        out_shape=jax.ShapeDtypeStruct((t, n), F32),
        scratch_shapes=[pltpu.VMEM((k, tn), BF16)],
        compiler_params=_cparams(2), name="in_proj",
    )(a, w)


def _rope(x, tab):
    return x * tab[:, :HEAD_DIM] + pltpu.roll(x, HEAD_DIM // 2, axis=1) * tab[:, HEAD_DIM:]


def _sink_column(sink_ref, h, rows):
    r = lax.broadcasted_iota(jnp.int32, (Q_PER_KV * rows, 1), 0)
    col = jnp.full((Q_PER_KV * rows, 1), sink_ref[h * Q_PER_KV], F32)
    for g in range(1, Q_PER_KV):
        col = jnp.where(r >= g * rows, sink_ref[h * Q_PER_KV + g], col)
    return col


def _nt(a, b):
    return lax.dot_general(a, b, (((1,), (1,)), ((), ())), preferred_element_type=F32)


def _lat_attn_kernel(sink_ref, q_ref, kvm_ref, kv0_ref, kvp_ref, kvc_ref, tm_ref, t0_ref, tp_ref, o_ref,
                     *, n_lat):
    i = pl.program_id(1)
    blk = ATTN_BLOCK
    kv_w = N_KV_HEADS * HEAD_DIM
    scale = HEAD_DIM ** -0.5
    tabs = (tm_ref[...], t0_ref[...], tp_ref[...])
    kvs = (kvm_ref, kv0_ref, kvp_ref)
    qi = lax.broadcasted_iota(jnp.int32, (Q_PER_KV * blk, 3 * blk), 0) & (blk - 1)
    kj = lax.broadcasted_iota(jnp.int32, (Q_PER_KV * blk, 3 * blk), 1)
    key_pos = (i - 1) * blk + kj
    ok = (jnp.abs(qi + blk - kj) <= WINDOW) & (key_pos >= 0) & (key_pos < n_lat)
    for h in range(N_KV_HEADS):
        ks = slice(h * HEAD_DIM, (h + 1) * HEAD_DIM)
        vs = slice(kv_w + h * HEAD_DIM, kv_w + (h + 1) * HEAD_DIM)
        k_loc = jnp.concatenate([_rope(kvs[n][:, ks], tabs[n]) for n in range(3)], axis=0).astype(BF16)
        v_loc = jnp.concatenate([kvs[n][:, vs] for n in range(3)], axis=0).astype(BF16)
        k_ctx = kvc_ref[:, ks].astype(BF16)
        v_ctx = kvc_ref[:, vs].astype(BF16)
        q4 = jnp.concatenate(
            [_rope(q_ref[:, (h * Q_PER_KV + g) * HEAD_DIM:(h * Q_PER_KV + g + 1) * HEAD_DIM], tabs[1])
             for g in range(Q_PER_KV)], axis=0).astype(BF16)
        s_loc = jnp.where(ok, _nt(q4, k_loc) * scale, -jnp.inf)
        s_ctx = _nt(q4, k_ctx) * scale
        sink = _sink_column(sink_ref, h, blk)
        m = jnp.maximum(jnp.maximum(jnp.max(s_loc, axis=-1, keepdims=True),
                                    jnp.max(s_ctx, axis=-1, keepdims=True)), sink)
        p_loc = jnp.exp(s_loc - m)
        p_ctx = jnp.exp(s_ctx - m)
        den = (jnp.sum(p_loc, axis=-1, keepdims=True) + jnp.sum(p_ctx, axis=-1, keepdims=True)
               + jnp.exp(sink - m))
        o = (jnp.dot(p_loc.astype(BF16), v_loc, preferred_element_type=F32)
             + jnp.dot(p_ctx.astype(BF16), v_ctx, preferred_element_type=F32)) / den
        for g in range(Q_PER_KV):
            hq = h * Q_PER_KV + g
            o_ref[:, hq * HEAD_DIM:(hq + 1) * HEAD_DIM] = o[g * blk:(g + 1) * blk].astype(o_ref.dtype)


def latent_attention(p, sink, rope_tab, batch, n_lat, n_ctx, q_off):
    blk = ATTN_BLOCK
    nb = n_lat // blk
    aw = N_Q_HEADS * HEAD_DIM
    kvw = 2 * N_KV_HEADS * HEAD_DIM
    ctx_blk0 = (batch * n_lat) // n_ctx
    kv_spec = lambda off: pl.BlockSpec(
        (blk, kvw), lambda b, i: (b * nb + jnp.clip(i + off, 0, nb - 1), 0))
    tab_spec = lambda off: pl.BlockSpec(
        (blk, 2 * HEAD_DIM), lambda b, i: (jnp.clip(i + off, 0, nb - 1), 0))
    return pl.pallas_call(
        functools.partial(_lat_attn_kernel, n_lat=n_lat),
        grid=(batch, nb),
        in_specs=[pl.BlockSpec(memory_space=pltpu.SMEM),
                  pl.BlockSpec((blk, aw), lambda b, i: (b * nb + i, q_off // aw)),
                  kv_spec(-1), kv_spec(0), kv_spec(1),
                  pl.BlockSpec((n_ctx, kvw), lambda b, i: (ctx_blk0 + b, 0)),
                  tab_spec(-1), tab_spec(0), tab_spec(1)],
        out_specs=pl.BlockSpec((blk, aw), lambda b, i: (b * nb + i, 0)),
        out_shape=jax.ShapeDtypeStruct((p.shape[0], aw), BF16),
        compiler_params=_cparams(2), name="latent_attention",
    )(sink, p, p, p, p, p, rope_tab, rope_tab, rope_tab)


def _ctx_attn_kernel(sink_ref, q_ref, kv_ref, o_lat_ref, o_ref):
    del o_lat_ref
    rows = q_ref.shape[0]
    kv_w = N_KV_HEADS * HEAD_DIM
    scale = HEAD_DIM ** -0.5
    for h in range(N_KV_HEADS):
        k = kv_ref[:, h * HEAD_DIM:(h + 1) * HEAD_DIM].astype(BF16)
        v = kv_ref[:, kv_w + h * HEAD_DIM:kv_w + (h + 1) * HEAD_DIM].astype(BF16)
        q4 = jnp.concatenate(
            [q_ref[:, (h * Q_PER_KV + g) * HEAD_DIM:(h * Q_PER_KV + g + 1) * HEAD_DIM]
             for g in range(Q_PER_KV)], axis=0).astype(BF16)
        s = _nt(q4, k) * scale
        sink = _sink_column(sink_ref, h, rows)
        m = jnp.maximum(jnp.max(s, axis=-1, keepdims=True), sink)
        p = jnp.exp(s - m)
        den = jnp.sum(p, axis=-1, keepdims=True) + jnp.exp(sink - m)
        o = jnp.dot(p.astype(BF16), v, preferred_element_type=F32) / den
        for g in range(Q_PER_KV):
            hq = h * Q_PER_KV + g
            o_ref[:, hq * HEAD_DIM:(hq + 1) * HEAD_DIM] = o[g * rows:(g + 1) * rows].astype(o_ref.dtype)


def context_attention(p, sink, o_lat, batch, n_lat, n_ctx, q_off):
    aw = N_Q_HEADS * HEAD_DIM
    kvw = 2 * N_KV_HEADS * HEAD_DIM
    ctx_blk0 = (batch * n_lat) // n_ctx
    return pl.pallas_call(
        _ctx_attn_kernel,
        grid=(batch,),
        in_specs=[pl.BlockSpec(memory_space=pltpu.SMEM),
                  pl.BlockSpec((n_ctx, aw), lambda b: (ctx_blk0 + b, q_off // aw)),
                  pl.BlockSpec((n_ctx, kvw), lambda b: (ctx_blk0 + b, 0)),
                  pl.BlockSpec(memory_space=pl.ANY)],
        out_specs=pl.BlockSpec((n_ctx, aw), lambda b: (ctx_blk0 + b, 0)),
        out_shape=jax.ShapeDtypeStruct(o_lat.shape, o_lat.dtype),
        input_output_aliases={3: 0},
        compiler_params=_cparams(1), name="context_attention",
    )(sink, p, p, o_lat)


def _dwconv_kernel(u_ref, w_ref, b_ref, o_ref, *, width):
    x = u_ref[...]
    length = x.shape[0]
    left = width // 2
    t = lax.broadcasted_iota(jnp.int32, (length, 1), 0)
    acc = jnp.broadcast_to(b_ref[...], x.shape)
    for j in range(width):
        s = j - left
        if s == 0:
            xs = x
        else:
            xs = pltpu.roll(x, (-s) % length, axis=0)
            xs = jnp.where((t + s >= 0) & (t + s < length), xs, 0.0)
        acc = acc + xs * w_ref[j:j + 1, :]
    o_ref[...] = acc


def dwconv(p, w, b, n_seq, length, row0, col0, n_ch):
    width = w.shape[0]
    cb = _pick(n_ch, 256)
    return pl.pallas_call(
        functools.partial(_dwconv_kernel, width=width),
        grid=(n_seq, n_ch // cb),
        in_specs=[pl.BlockSpec((length, cb), lambda s, c: (row0 // length + s, col0 // cb + c)),
                  pl.BlockSpec((width, cb), lambda s, c: (0, c)),
                  pl.BlockSpec((1, cb), lambda s, c: (0, c))],
        out_specs=pl.BlockSpec((length, cb), lambda s, c: (s, c)),
        out_shape=jax.ShapeDtypeStruct((n_seq * length, n_ch), F32),
        compiler_params=_cparams(2), name="dwconv",
    )(p, w, b.reshape(1, n_ch))


def _lru_kernel(xf_ref, xb_ref, w_ref, bias_ref, sp_ref, h0_ref, *rest):
    hf_ref, hb_ref, hl_ref, af, bf, ab, bb, carry = rest[-8:]
    j = pl.program_id(1)
    width = xf_ref.shape[1]

    @pl.when(j == 0)
    def _():
        carry[...] = h0_ref[0]

    def gates(x, d):
        g = jnp.dot(x.astype(BF16), w_ref[0, d], preferred_element_type=F32) + bias_ref[d]
        r = _sigmoid(g[:, :width])
        gi = _sigmoid(g[:, width:])
        a = jnp.exp(-LRU_C * r * sp_ref[d])
        return a, jnp.sqrt(1.0 - a * a) * (gi * x)

    af[...], bf[...] = gates(xf_ref[...], 0)
    ab[...], bb[...] = gates(xb_ref[...], 1)
    tl = af.shape[0]

    def body(t, hs):
        hf, hb = hs
        hf = af[pl.ds(t, 1), :] * hf + bf[pl.ds(t, 1), :]
        hf_ref[pl.ds(t, 1), :] = hf
        tb = tl - 1 - t
        hb = ab[pl.ds(tb, 1), :] * hb + bb[pl.ds(tb, 1), :]
        hb_ref[pl.ds(tb, 1), :] = hb
        return hf, hb

    hf, hb = lax.fori_loop(0, tl, body, (carry[0:1, :], carry[1:2, :]), unroll=8)
    carry[0:1, :] = hf
    carry[1:2, :] = hb
    hl_ref[0] = carry[...]


def rg_lru(xc, w_gates, layer, b_gates, softplus_neg_lam, h0, n_seq, length, total_rows, row0, prev=None):
    width = xc.shape[1]
    tl = _pick(length, 512)
    nc = length // tl
    blk0 = row0 // tl
    in_specs = [pl.BlockSpec((tl, width), lambda s, j: (s * nc + j, 0)),
                pl.BlockSpec((tl, width), lambda s, j: (s * nc + nc - 1 - j, 0)),
                pl.BlockSpec((1, 2, width, 2 * width), lambda s, j: (layer, 0, 0, 0)),
                pl.BlockSpec((2, 1, 2 * width), lambda s, j: (0, 0, 0)),
                pl.BlockSpec((2, 1, width), lambda s, j: (0, 0, 0)),
                pl.BlockSpec((1, 2, width), lambda s, j: (s, 0, 0))]
    args = [xc, xc, w_gates, b_gates, softplus_neg_lam, h0]
    aliases = {}
    if prev is not None:
        in_specs += [pl.BlockSpec(memory_space=pl.ANY)] * 2
        aliases = {len(args): 0, len(args) + 1: 1}
        args += list(prev)
    return pl.pallas_call(
        _lru_kernel,
        grid=(n_seq, nc),
        in_specs=in_specs,
        out_specs=[pl.BlockSpec((tl, width), lambda s, j: (blk0 + s * nc + j, 0)),
                   pl.BlockSpec((tl, width), lambda s, j: (blk0 + s * nc + nc - 1 - j, 0)),
                   pl.BlockSpec((1, 2, width), lambda s, j: (s, 0, 0))],
        out_shape=[jax.ShapeDtypeStruct((total_rows, width), F32),
                   jax.ShapeDtypeStruct((total_rows, width), F32),
                   jax.ShapeDtypeStruct((n_seq, 2, width), F32)],
        scratch_shapes=[pltpu.VMEM((tl, width), F32)] * 4 + [pltpu.VMEM((2, width), F32)],
        input_output_aliases=aliases,
        compiler_params=_cparams(2), name="rg_lru",
    )(*args)


def _softplus_kernel(x_ref, o_ref):
    x = -x_ref[...]
    o_ref[...] = jnp.maximum(x, 0.0) + jnp.log(1.0 + jnp.exp(-jnp.abs(x)))


def softplus_neg(lam):
    depth, two, width = lam.shape
    x = lam.reshape(depth * two, width)
    out = pl.pallas_call(
        _softplus_kernel,
        out_shape=jax.ShapeDtypeStruct(x.shape, F32), name="softplus_neg",
    )(x)
    return out.reshape(depth, two, 1, width)


def _hy_hidden_kernel(z_ref, w1_ref, b1_ref, fr_ref, w2_ref, b2_ref, o_ref):
    fr = fr_ref[...]
    h = jnp.sin(fr * (jnp.dot(z_ref[...], w1_ref[...], precision=HIGHEST,
                              preferred_element_type=F32) + b1_ref[...]))
    o_ref[...] = jnp.sin(fr * (jnp.dot(h, w2_ref[...], precision=HIGHEST,
                                       preferred_element_type=F32) + b2_ref[...]))


def hyena_hidden(z, w1, b1, freq, w2, b2):
    length = z.shape[0]
    fh = w2.shape[0]
    return pl.pallas_call(
        _hy_hidden_kernel,
        out_shape=jax.ShapeDtypeStruct((length, fh), F32), name="hyena_hidden",
    )(z, w1, b1.reshape(1, fh), freq.reshape(1, fh), w2, b2.reshape(1, fh))


def _hy_taps_kernel(hid_ref, wf_ref, wb_ref, delta_ref, u_ref, kn_ref):
    hid = hid_ref[...]
    length = hid.shape[0]
    hf = jnp.dot(hid, wf_ref[...], precision=HIGHEST, preferred_element_type=F32)
    hb = jnp.dot(hid, wb_ref[...], precision=HIGHEST, preferred_element_type=F32)
    ti = lax.broadcasted_iota(jnp.int32, (length, 1), 0)
    win = jnp.exp(-(ti.astype(F32) * (1.0 / (length - 1))) * delta_ref[...])
    f = hf * win
    b = jnp.where(ti >= 1, hb * win, 0.0)
    sc = lax.rsqrt(jnp.sum(f * f + b * b, axis=0, keepdims=True) + EPS)
    u1 = (f + b) * sc
    u2 = (f - b) * sc
    sign = jnp.where((ti & 1) == 0, 1.0, -1.0)
    kn_ref[...] = jnp.sum(u1 * sign, axis=0, keepdims=True)
    u_ref[0] = u1.astype(u_ref.dtype)
    u_ref[1] = u2.astype(u_ref.dtype)


def hyena_taps(hid, w3, deltas):
    length, fh = hid.shape
    hw = deltas.shape[1]
    n_ord = w3.shape[1] // (2 * hw)
    cw = _pick(hw, 128)
    nc = hw // cw
    return pl.pallas_call(
        _hy_taps_kernel,
        grid=(n_ord, nc),
        in_specs=[pl.BlockSpec((length, fh), lambda o, c: (0, 0)),
                  pl.BlockSpec((fh, cw), lambda o, c: (0, o * 2 * nc + c)),
                  pl.BlockSpec((fh, cw), lambda o, c: (0, o * 2 * nc + nc + c)),
                  pl.BlockSpec((1, cw), lambda o, c: (0, c))],
        out_specs=[pl.BlockSpec((2, length, cw), lambda o, c: (0, 0, o * nc + c)),
                   pl.BlockSpec((1, cw), lambda o, c: (0, o * nc + c))],
        out_shape=[jax.ShapeDtypeStruct((2, length, n_ord * hw), BF16),
                   jax.ShapeDtypeStruct((1, n_ord * hw), F32)],
        compiler_params=_cparams(2), name="hyena_taps",
    )(hid, w3, w3, deltas)


def _spectrum_kernel(fw_ref, u_ref, o_ref):
    o_ref[0] = jnp.dot(fw_ref[0], u_ref[0], preferred_element_type=F32)


def hyena_spectrum(fw, u):
    _, length, n = u.shape
    tm = _pick(length, 512)
    return pl.pallas_call(
        _spectrum_kernel,
        grid=(2, length // tm),
        in_specs=[pl.BlockSpec((1, tm, length), lambda h, i: (h, i, 0)),
                  pl.BlockSpec((1, length, n), lambda h, i: (h, 0, 0))],
        out_specs=pl.BlockSpec((1, tm, n), lambda h, i: (h, i, 0)),
        out_shape=jax.ShapeDtypeStruct((2, length, n), F32),
        compiler_params=_cparams(2), name="hyena_spectrum",
    )(fw, u)


def _conv_fwd_kernel(fw_ref, z_ref, k_ref, kn_ref, y_ref, zb_ref):
    i = pl.program_id(1)

    @pl.when(i == 0)
    def _():
        zb_ref[...] = z_ref[...].astype(BF16)

    z = zb_ref[...]
    zr = jnp.dot(fw_ref[0], z, preferred_element_type=F32)
    zi = jnp.dot(fw_ref[1], z, preferred_element_type=F32)
    kr = k_ref[0]
    ki = k_ref[1]
    yr = zr * kr - zi * ki
    yi = zr * ki + zi * kr
    first = (lax.broadcasted_iota(jnp.int32, (zr.shape[0], 1), 0) == 0) & (i == 0)
    y_ref[0, 0] = jnp.where(first, 0.5 * zr * kr, yr).astype(y_ref.dtype)
    y_ref[0, 1] = jnp.where(first, 0.5 * zi * kn_ref[...], yi).astype(y_ref.dtype)


def conv_forward(fw, z, col0, kspec, kn, order, n_seq, length, hw):
    tm = _pick(length, 512)
    return pl.pallas_call(
        _conv_fwd_kernel,
        grid=(n_seq, length // tm),
        in_specs=[pl.BlockSpec((2, tm, length), lambda s, i: (0, i, 0)),
                  pl.BlockSpec((length, hw), lambda s, i: (s, col0 // hw)),
                  pl.BlockSpec((2, tm, hw), lambda s, i: (0, i, order)),
                  pl.BlockSpec((1, hw), lambda s, i: (0, order))],
        out_specs=pl.BlockSpec((1, 2, tm, hw), lambda s, i: (s, 0, i, 0)),
        out_shape=jax.ShapeDtypeStruct((n_seq, 2, length, hw), BF16),
        scratch_shapes=[pltpu.VMEM((length, hw), BF16)],
        compiler_params=_cparams(2), name="hyena_conv_fwd",
    )(fw, z, kspec, kn)


def _conv_inv_kernel(g_ref, y_ref, z_ref, gate_ref, skip_ref, *rest, inv_len):
    o_ref = rest[-1]
    y = jnp.dot(g_ref[...], y_ref[0], preferred_element_type=F32) * inv_len
    o_ref[...] = (gate_ref[...] * (y + skip_ref[0] * z_ref[...])).astype(o_ref.dtype)


def conv_inverse(g, yp, z, z_col0, gates, gate_col0, skip, order, n_seq, length, out_dtype,
                 total_rows=None, row0=0, prev=None):
    hw = yp.shape[3]
    tm = _pick(length, 512)
    nt = length // tm
    blk0 = row0 // tm
    total_rows = n_seq * length if total_rows is None else total_rows
    in_specs = [pl.BlockSpec((tm, 2 * length), lambda s, i: (i, 0)),
                pl.BlockSpec((1, 2 * length, hw), lambda s, i: (s, 0, 0)),
                pl.BlockSpec((tm, hw), lambda s, i: (s * nt + i, z_col0 // hw)),
                pl.BlockSpec((tm, hw), lambda s, i: (s * nt + i, gate_col0 // hw)),
                pl.BlockSpec((1, 1, hw), lambda s, i: (order, 0, 0))]
    args = [g, yp.reshape(n_seq, 2 * length, hw), z, gates, skip.reshape(skip.shape[0], 1, hw)]
    aliases = {}
    if prev is not None:
        in_specs.append(pl.BlockSpec(memory_space=pl.ANY))
        aliases = {len(args): 0}
        args.append(prev)
    return pl.pallas_call(
        functools.partial(_conv_inv_kernel, inv_len=1.0 / length),
        grid=(n_seq, nt),
        in_specs=in_specs,
        out_specs=pl.BlockSpec((tm, hw), lambda s, i: (blk0 + s * nt + i, 0)),
        out_shape=jax.ShapeDtypeStruct((total_rows, hw), out_dtype),
        input_output_aliases=aliases,
        compiler_params=_cparams(2), name="hyena_conv_inv",
    )(*args)


def _merge_kernel(oa_ref, oh_ref, y_ref, hf_ref, hb_ref, g0_ref, g1_ref, g2_ref,
                  wa_ref, wh_ref, wl_ref, o_ref, ol_ref):
    @pl.when(pl.program_id(1) == 0)
    def _():
        ol_ref[...] = (jax.nn.gelu(y_ref[...], approximate=True)
                       * (hf_ref[...] + hb_ref[...])).astype(ol_ref.dtype)

    m = (_sigmoid(g0_ref[...]) * jnp.dot(oa_ref[...], wa_ref[0], preferred_element_type=F32)
         + _sigmoid(g1_ref[...]) * jnp.dot(oh_ref[...], wh_ref[0], preferred_element_type=F32)
         + _sigmoid(g2_ref[...]) * jnp.dot(ol_ref[...], wl_ref[0], preferred_element_type=F32))
    o_ref[...] = m.astype(o_ref.dtype)


def merge_branches(o_att, o_hy, p, hf, hb, wa, wh, wl, layer, lru_y_off, gate_off, tm):
    t, aw = o_att.shape
    hw = o_hy.shape[1]
    lw = hf.shape[1]
    d = wa.shape[2]
    tn = _pick(d, 512)
    gate_spec = lambda k: pl.BlockSpec((tm, tn), lambda i, j: (i, (gate_off + k * d) // tn + j))
    w_spec = lambda rows: pl.BlockSpec((1, rows, tn), lambda i, j: (layer, 0, j))
    return pl.pallas_call(
        _merge_kernel,
        grid=(t // tm, d // tn),
        in_specs=[pl.BlockSpec((tm, aw), lambda i, j: (i, 0)),
                  pl.BlockSpec((tm, hw), lambda i, j: (i, 0)),
                  pl.BlockSpec((tm, lw), lambda i, j: (i, lru_y_off // lw)),
                  pl.BlockSpec((tm, lw), lambda i, j: (i, 0)),
                  pl.BlockSpec((tm, lw), lambda i, j: (i, 0)),
                  gate_spec(0), gate_spec(1), gate_spec(2),
                  w_spec(aw), w_spec(hw), w_spec(lw)],
        out_specs=pl.BlockSpec((tm, tn), lambda i, j: (i, j)),
        out_shape=jax.ShapeDtypeStruct((t, d), BF16),
        scratch_shapes=[pltpu.VMEM((tm, lw), BF16)],
        compiler_params=_cparams(2), name="merge_branches",
    )(o_att, o_hy, p, hf, hb, p, p, p, wa, wh, wl)


def _out_proj_kernel(m_ref, w_ref, x_ref, gate_ref, o_ref):
    o_ref[...] = x_ref[...] + gate_ref[0, 0] * jnp.dot(m_ref[...], w_ref[0],
                                                       preferred_element_type=F32)


def out_proj_residual(m, w, layer, x, mods, which_gate, n_lat, batch, tm):
    t, d = x.shape
    return pl.pallas_call(
        _out_proj_kernel,
        grid=(t // tm,),
        in_specs=[pl.BlockSpec((tm, d), lambda i: (i, 0)),
                  pl.BlockSpec((1, d, d), lambda i: (layer, 0, 0)),
                  pl.BlockSpec((tm, d), lambda i: (i, 0)),
                  _mod_spec(which_gate, tm, n_lat, batch, d)],
        out_specs=pl.BlockSpec((tm, d), lambda i: (i, 0)),
        out_shape=jax.ShapeDtypeStruct((t, d), F32),
        compiler_params=_cparams(1), name="out_proj_residual",
    )(m, w, x, mods)


def _router_kernel(x_ref, g_ref, sh_ref, sc_ref, rw_ref, rb_ref, h_ref, meta_ref, cnt_ref):
    h = _normmod(x_ref[...], g_ref[...], sh_ref[0, 0], sc_ref[0, 0])
    h_ref[...] = h.astype(h_ref.dtype)
    logits = jnp.dot(h, rw_ref[...], precision=HIGHEST, preferred_element_type=F32) + rb_ref[...]
    lane = lax.broadcasted_iota(jnp.int32, logits.shape, 1)
    valid = lane < N_EXPERTS
    logits = jnp.where(valid, logits, -jnp.inf)
    ex = jnp.exp(logits - jnp.max(logits, axis=-1, keepdims=True))
    scores = ex / jnp.sum(ex, axis=-1, keepdims=True)
    grp = lax.shift_right_logical(lane, int(math.log2(EXPERTS_PER_GROUP)))
    best = jnp.zeros((logits.shape[0], 1), jnp.int32)
    best_max = jnp.max(jnp.where((grp == 0) & valid, scores, -1.0), axis=-1, keepdims=True)
    for g in range(1, N_GROUPS):
        gm = jnp.max(jnp.where((grp == g) & valid, scores, -1.0), axis=-1, keepdims=True)
        better = gm > best_max
        best = jnp.where(better, g, best)
        best_max = jnp.where(better, gm, best_max)
    s1 = jnp.where((grp == best) & valid, scores, -1.0)
    m1 = jnp.max(s1, axis=-1, keepdims=True)
    i1 = jnp.min(jnp.where(s1 == m1, lane, LANES_V7X), axis=-1, keepdims=True)
    s2 = jnp.where(lane == i1, -1.0, s1)
    m2 = jnp.max(s2, axis=-1, keepdims=True)
    i2 = jnp.min(jnp.where(s2 == m2, lane, LANES_V7X), axis=-1, keepdims=True)
    tot = m1 + m2
    @pl.when(pl.program_id(0) == 0)
    def _():
        cnt_ref[...] = jnp.zeros_like(cnt_ref)

    tm = logits.shape[0]
    onehot = ((lane == i1) | (lane == i2)).astype(BF16)
    lower = (lax.broadcasted_iota(jnp.int32, (tm, tm), 0)
             > lax.broadcasted_iota(jnp.int32, (tm, tm), 1)).astype(BF16)
    before = jnp.dot(lower, onehot, preferred_element_type=F32) + cnt_ref[...]
    r1 = jnp.sum(jnp.where(lane == i1, before, 0.0), axis=-1, keepdims=True)
    r2 = jnp.sum(jnp.where(lane == i2, before, 0.0), axis=-1, keepdims=True)
    cnt_ref[...] += jnp.sum(onehot.astype(F32), axis=0, keepdims=True)
    cols = (i1.astype(F32), i2.astype(F32), r1, r2, m1 / tot, m2 / tot)
    meta = jnp.zeros(logits.shape, F32)
    for k, col in enumerate(cols):
        meta = jnp.where(lane == k, col, meta)
    meta_ref[...] = meta


ROUTE_E, ROUTE_RANK, ROUTE_W = 0, 2, 4


def router(x, g, mods, which_shift, rw, rb, n_lat, batch, tm):
    t, d = x.shape
    return pl.pallas_call(
        _router_kernel,
        grid=(t // tm,),
        in_specs=[pl.BlockSpec((tm, d), lambda i: (i, 0)),
                  pl.BlockSpec((1, d), lambda i: (0, 0)),
                  _mod_spec(which_shift, tm, n_lat, batch, d),
                  _mod_spec(which_shift + 1, tm, n_lat, batch, d),
                  pl.BlockSpec((d, LANES_V7X), lambda i: (0, 0)),
                  pl.BlockSpec((1, LANES_V7X), lambda i: (0, 0))],
        out_specs=[pl.BlockSpec((tm, d), lambda i: (i, 0)),
                   pl.BlockSpec((tm, LANES_V7X), lambda i: (i, 0)),
                   pl.BlockSpec((1, LANES_V7X), lambda i: (0, 0))],
        out_shape=[jax.ShapeDtypeStruct((t, d), F32),
                   jax.ShapeDtypeStruct((t, LANES_V7X), F32),
                   jax.ShapeDtypeStruct((1, LANES_V7X), F32)],
        compiler_params=_cparams(1), name="router",
    )(x, g.reshape(1, d), mods, mods, rw, rb)


def dispatch_plan(meta, counts, n_exp, tile, n_tiles):
    cnt = counts[0, :n_exp].astype(jnp.int32)
    tiles_e = (cnt + tile - 1) // tile
    tile_end = jnp.cumsum(tiles_e)
    row_start = (tile_end - tiles_e) * tile
    experts = meta[:, ROUTE_E:ROUTE_E + 2].astype(jnp.int32)
    ranks = meta[:, ROUTE_RANK:ROUTE_RANK + 2].astype(jnp.int32)
    dest = (row_start[experts] + ranks).reshape(-1)
    tile_expert = jnp.minimum(jnp.searchsorted(tile_end, jnp.arange(n_tiles), side='right'),
                              n_exp - 1).astype(jnp.int32)
    return dest, tile_expert, tile_end[-1:].astype(jnp.int32)


def _dispatch_kernel(dest_ref, h_ref, xs_in_ref, xs_ref, sem):
    del xs_in_ref
    tm = h_ref.shape[0]
    base = pl.program_id(0) * (2 * tm)

    def issue(r, carry):
        for k in range(2):
            row = dest_ref[base + 2 * r + k]
            pltpu.make_async_copy(h_ref.at[pl.ds(r, 1)], xs_ref.at[pl.ds(row, 1)], sem).start()
        return carry

    lax.fori_loop(0, tm, issue, 0, unroll=8)
    for k in range(2):
        pltpu.make_async_copy(h_ref, xs_ref.at[pl.ds(0, tm)], sem).wait()


def moe_dispatch(h, dest, n_rows, tm):
    t, d = h.shape
    return pl.pallas_call(
        _dispatch_kernel,
        grid_spec=pltpu.PrefetchScalarGridSpec(
            num_scalar_prefetch=1, grid=(t // tm,),
            in_specs=[pl.BlockSpec((tm, d), lambda i, dest: (i, 0)),
                      pl.BlockSpec(memory_space=pl.ANY)],
            out_specs=pl.BlockSpec(memory_space=pl.ANY),
            scratch_shapes=[pltpu.SemaphoreType.DMA]),
        out_shape=jax.ShapeDtypeStruct((n_rows, d), F32),
        input_output_aliases={2: 0},
        compiler_params=_cparams(1), name="moe_dispatch",
    )(dest, h, jnp.zeros((n_rows, d), F32))


def _experts_kernel(te_ref, nu_ref, xs_ref, wg_ref, wu_ref, wd_ref, ys_ref, wgb, wub, wdb):
    j = pl.program_id(0)

    @pl.when(j < nu_ref[0])
    def _():
        @pl.when((j == 0) | (te_ref[j] != te_ref[jnp.maximum(j - 1, 0)]))
        def _():
            wgb[...] = wg_ref[0, 0].astype(BF16)
            wub[...] = wu_ref[0, 0].astype(BF16)
            wdb[...] = wd_ref[0, 0].astype(BF16)

        x = xs_ref[...].astype(BF16)
        gt = jnp.dot(x, wgb[...], preferred_element_type=F32)
        act = (gt * _sigmoid(gt)) * jnp.dot(x, wub[...], preferred_element_type=F32)
        ys_ref[...] = jnp.dot(act.astype(BF16), wdb[...], preferred_element_type=F32)


def moe_experts(tile_expert, n_used, xs, wg, wu, wd, layer, tile):
    n_rows, d = xs.shape
    de = wg.shape[3]
    used = lambda j, nu: jnp.minimum(j, nu[0] - 1)
    w_spec = lambda shape: pl.BlockSpec((1, 1) + shape, lambda j, te, nu: (layer, te[used(j, nu)], 0, 0))
    return pl.pallas_call(
        _experts_kernel,
        grid_spec=pltpu.PrefetchScalarGridSpec(
            num_scalar_prefetch=2, grid=(n_rows // tile,),
            in_specs=[pl.BlockSpec((tile, d), lambda j, te, nu: (used(j, nu), 0)),
                      w_spec((d, de)), w_spec((d, de)), w_spec((de, d))],
            out_specs=pl.BlockSpec((tile, d), lambda j, te, nu: (used(j, nu), 0)),
            scratch_shapes=[pltpu.VMEM((d, de), BF16), pltpu.VMEM((d, de), BF16),
                            pltpu.VMEM((de, d), BF16)]),
        out_shape=jax.ShapeDtypeStruct((n_rows, d), F32),
        compiler_params=_cparams(1), name="moe_experts",
    )(tile_expert, n_used, xs, wg, wu, wd)


def _combine_kernel(dest_ref, meta_ref, x_ref, gate_ref, ys_ref, o_ref, buf, sem):
    i = pl.program_id(0)
    tm = x_ref.shape[0]

    def gather(tile, slot):
        base = tile * (2 * tm)

        def issue(r, carry):
            for k in range(2):
                row = dest_ref[base + 2 * r + k]
                pltpu.make_async_copy(ys_ref.at[pl.ds(row, 1)], buf.at[slot, k, pl.ds(r, 1)],
                                      sem.at[slot]).start()
            return carry

        lax.fori_loop(0, tm, issue, 0, unroll=8)

    @pl.when(i == 0)
    def _():
        gather(0, 0)

    @pl.when(i + 1 < pl.num_programs(0))
    def _():
        gather(i + 1, (i + 1) % 2)

    slot = i % 2
    for k in range(2):
        pltpu.make_async_copy(ys_ref.at[pl.ds(0, tm)], buf.at[slot, k], sem.at[slot]).wait()
    meta = meta_ref[...]
    y = (meta[:, ROUTE_W:ROUTE_W + 1] * buf[slot, 0] + meta[:, ROUTE_W + 1:ROUTE_W + 2] * buf[slot, 1])
    o_ref[...] = x_ref[...] + gate_ref[0, 0] * y


def moe_combine(dest, meta, x, mods, which_gate, ys, n_lat, batch, tm):
    t, d = x.shape
    return pl.pallas_call(
        _combine_kernel,
        grid_spec=pltpu.PrefetchScalarGridSpec(
            num_scalar_prefetch=1, grid=(t // tm,),
            in_specs=[pl.BlockSpec((tm, LANES_V7X), lambda i, dest: (i, 0)),
                      pl.BlockSpec((tm, d), lambda i, dest: (i, 0)),
                      _mod_spec(which_gate, tm, n_lat, batch, d),
                      pl.BlockSpec(memory_space=pl.ANY)],
            out_specs=pl.BlockSpec((tm, d), lambda i, dest: (i, 0)),
            scratch_shapes=[pltpu.VMEM((2, 2, tm, d), F32), pltpu.SemaphoreType.DMA((2,))]),
        out_shape=jax.ShapeDtypeStruct((t, d), F32),
        compiler_params=_cparams(1), name="moe_combine",
    )(dest, meta, x, mods, ys)


def _rope_table(n_tokens):
    rows = n_tokens // GRID_W
    row = jnp.repeat(jnp.arange(rows), GRID_W).astype(F32)
    col = jnp.tile(jnp.arange(GRID_W), rows).astype(F32)
    pairs = HEAD_DIM // 4
    inv = ROPE_THETA ** (-jnp.arange(pairs, dtype=F32) / pairs)
    ang = jnp.concatenate([row[:, None] * inv, col[:, None] * inv], axis=-1)
    cos, sin = jnp.cos(ang), jnp.sin(ang)
    return jnp.concatenate([cos, cos, -sin, sin], axis=-1)


def _dft_tables(length):
    k = jnp.arange(length, dtype=jnp.int32)
    phase = (k[:, None] * k[None, :]) % (2 * length)
    ang = phase.astype(F32) * (math.pi / length)
    cm = jnp.cos(ang)
    alt = jnp.where(k % 2 == 0, 1.0, -1.0).astype(F32)
    fim = (-jnp.sin(ang)).at[0, :].set(alt)
    fw = jnp.stack([cm, fim]).astype(BF16)
    g = jnp.concatenate([cm, fim.T], axis=1).astype(BF16)
    return fw, g


def _hyena_features(length, k_pad):
    t = jnp.linspace(0.0, 1.0, length, dtype=F32)[:, None]
    w = 2.0 * math.pi * jnp.arange(length, dtype=F32)[:, None] / length
    f = jnp.linspace(1e-4, HYENA_BANDS - 1, HYENA_BANDS, dtype=F32)[None, :]
    z = jnp.concatenate([t, jnp.cos(f * w), -jnp.sin(f * w)], axis=-1)
    return jnp.pad(z, ((0, 0), (0, k_pad - z.shape[1])))


def _blockdiag_dense(w):
    nb, bs = w.shape[-3], w.shape[-2]
    eye = jnp.eye(nb, dtype=w.dtype)
    dense = jnp.einsum('...nde,nm->...ndme', w, eye)
    return dense.reshape(w.shape[:-3] + (nb * bs, nb * bs))


def kernel(x, c, ctx, c_ctx, w_ada, b_ada, g_mix, g_ffn, w_in, attn_sink, hy_short_w, hy_short_b, hy_w1, hy_b1, hy_freq, hy_w2, hy_b2, hy_w3, hy_skip, lru_conv_w, lru_conv_b, lru_wa, lru_ba, lru_wx, lru_bx, lru_lambda, w_br_attn, w_br_hy, w_br_lru, w_out, router_w, router_b, exp_w_gate, exp_w_up, exp_w_down, final_g):
    batch, n_lat, d = x.shape
    n_ctx = ctx.shape[1]
    depth = w_ada.shape[0]
    t_lat, t_ctx = batch * n_lat, batch * n_ctx
    hw = hy_skip.shape[2]
    lw = lru_lambda.shape[2]
    fh = hy_w2.shape[1]
    aw = N_Q_HEADS * HEAD_DIM
    kvw = N_KV_HEADS * HEAD_DIM
    q_off = 2 * kvw + lw
    hy_off = q_off + aw
    lru_y_off = hy_off + 3 * hw
    gate_off = lru_y_off + lw
    assert batch + 1 <= MOD_ROWS and t_lat % n_ctx == 0 and n_lat % ATTN_BLOCK == 0
    tile_base = math.gcd(n_lat, t_ctx)
    tm = _pick(tile_base, 512)
    tm_big = _pick(tile_base, 1024)

    xu = jnp.concatenate([x.reshape(t_lat, d), ctx.reshape(t_ctx, d)], axis=0)
    cc = jnp.concatenate([c, c_ctx[None, :], jnp.zeros((MOD_ROWS - batch - 1, d), F32)], axis=0)
    mods_all = ada_tables(cc, w_ada, b_ada).reshape(depth, MOD_ROWS, N_MOD, 1, d)

    rope_tab = _rope_table(n_lat)
    deltas = jnp.abs(jnp.linspace(math.log(HYENA_DECAY_TARGET) / HYENA_FAST_DECAY,
                                  math.log(HYENA_DECAY_TARGET) / HYENA_SLOW_DECAY, hw, dtype=F32))[None, :]
    seqs = []
    for length, row0 in ((n_lat, 0), (n_ctx, t_lat)):
        fw, g = _dft_tables(length)
        seqs.append((length, row0, fw, g, _hyena_features(length, fh)))
    hy_w1p = jnp.pad(hy_w1, ((0, 0), (0, fh - hy_w1.shape[1]), (0, 0)))

    sp = softplus_neg(lru_lambda)
    lru_w = jnp.concatenate([_blockdiag_dense(lru_wa), _blockdiag_dense(lru_wx)], axis=-1).astype(BF16)
    lru_b = jnp.concatenate([lru_ba, lru_bx], axis=-1)[:, :, None, :]
    wa_b, wh_b, wl_b, wo_b = (w.astype(BF16) for w in (w_br_attn, w_br_hy, w_br_lru, w_out))
    rw = jnp.pad(router_w, ((0, 0), (0, LANES_V7X - router_w.shape[1])))
    rb = jnp.pad(router_b, (0, LANES_V7X - router_b.shape[0]))[None, :]
    h_zero = jnp.zeros((batch, 2, lw), F32)
    t_all = t_lat + t_ctx
    n_exp = exp_w_gate.shape[1]
    exp_tile = _pick(2 * t_all, 512)
    n_exp_tiles = (2 * t_all) // exp_tile + n_exp
    tm_cmb = _pick(tile_base, 256)

    for l in range(depth):
        mods = mods_all[l]
        h = normmod(xu, g_mix[l], mods, 0, n_lat, batch, tm)
        p = in_proj(h, w_in, l, tm_big)

        o_att = latent_attention(p, attn_sink[l], rope_tab, batch, n_lat, n_ctx, q_off)
        o_att = context_attention(p, attn_sink[l], o_att, batch, n_lat, n_ctx, q_off)

        xc_c = dwconv(p, lru_conv_w[l], lru_conv_b[l], batch, n_ctx, t_lat, 2 * kvw, lw)
        hf, hb, h_end = rg_lru(xc_c, lru_w, l, lru_b[l], sp[l], h_zero, batch, n_ctx, t_all, t_lat)
        xc_l = dwconv(p, lru_conv_w[l], lru_conv_b[l], batch, n_lat, 0, 2 * kvw, lw)
        hf, hb, _ = rg_lru(xc_l, lru_w, l, lru_b[l], sp[l], h_end, batch, n_lat, t_all, 0, prev=(hf, hb))

        o_hy = None
        for length, row0, fw, g, feats in seqs:
            u = dwconv(p, hy_short_w[l], hy_short_b[l], batch, length, row0, hy_off, 3 * hw)
            hid = hyena_hidden(feats, hy_w1p[l], hy_b1[l], hy_freq[l], hy_w2[l], hy_b2[l])
            taps, kn = hyena_taps(hid, hy_w3[l], deltas)
            kspec = hyena_spectrum(fw, taps)
            yp = conv_forward(fw, u, 0, kspec, kn, 0, batch, length, hw)
            z1 = conv_inverse(g, yp, u, 0, u, hw, hy_skip[l], 0, batch, length, F32)
            yp = conv_forward(fw, z1, 0, kspec, kn, 1, batch, length, hw)
            o_hy = conv_inverse(g, yp, z1, 0, u, 2 * hw, hy_skip[l], 1, batch, length, BF16,
                                total_rows=t_all, row0=row0, prev=o_hy)

        m = merge_branches(o_att, o_hy, p, hf, hb, wa_b, wh_b, wl_b, l, lru_y_off, gate_off, tm)
        xu = out_proj_residual(m, wo_b, l, xu, mods, 2, n_lat, batch, tm)

        fl, route, counts = router(xu, g_ffn[l], mods, 3, rw, rb, n_lat, batch, tm)
        dest, tile_expert, n_used = dispatch_plan(route, counts, n_exp, exp_tile, n_exp_tiles)
        xs = moe_dispatch(fl, dest, n_exp_tiles * exp_tile, tm)
        ys = moe_experts(tile_expert, n_used, xs, exp_w_gate, exp_w_up, exp_w_down, l, exp_tile)
        xu = moe_combine(dest, route, xu, mods, 5, ys, n_lat, batch, tm_cmb)

    return final_norm(xu, final_g, t_lat, tm).reshape(batch, n_lat, d)
```

```python
import functools
import math

import jax
import jax.numpy as jnp
import numpy as np
from jax import lax
from jax.experimental import pallas as pl
from jax.experimental.pallas import tpu as pltpu

F32 = jnp.float32
BF16 = jnp.bfloat16
HIGHEST = lax.Precision.HIGHEST

LANES_V7X = 128
VMEM_LIMIT_V7X = 56 * 1024 * 1024

EPS = 1e-6
GRID_W = 64
HEAD_DIM = 128
N_Q_HEADS = 8
N_KV_HEADS = 2
Q_PER_KV = N_Q_HEADS // N_KV_HEADS
WINDOW = 128
ATTN_BLOCK = 128
ROPE_THETA = 10000.0
HYENA_BANDS = 16
HYENA_DECAY_TARGET = 1e-2
HYENA_FAST_DECAY = 0.3
HYENA_SLOW_DECAY = 1.5
LRU_C = 8.0
N_EXPERTS = 16
N_GROUPS = 4
EXPERTS_PER_GROUP = N_EXPERTS // N_GROUPS
N_MOD = 6
MOD_ROWS = 8


def _cparams(n_axes):
    return pltpu.CompilerParams(dimension_semantics=("arbitrary",) * n_axes,
                                vmem_limit_bytes=VMEM_LIMIT_V7X)


def _sigmoid(x):
    return 0.5 * (1.0 + jnp.tanh(0.5 * x))


def _pack_bf16_pairs(x):
    half = x.shape[1] // 2
    hi = lax.bitcast_convert_type(x[:, :half].astype(BF16).astype(F32), jnp.uint32)
    lo = lax.bitcast_convert_type(x[:, half:].astype(BF16).astype(F32), jnp.uint32)
    return hi | (lo >> 16)


def _unpack_bf16_pairs(w):
    hi = lax.bitcast_convert_type(w & jnp.uint32(0xFFFF0000), F32)
    lo = lax.bitcast_convert_type(w << 16, F32)
    return hi, lo


def _pick(n, cap):
    t = cap
    while n % t:
        t //= 2
    return t


def _ada_kernel(c_ref, w_ref, b_ref, o_ref):
    c = c_ref[...]
    s = (c * jax.nn.sigmoid(c)).astype(BF16)
    o_ref[0] = jnp.dot(s, w_ref[0].astype(BF16), preferred_element_type=F32) + b_ref[0]


def ada_tables(cc, w_ada, b_ada):
    depth, d, n6 = w_ada.shape
    tn = _pick(n6, 1024)
    return pl.pallas_call(
        _ada_kernel,
        grid=(depth, n6 // tn),
        in_specs=[pl.BlockSpec((MOD_ROWS, d), lambda l, j: (0, 0)),
                  pl.BlockSpec((1, d, tn), lambda l, j: (l, 0, j)),
                  pl.BlockSpec((1, 1, tn), lambda l, j: (l, 0, j))],
        out_specs=pl.BlockSpec((1, MOD_ROWS, tn), lambda l, j: (l, 0, j)),
        out_shape=jax.ShapeDtypeStruct((depth, MOD_ROWS, n6), F32),
        compiler_params=_cparams(2), name="ada_tables",
    )(cc, w_ada, b_ada.reshape(depth, 1, n6))


def _mod_spec(which, tm, n_lat, batch, d):
    return pl.BlockSpec((1, 1, 1, d),
                        lambda i, *_: (jnp.minimum((i * tm) // n_lat, batch), which, 0, 0))


def _normmod(x, g, shift, scale):
    y = x * lax.rsqrt(jnp.mean(x * x, axis=-1, keepdims=True) + EPS) * g
    return y * (1.0 + scale) + shift


def _normmod_kernel(x_ref, g_ref, sh_ref, sc_ref, o_ref):
    o_ref[...] = _normmod(x_ref[...], g_ref[...], sh_ref[0, 0], sc_ref[0, 0]).astype(o_ref.dtype)


def normmod(x, g, mods, which_shift, n_lat, batch, tm):
    t, d = x.shape
    return pl.pallas_call(
        _normmod_kernel,
        grid=(t // tm,),
        in_specs=[pl.BlockSpec((tm, d), lambda i: (i, 0)),
                  pl.BlockSpec((1, d), lambda i: (0, 0)),
                  _mod_spec(which_shift, tm, n_lat, batch, d),
                  _mod_spec(which_shift + 1, tm, n_lat, batch, d)],
        out_specs=pl.BlockSpec((tm, d), lambda i: (i, 0)),
        out_shape=jax.ShapeDtypeStruct((t, d), BF16),
        compiler_params=_cparams(1), name="normmod",
    )(x, g.reshape(1, d), mods, mods)


def _final_norm_kernel(x_ref, g_ref, o_ref):
    x = x_ref[...]
    o_ref[...] = x * lax.rsqrt(jnp.mean(x * x, axis=-1, keepdims=True) + EPS) * g_ref[...]


def final_norm(x, g, rows, tm):
    d = x.shape[1]
    return pl.pallas_call(
        _final_norm_kernel,
        grid=(rows // tm,),
        in_specs=[pl.BlockSpec((tm, d), lambda i: (i, 0)), pl.BlockSpec((1, d), lambda i: (0, 0))],
        out_specs=pl.BlockSpec((tm, d), lambda i: (i, 0)),
        out_shape=jax.ShapeDtypeStruct((rows, d), F32),
        compiler_params=_cparams(1), name="final_norm",
    )(x, g.reshape(1, d))


def _proj_kernel(a_ref, w_ref, o_ref, wb_ref):
    @pl.when(pl.program_id(1) == 0)
    def _():
        wb_ref[...] = w_ref[0].astype(BF16)

    o_ref[...] = jnp.dot(a_ref[...], wb_ref[...], preferred_element_type=F32).astype(o_ref.dtype)


def in_proj(a, w, layer, tm, col0, n, out_dtype):
    t, k = a.shape
    tn = _pick(math.gcd(n, col0) if col0 else n, 1024)
    return pl.pallas_call(
        _proj_kernel,
        grid=(n // tn, t // tm),
        in_specs=[pl.BlockSpec((tm, k), lambda j, i: (i, 0)),
                  pl.BlockSpec((1, k, tn), lambda j, i: (layer, 0, col0 // tn + j))],
        out_specs=pl.BlockSpec((tm, tn), lambda j, i: (i, j)),
        out_shape=jax.ShapeDtypeStruct((t, n), out_dtype),
        scratch_shapes=[pltpu.VMEM((k, tn), BF16)],
        compiler_params=_cparams(2), name="in_proj",
    )(a, w)


def _rope(x, tab):
    return x * tab[:, :HEAD_DIM] + pltpu.roll(x, HEAD_DIM // 2, axis=1) * tab[:, HEAD_DIM:]


def _sink_column(sink_ref, h, rows):
    r = lax.broadcasted_iota(jnp.int32, (Q_PER_KV * rows, 1), 0)
    col = jnp.full((Q_PER_KV * rows, 1), sink_ref[h * Q_PER_KV], F32)
    for g in range(1, Q_PER_KV):
        col = jnp.where(r >= g * rows, sink_ref[h * Q_PER_KV + g], col)
    return col


def _nt(a, b):
    return lax.dot_general(a, b, (((1,), (1,)), ((), ())), preferred_element_type=F32)


def _lat_attn_kernel(sink_ref, q_ref, kvm_ref, kv0_ref, kvp_ref, kvc_ref, tm_ref, t0_ref, tp_ref, o_ref,
                     *, n_lat):
    i = pl.program_id(1)
    blk = ATTN_BLOCK
    kv_w = N_KV_HEADS * HEAD_DIM
    scale = HEAD_DIM ** -0.5
    tabs = (tm_ref[...], t0_ref[...], tp_ref[...])
    kvs = (kvm_ref, kv0_ref, kvp_ref)
    qi = lax.broadcasted_iota(jnp.int32, (Q_PER_KV * blk, 3 * blk), 0) & (blk - 1)
    kj = lax.broadcasted_iota(jnp.int32, (Q_PER_KV * blk, 3 * blk), 1)
    key_pos = (i - 1) * blk + kj
    ok = (jnp.abs(qi + blk - kj) <= WINDOW) & (key_pos >= 0) & (key_pos < n_lat)
    for h in range(N_KV_HEADS):
        ks = slice(h * HEAD_DIM, (h + 1) * HEAD_DIM)
        vs = slice(kv_w + h * HEAD_DIM, kv_w + (h + 1) * HEAD_DIM)
        k_loc = jnp.concatenate([_rope(kvs[n][:, ks], tabs[n]) for n in range(3)], axis=0).astype(BF16)
        v_loc = jnp.concatenate([kvs[n][:, vs] for n in range(3)], axis=0).astype(BF16)
        k_ctx = kvc_ref[:, ks].astype(BF16)
        v_ctx = kvc_ref[:, vs].astype(BF16)
        q4 = jnp.concatenate(
            [_rope(q_ref[:, (h * Q_PER_KV + g) * HEAD_DIM:(h * Q_PER_KV + g + 1) * HEAD_DIM], tabs[1])
             for g in range(Q_PER_KV)], axis=0).astype(BF16)
        s_loc = jnp.where(ok, _nt(q4, k_loc) * scale, -jnp.inf)
        s_ctx = _nt(q4, k_ctx) * scale
        sink = _sink_column(sink_ref, h, blk)
        m = jnp.maximum(jnp.maximum(jnp.max(s_loc, axis=-1, keepdims=True),
                                    jnp.max(s_ctx, axis=-1, keepdims=True)), sink)
        p_loc = jnp.exp(s_loc - m)
        p_ctx = jnp.exp(s_ctx - m)
        den = (jnp.sum(p_loc, axis=-1, keepdims=True) + jnp.sum(p_ctx, axis=-1, keepdims=True)
               + jnp.exp(sink - m))
        o = (jnp.dot(p_loc.astype(BF16), v_loc, preferred_element_type=F32)
             + jnp.dot(p_ctx.astype(BF16), v_ctx, preferred_element_type=F32)) / den
        for g in range(Q_PER_KV):
            hq = h * Q_PER_KV + g
            o_ref[:, hq * HEAD_DIM:(hq + 1) * HEAD_DIM] = o[g * blk:(g + 1) * blk].astype(o_ref.dtype)


def latent_attention(p, sink, rope_tab, batch, n_lat, n_ctx, q_off):
    blk = ATTN_BLOCK
    nb = n_lat // blk
    aw = N_Q_HEADS * HEAD_DIM
    kvw = 2 * N_KV_HEADS * HEAD_DIM
    ctx_blk0 = (batch * n_lat) // n_ctx
    kv_spec = lambda off: pl.BlockSpec(
        (blk, kvw), lambda b, i: (b * nb + jnp.clip(i + off, 0, nb - 1), 0))
    tab_spec = lambda off: pl.BlockSpec(
        (blk, 2 * HEAD_DIM), lambda b, i: (jnp.clip(i + off, 0, nb - 1), 0))
    return pl.pallas_call(
        functools.partial(_lat_attn_kernel, n_lat=n_lat),
        grid=(batch, nb),
        in_specs=[pl.BlockSpec(memory_space=pltpu.SMEM),
                  pl.BlockSpec((blk, aw), lambda b, i: (b * nb + i, q_off // aw)),
                  kv_spec(-1), kv_spec(0), kv_spec(1),
                  pl.BlockSpec((n_ctx, kvw), lambda b, i: (ctx_blk0 + b, 0)),
                  tab_spec(-1), tab_spec(0), tab_spec(1)],
        out_specs=pl.BlockSpec((blk, aw), lambda b, i: (b * nb + i, 0)),
        out_shape=jax.ShapeDtypeStruct((p.shape[0], aw), BF16),
        compiler_params=_cparams(2), name="latent_attention",
    )(sink, p, p, p, p, p, rope_tab, rope_tab, rope_tab)


def _ctx_attn_kernel(sink_ref, q_ref, kv_ref, o_lat_ref, o_ref):
    del o_lat_ref
    rows = q_ref.shape[0]
    kv_w = N_KV_HEADS * HEAD_DIM
    scale = HEAD_DIM ** -0.5
    for h in range(N_KV_HEADS):
        k = kv_ref[:, h * HEAD_DIM:(h + 1) * HEAD_DIM].astype(BF16)
        v = kv_ref[:, kv_w + h * HEAD_DIM:kv_w + (h + 1) * HEAD_DIM].astype(BF16)
        q4 = jnp.concatenate(
            [q_ref[:, (h * Q_PER_KV + g) * HEAD_DIM:(h * Q_PER_KV + g + 1) * HEAD_DIM]
             for g in range(Q_PER_KV)], axis=0).astype(BF16)
        s = _nt(q4, k) * scale
        sink = _sink_column(sink_ref, h, rows)
        m = jnp.maximum(jnp.max(s, axis=-1, keepdims=True), sink)
        p = jnp.exp(s - m)
        den = jnp.sum(p, axis=-1, keepdims=True) + jnp.exp(sink - m)
        o = jnp.dot(p.astype(BF16), v, preferred_element_type=F32) / den
        for g in range(Q_PER_KV):
            hq = h * Q_PER_KV + g
            o_ref[:, hq * HEAD_DIM:(hq + 1) * HEAD_DIM] = o[g * rows:(g + 1) * rows].astype(o_ref.dtype)


def context_attention(p, sink, o_lat, batch, n_lat, n_ctx, q_off):
    aw = N_Q_HEADS * HEAD_DIM
    kvw = 2 * N_KV_HEADS * HEAD_DIM
    ctx_blk0 = (batch * n_lat) // n_ctx
    return pl.pallas_call(
        _ctx_attn_kernel,
        grid=(batch,),
        in_specs=[pl.BlockSpec(memory_space=pltpu.SMEM),
                  pl.BlockSpec((n_ctx, aw), lambda b: (ctx_blk0 + b, q_off // aw)),
                  pl.BlockSpec((n_ctx, kvw), lambda b: (ctx_blk0 + b, 0)),
                  pl.BlockSpec(memory_space=pl.ANY)],
        out_specs=pl.BlockSpec((n_ctx, aw), lambda b: (ctx_blk0 + b, 0)),
        out_shape=jax.ShapeDtypeStruct(o_lat.shape, o_lat.dtype),
        input_output_aliases={3: 0},
        compiler_params=_cparams(1), name="context_attention",
    )(sink, p, p, o_lat)


def _dwconv_kernel(u_ref, w_ref, b_ref, o_ref, *, width):
    x = u_ref[...]
    length = x.shape[0]
    left = width // 2
    t = lax.broadcasted_iota(jnp.int32, (length, 1), 0)
    acc = jnp.broadcast_to(b_ref[...], x.shape)
    for j in range(width):
        s = j - left
        if s == 0:
            xs = x
        else:
            xs = pltpu.roll(x, (-s) % length, axis=0)
            xs = jnp.where((t + s >= 0) & (t + s < length), xs, 0.0)
        acc = acc + xs * w_ref[j:j + 1, :]
    o_ref[...] = acc


def dwconv(p, w, b, n_seq, length, row0, col0, n_ch):
    width = w.shape[0]
    cb = _pick(n_ch, 256)
    return pl.pallas_call(
        functools.partial(_dwconv_kernel, width=width),
        grid=(n_seq, n_ch // cb),
        in_specs=[pl.BlockSpec((length, cb), lambda s, c: (row0 // length + s, col0 // cb + c)),
                  pl.BlockSpec((width, cb), lambda s, c: (0, c)),
                  pl.BlockSpec((1, cb), lambda s, c: (0, c))],
        out_specs=pl.BlockSpec((length, cb), lambda s, c: (s, c)),
        out_shape=jax.ShapeDtypeStruct((n_seq * length, n_ch), F32),
        compiler_params=_cparams(2), name="dwconv",
    )(p, w, b.reshape(1, n_ch))


def _lru_kernel(xf_ref, xb_ref, w_ref, bias_ref, sp_ref, h0_ref, *rest):
    hf_ref, hb_ref, hl_ref, af, bf, ab, bb, carry = rest[-8:]
    j = pl.program_id(1)
    width = xf_ref.shape[1]

    @pl.when(j == 0)
    def _():
        carry[...] = h0_ref[0]

    def gates(x, d):
        g = jnp.dot(x.astype(BF16), w_ref[0, d], preferred_element_type=F32) + bias_ref[d]
        r = _sigmoid(g[:, :width])
        gi = _sigmoid(g[:, width:])
        a = jnp.exp(-LRU_C * r * sp_ref[d])
        return a, jnp.sqrt(1.0 - a * a) * (gi * x)

    af[...], bf[...] = gates(xf_ref[...], 0)
    ab[...], bb[...] = gates(xb_ref[...], 1)
    tl = af.shape[0]

    def body(t, hs):
        hf, hb = hs
        hf = af[pl.ds(t, 1), :] * hf + bf[pl.ds(t, 1), :]
        hf_ref[pl.ds(t, 1), :] = hf
        tb = tl - 1 - t
        hb = ab[pl.ds(tb, 1), :] * hb + bb[pl.ds(tb, 1), :]
        hb_ref[pl.ds(tb, 1), :] = hb
        return hf, hb

    hf, hb = lax.fori_loop(0, tl, body, (carry[0:1, :], carry[1:2, :]), unroll=8)
    carry[0:1, :] = hf
    carry[1:2, :] = hb
    hl_ref[0] = carry[...]


def rg_lru(xc, w_gates, layer, b_gates, softplus_neg_lam, h0, n_seq, length, total_rows, row0, prev=None):
    width = xc.shape[1]
    tl = _pick(length, 512)
    nc = length // tl
    blk0 = row0 // tl
    in_specs = [pl.BlockSpec((tl, width), lambda s, j: (s * nc + j, 0)),
                pl.BlockSpec((tl, width), lambda s, j: (s * nc + nc - 1 - j, 0)),
                pl.BlockSpec((1, 2, width, 2 * width), lambda s, j: (layer, 0, 0, 0)),
                pl.BlockSpec((2, 1, 2 * width), lambda s, j: (0, 0, 0)),
                pl.BlockSpec((2, 1, width), lambda s, j: (0, 0, 0)),
                pl.BlockSpec((1, 2, width), lambda s, j: (s, 0, 0))]
    args = [xc, xc, w_gates, b_gates, softplus_neg_lam, h0]
    aliases = {}
    if prev is not None:
        in_specs += [pl.BlockSpec(memory_space=pl.ANY)] * 2
        aliases = {len(args): 0, len(args) + 1: 1}
        args += list(prev)
    return pl.pallas_call(
        _lru_kernel,
        grid=(n_seq, nc),
        in_specs=in_specs,
        out_specs=[pl.BlockSpec((tl, width), lambda s, j: (blk0 + s * nc + j, 0)),
                   pl.BlockSpec((tl, width), lambda s, j: (blk0 + s * nc + nc - 1 - j, 0)),
                   pl.BlockSpec((1, 2, width), lambda s, j: (s, 0, 0))],
        out_shape=[jax.ShapeDtypeStruct((total_rows, width), F32),
                   jax.ShapeDtypeStruct((total_rows, width), F32),
                   jax.ShapeDtypeStruct((n_seq, 2, width), F32)],
        scratch_shapes=[pltpu.VMEM((tl, width), F32)] * 4 + [pltpu.VMEM((2, width), F32)],
        input_output_aliases=aliases,
        compiler_params=_cparams(2), name="rg_lru",
    )(*args)


def _softplus_kernel(x_ref, o_ref):
    x = -x_ref[...]
    o_ref[...] = jnp.maximum(x, 0.0) + jnp.log(1.0 + jnp.exp(-jnp.abs(x)))


def softplus_neg(lam):
    depth, two, width = lam.shape
    x = lam.reshape(depth * two, width)
    out = pl.pallas_call(
        _softplus_kernel,
        out_shape=jax.ShapeDtypeStruct(x.shape, F32), name="softplus_neg",
    )(x)
    return out.reshape(depth, two, 1, width)


def _hy_hidden_kernel(z_ref, w1_ref, b1_ref, fr_ref, w2_ref, b2_ref, o_ref):
    fr = fr_ref[...]
    h = jnp.sin(fr * (jnp.dot(z_ref[...], w1_ref[...], precision=HIGHEST,
                              preferred_element_type=F32) + b1_ref[...]))
    o_ref[...] = jnp.sin(fr * (jnp.dot(h, w2_ref[...], precision=HIGHEST,
                                       preferred_element_type=F32) + b2_ref[...]))


def hyena_hidden(z, w1, b1, freq, w2, b2):
    length = z.shape[0]
    fh = w2.shape[0]
    return pl.pallas_call(
        _hy_hidden_kernel,
        out_shape=jax.ShapeDtypeStruct((length, fh), F32), name="hyena_hidden",
    )(z, w1, b1.reshape(1, fh), freq.reshape(1, fh), w2, b2.reshape(1, fh))


def _hy_taps_kernel(hid_ref, wf_ref, wb_ref, delta_ref, u_ref, kn_ref):
    hid = hid_ref[...]
    length = hid.shape[0]
    hf = jnp.dot(hid, wf_ref[...], precision=HIGHEST, preferred_element_type=F32)
    hb = jnp.dot(hid, wb_ref[...], precision=HIGHEST, preferred_element_type=F32)
    ti = lax.broadcasted_iota(jnp.int32, (length, 1), 0)
    win = jnp.exp(-(ti.astype(F32) * (1.0 / (length - 1))) * delta_ref[...])
    f = hf * win
    b = jnp.where(ti >= 1, hb * win, 0.0)
    sc = lax.rsqrt(jnp.sum(f * f + b * b, axis=0, keepdims=True) + EPS)
    u1 = (f + b) * sc
    u2 = (f - b) * sc
    sign = jnp.where((ti & 1) == 0, 1.0, -1.0)
    kn_ref[...] = jnp.sum(u1 * sign, axis=0, keepdims=True)
    u_ref[0] = u1.astype(u_ref.dtype)
    u_ref[1] = u2.astype(u_ref.dtype)


def hyena_taps(hid, w3, deltas):
    length, fh = hid.shape
    hw = deltas.shape[1]
    n_ord = w3.shape[1] // (2 * hw)
    cw = _pick(hw, 128)
    nc = hw // cw
    return pl.pallas_call(
        _hy_taps_kernel,
        grid=(n_ord, nc),
        in_specs=[pl.BlockSpec((length, fh), lambda o, c: (0, 0)),
                  pl.BlockSpec((fh, cw), lambda o, c: (0, o * 2 * nc + c)),
                  pl.BlockSpec((fh, cw), lambda o, c: (0, o * 2 * nc + nc + c)),
                  pl.BlockSpec((1, cw), lambda o, c: (0, c))],
        out_specs=[pl.BlockSpec((2, length, cw), lambda o, c: (0, 0, o * nc + c)),
                   pl.BlockSpec((1, cw), lambda o, c: (0, o * nc + c))],
        out_shape=[jax.ShapeDtypeStruct((2, length, n_ord * hw), BF16),
                   jax.ShapeDtypeStruct((1, n_ord * hw), F32)],
        compiler_params=_cparams(2), name="hyena_taps",
    )(hid, w3, w3, deltas)


def _spectrum_kernel(fw_ref, u_ref, o_ref):
    o_ref[0] = jnp.dot(fw_ref[0], u_ref[0], preferred_element_type=F32)


def hyena_spectrum(fw, u):
    _, length, n = u.shape
    tm = _pick(length, 512)
    return pl.pallas_call(
        _spectrum_kernel,
        grid=(2, length // tm),
        in_specs=[pl.BlockSpec((1, tm, length), lambda h, i: (h, i, 0)),
                  pl.BlockSpec((1, length, n), lambda h, i: (h, 0, 0))],
        out_specs=pl.BlockSpec((1, tm, n), lambda h, i: (h, i, 0)),
        out_shape=jax.ShapeDtypeStruct((2, length, n), F32),
        compiler_params=_cparams(2), name="hyena_spectrum",
    )(fw, u)


def _conv_fwd_kernel(fw_ref, z_ref, k_ref, kn_ref, y_ref, zb_ref):
    i = pl.program_id(1)

    @pl.when(i == 0)
    def _():
        zb_ref[...] = z_ref[...].astype(BF16)

    z = zb_ref[...]
    zr = jnp.dot(fw_ref[0], z, preferred_element_type=F32)
    zi = jnp.dot(fw_ref[1], z, preferred_element_type=F32)
    kr = k_ref[0]
    ki = k_ref[1]
    yr = zr * kr - zi * ki
    yi = zr * ki + zi * kr
    first = (lax.broadcasted_iota(jnp.int32, (zr.shape[0], 1), 0) == 0) & (i == 0)
    y_ref[0, 0] = jnp.where(first, 0.5 * zr * kr, yr).astype(y_ref.dtype)
    y_ref[0, 1] = jnp.where(first, 0.5 * zi * kn_ref[...], yi).astype(y_ref.dtype)


def conv_forward(fw, z, col0, kspec, kn, order, n_seq, length, hw):
    tm = _pick(length, 512)
    return pl.pallas_call(
        _conv_fwd_kernel,
        grid=(n_seq, length // tm),
        in_specs=[pl.BlockSpec((2, tm, length), lambda s, i: (0, i, 0)),
                  pl.BlockSpec((length, hw), lambda s, i: (s, col0 // hw)),
                  pl.BlockSpec((2, tm, hw), lambda s, i: (0, i, order)),
                  pl.BlockSpec((1, hw), lambda s, i: (0, order))],
        out_specs=pl.BlockSpec((1, 2, tm, hw), lambda s, i: (s, 0, i, 0)),
        out_shape=jax.ShapeDtypeStruct((n_seq, 2, length, hw), BF16),
        scratch_shapes=[pltpu.VMEM((length, hw), BF16)],
        compiler_params=_cparams(2), name="hyena_conv_fwd",
    )(fw, z, kspec, kn)


def _conv_inv_kernel(g_ref, y_ref, z_ref, gate_ref, skip_ref, *rest, inv_len):
    o_ref = rest[-1]
    y = jnp.dot(g_ref[...], y_ref[0], preferred_element_type=F32) * inv_len
    o_ref[...] = (gate_ref[...] * (y + skip_ref[0] * z_ref[...])).astype(o_ref.dtype)


def conv_inverse(g, yp, z, z_col0, gates, gate_col0, skip, order, n_seq, length, out_dtype,
                 total_rows=None, row0=0, prev=None):
    hw = yp.shape[3]
    tm = _pick(length, 512)
    nt = length // tm
    blk0 = row0 // tm
    total_rows = n_seq * length if total_rows is None else total_rows
    in_specs = [pl.BlockSpec((tm, 2 * length), lambda s, i: (i, 0)),
                pl.BlockSpec((1, 2 * length, hw), lambda s, i: (s, 0, 0)),
                pl.BlockSpec((tm, hw), lambda s, i: (s * nt + i, z_col0 // hw)),
                pl.BlockSpec((tm, hw), lambda s, i: (s * nt + i, gate_col0 // hw)),
                pl.BlockSpec((1, 1, hw), lambda s, i: (order, 0, 0))]
    args = [g, yp.reshape(n_seq, 2 * length, hw), z, gates, skip.reshape(skip.shape[0], 1, hw)]
    aliases = {}
    if prev is not None:
        in_specs.append(pl.BlockSpec(memory_space=pl.ANY))
        aliases = {len(args): 0}
        args.append(prev)
    return pl.pallas_call(
        functools.partial(_conv_inv_kernel, inv_len=1.0 / length),
        grid=(n_seq, nt),
        in_specs=in_specs,
        out_specs=pl.BlockSpec((tm, hw), lambda s, i: (blk0 + s * nt + i, 0)),
        out_shape=jax.ShapeDtypeStruct((total_rows, hw), out_dtype),
        input_output_aliases=aliases,
        compiler_params=_cparams(2), name="hyena_conv_inv",
    )(*args)


def _merge_kernel(oa_ref, oh_ref, y_ref, hf_ref, hb_ref, g_ref, wa_ref, wh_ref, wl_ref, o_ref, *, tn):
    d = o_ref.shape[1]
    ol = (jax.nn.gelu(y_ref[...], approximate=True) * (hf_ref[...] + hb_ref[...])).astype(BF16)
    oa = oa_ref[...]
    oh = oh_ref[...]
    for c0 in range(0, d, tn):
        cols = slice(c0, c0 + tn)
        gate = lambda k: _sigmoid(g_ref[:, k * d + c0:k * d + c0 + tn].astype(F32))
        m = (gate(0) * jnp.dot(oa, wa_ref[0, :, cols], preferred_element_type=F32)
             + gate(1) * jnp.dot(oh, wh_ref[0, :, cols], preferred_element_type=F32)
             + gate(2) * jnp.dot(ol, wl_ref[0, :, cols], preferred_element_type=F32))
        o_ref[:, cols] = m.astype(o_ref.dtype)


def merge_branches(o_att, o_hy, p, gates, hf, hb, wa, wh, wl, layer, lru_y_off, tm):
    t, aw = o_att.shape
    hw = o_hy.shape[1]
    lw = hf.shape[1]
    d = wa.shape[2]
    w_spec = lambda rows: pl.BlockSpec((1, rows, d), lambda i: (layer, 0, 0))
    return pl.pallas_call(
        functools.partial(_merge_kernel, tn=_pick(d, 512)),
        grid=(t // tm,),
        in_specs=[pl.BlockSpec((tm, aw), lambda i: (i, 0)),
                  pl.BlockSpec((tm, hw), lambda i: (i, 0)),
                  pl.BlockSpec((tm, lw), lambda i: (i, lru_y_off // lw)),
                  pl.BlockSpec((tm, lw), lambda i: (i, 0)),
                  pl.BlockSpec((tm, lw), lambda i: (i, 0)),
                  pl.BlockSpec((tm, 3 * d), lambda i: (i, 0)),
                  w_spec(aw), w_spec(hw), w_spec(lw)],
        out_specs=pl.BlockSpec((tm, d), lambda i: (i, 0)),
        out_shape=jax.ShapeDtypeStruct((t, d), BF16),
        compiler_params=_cparams(1), name="merge_branches",
    )(o_att, o_hy, p, hf, hb, gates, wa, wh, wl)


def _out_proj_kernel(m_ref, w_ref, x_ref, gate_ref, o_ref):
    o_ref[...] = x_ref[...] + gate_ref[0, 0] * jnp.dot(m_ref[...], w_ref[0],
                                                       preferred_element_type=F32)


def out_proj_residual(m, w, layer, x, mods, which_gate, n_lat, batch, tm):
    t, d = x.shape
    return pl.pallas_call(
        _out_proj_kernel,
        grid=(t // tm,),
        in_specs=[pl.BlockSpec((tm, d), lambda i: (i, 0)),
                  pl.BlockSpec((1, d, d), lambda i: (layer, 0, 0)),
                  pl.BlockSpec((tm, d), lambda i: (i, 0)),
                  _mod_spec(which_gate, tm, n_lat, batch, d)],
        out_specs=pl.BlockSpec((tm, d), lambda i: (i, 0)),
        out_shape=jax.ShapeDtypeStruct((t, d), F32),
        compiler_params=_cparams(1), name="out_proj_residual",
    )(m, w, x, mods)


def _router_kernel(x_ref, g_ref, sh_ref, sc_ref, rw_ref, rb_ref, h_ref, meta_ref, cnt_ref):
    h = _normmod(x_ref[...], g_ref[...], sh_ref[0, 0], sc_ref[0, 0])
    h_ref[...] = _pack_bf16_pairs(h)
    w = rw_ref[...]
    h_hi, w_hi = h.astype(BF16), w.astype(BF16)
    h_lo = (h - h_hi.astype(F32)).astype(BF16)
    w_lo = (w - w_hi.astype(F32)).astype(BF16)
    logits = (jnp.dot(h_hi, w_hi, preferred_element_type=F32) + jnp.dot(h_lo, w_hi, preferred_element_type=F32)
              + jnp.dot(h_hi, w_lo, preferred_element_type=F32)) + rb_ref[...]
    lane = lax.broadcasted_iota(jnp.int32, logits.shape, 1)
    valid = lane < N_EXPERTS
    logits = jnp.where(valid, logits, -jnp.inf)
    ex = jnp.exp(logits - jnp.max(logits, axis=-1, keepdims=True))
    scores = ex / jnp.sum(ex, axis=-1, keepdims=True)
    grp = lax.shift_right_logical(lane, int(math.log2(EXPERTS_PER_GROUP)))
    best = jnp.zeros((logits.shape[0], 1), jnp.int32)
    best_max = jnp.max(jnp.where((grp == 0) & valid, scores, -1.0), axis=-1, keepdims=True)
    for g in range(1, N_GROUPS):
        gm = jnp.max(jnp.where((grp == g) & valid, scores, -1.0), axis=-1, keepdims=True)
        better = gm > best_max
        best = jnp.where(better, g, best)
        best_max = jnp.where(better, gm, best_max)
    s1 = jnp.where((grp == best) & valid, scores, -1.0)
    m1 = jnp.max(s1, axis=-1, keepdims=True)
    i1 = jnp.min(jnp.where(s1 == m1, lane, LANES_V7X), axis=-1, keepdims=True)
    s2 = jnp.where(lane == i1, -1.0, s1)
    m2 = jnp.max(s2, axis=-1, keepdims=True)
    i2 = jnp.min(jnp.where(s2 == m2, lane, LANES_V7X), axis=-1, keepdims=True)
    tot = m1 + m2
    @pl.when(pl.program_id(0) == 0)
    def _():
        cnt_ref[...] = jnp.zeros_like(cnt_ref)

    tm = logits.shape[0]
    onehot = ((lane == i1) | (lane == i2)).astype(BF16)
    lower = (lax.broadcasted_iota(jnp.int32, (tm, tm), 0)
             > lax.broadcasted_iota(jnp.int32, (tm, tm), 1)).astype(BF16)
    before = jnp.dot(lower, onehot, preferred_element_type=F32) + cnt_ref[...]
    r1 = jnp.sum(jnp.where(lane == i1, before, 0.0), axis=-1, keepdims=True)
    r2 = jnp.sum(jnp.where(lane == i2, before, 0.0), axis=-1, keepdims=True)
    cnt_ref[...] += jnp.sum(onehot.astype(F32), axis=0, keepdims=True)
    cols = (i1.astype(F32), i2.astype(F32), r1, r2, m1 / tot, m2 / tot)
    meta = jnp.zeros(logits.shape, F32)
    for k, col in enumerate(cols):
        meta = jnp.where(lane == k, col, meta)
    meta_ref[...] = meta


ROUTE_E, ROUTE_RANK, ROUTE_W = 0, 2, 4


def router(x, g, mods, which_shift, rw, rb, n_lat, batch, tm):
    t, d = x.shape
    return pl.pallas_call(
        _router_kernel,
        grid=(t // tm,),
        in_specs=[pl.BlockSpec((tm, d), lambda i: (i, 0)),
                  pl.BlockSpec((1, d), lambda i: (0, 0)),
                  _mod_spec(which_shift, tm, n_lat, batch, d),
                  _mod_spec(which_shift + 1, tm, n_lat, batch, d),
                  pl.BlockSpec((d, LANES_V7X), lambda i: (0, 0)),
                  pl.BlockSpec((1, LANES_V7X), lambda i: (0, 0))],
        out_specs=[pl.BlockSpec((tm, d // 2), lambda i: (i, 0)),
                   pl.BlockSpec((tm, LANES_V7X), lambda i: (i, 0)),
                   pl.BlockSpec((1, LANES_V7X), lambda i: (0, 0))],
        out_shape=[jax.ShapeDtypeStruct((t, d // 2), jnp.uint32),
                   jax.ShapeDtypeStruct((t, LANES_V7X), F32),
                   jax.ShapeDtypeStruct((1, LANES_V7X), F32)],
        compiler_params=_cparams(1), name="router",
    )(x, g.reshape(1, d), mods, mods, rw, rb)


def dispatch_plan(meta, counts, n_exp, tile, n_tiles):
    cnt = counts[0, :n_exp].astype(jnp.int32)
    tiles_e = (cnt + tile - 1) // tile
    tile_end = jnp.cumsum(tiles_e)
    row_start = (tile_end - tiles_e) * tile
    experts = meta[:, ROUTE_E:ROUTE_E + 2].astype(jnp.int32)
    ranks = meta[:, ROUTE_RANK:ROUTE_RANK + 2].astype(jnp.int32)
    dest = (row_start[experts] + ranks).reshape(-1)
    tile_expert = jnp.minimum(jnp.sum(jnp.arange(n_tiles)[:, None] >= tile_end[None, :], axis=1),
                              n_exp - 1).astype(jnp.int32)
    zero_start = jnp.minimum((row_start + cnt) // ZERO_ALIGN * ZERO_ALIGN,
                             (n_tiles - 1) * tile - ZERO_ALIGN).astype(jnp.int32)
    return dest, tile_expert, tile_end[-1:].astype(jnp.int32), zero_start


ZERO_ALIGN = 8


def _dispatch_kernel(dest_ref, zero_ref, h_ref, xs_ref, zbuf, sem, zsem):
    tm = h_ref.shape[0]
    base = pl.program_id(0) * (2 * tm)

    @pl.when(pl.program_id(0) == 0)
    def _():
        zbuf[...] = jnp.zeros_like(zbuf)

        def clear(e, carry):
            start = pl.multiple_of(zero_ref[e], ZERO_ALIGN)
            copy = pltpu.make_async_copy(zbuf, xs_ref.at[pl.ds(start, zbuf.shape[0])], zsem)
            copy.start()
            copy.wait()
            return carry

        lax.fori_loop(0, zero_ref.shape[0], clear, 0)

    def issue(r, carry):
        for k in range(2):
            row = dest_ref[base + 2 * r + k]
            pltpu.make_async_copy(h_ref.at[pl.ds(r, 1)], xs_ref.at[pl.ds(row, 1)], sem).start()
        return carry

    lax.fori_loop(0, tm, issue, 0, unroll=8)
    for k in range(2):
        pltpu.make_async_copy(h_ref, xs_ref.at[pl.ds(0, tm)], sem).wait()


def moe_dispatch(h, dest, zero_start, n_rows, tile, tm):
    t, d = h.shape
    return pl.pallas_call(
        _dispatch_kernel,
        grid_spec=pltpu.PrefetchScalarGridSpec(
            num_scalar_prefetch=2, grid=(t // tm,),
            in_specs=[pl.BlockSpec((tm, d), lambda i, dest, zs: (i, 0))],
            out_specs=pl.BlockSpec(memory_space=pl.ANY),
            scratch_shapes=[pltpu.VMEM((tile + ZERO_ALIGN, d), h.dtype),
                            pltpu.SemaphoreType.DMA, pltpu.SemaphoreType.DMA]),
        out_shape=jax.ShapeDtypeStruct((n_rows, d), h.dtype),
        compiler_params=_cparams(1), name="moe_dispatch",
    )(dest, zero_start, h)


def _experts_kernel(te_ref, nu_ref, xs_ref, wg_ref, wu_ref, wd_ref, ys_ref, wgb, wub, wdb):
    j = pl.program_id(0)

    @pl.when(j < nu_ref[0])
    def _():
        @pl.when((j == 0) | (te_ref[j] != te_ref[jnp.maximum(j - 1, 0)]))
        def _():
            wgb[...] = wg_ref[0, 0].astype(BF16)
            wub[...] = wu_ref[0, 0].astype(BF16)
            wdb[...] = wd_ref[0, 0].astype(BF16)

        x_hi, x_lo = (v.astype(BF16) for v in _unpack_bf16_pairs(xs_ref[...]))
        half = x_hi.shape[1]

        def proj(w):
            return (jnp.dot(x_hi, w[:half, :], preferred_element_type=F32)
                    + jnp.dot(x_lo, w[half:, :], preferred_element_type=F32))

        gt = proj(wgb)
        act = (gt * _sigmoid(gt)) * proj(wub)
        ys_ref[...] = _pack_bf16_pairs(jnp.dot(act.astype(BF16), wdb[...], preferred_element_type=F32))


def moe_experts(tile_expert, n_used, xs, wg, wu, wd, layer, tile):
    n_rows, dp = xs.shape
    d, de = wg.shape[2], wg.shape[3]
    used = lambda j, nu: jnp.minimum(j, nu[0] - 1)
    w_spec = lambda shape: pl.BlockSpec((1, 1) + shape, lambda j, te, nu: (layer, te[used(j, nu)], 0, 0))
    return pl.pallas_call(
        _experts_kernel,
        grid_spec=pltpu.PrefetchScalarGridSpec(
            num_scalar_prefetch=2, grid=(n_rows // tile,),
            in_specs=[pl.BlockSpec((tile, dp), lambda j, te, nu: (used(j, nu), 0)),
                      w_spec((d, de)), w_spec((d, de)), w_spec((de, d))],
            out_specs=pl.BlockSpec((tile, dp), lambda j, te, nu: (used(j, nu), 0)),
            scratch_shapes=[pltpu.VMEM((d, de), BF16), pltpu.VMEM((d, de), BF16),
                            pltpu.VMEM((de, d), BF16)]),
        out_shape=jax.ShapeDtypeStruct((n_rows, dp), xs.dtype),
        compiler_params=_cparams(1), name="moe_experts",
    )(tile_expert, n_used, xs, wg, wu, wd)


def _combine_kernel(dest_ref, meta_ref, x_ref, gate_ref, ys_ref, o_ref, buf, sem):
    i = pl.program_id(0)
    tm = x_ref.shape[0]

    def gather(tile, slot):
        base = tile * (2 * tm)

        def issue(r, carry):
            for k in range(2):
                row = dest_ref[base + 2 * r + k]
                pltpu.make_async_copy(ys_ref.at[pl.ds(row, 1)], buf.at[slot, k, pl.ds(r, 1)],
                                      sem.at[slot]).start()
            return carry

        lax.fori_loop(0, tm, issue, 0, unroll=8)

    @pl.when(i == 0)
    def _():
        gather(0, 0)

    @pl.when(i + 1 < pl.num_programs(0))
    def _():
        gather(i + 1, (i + 1) % 2)

    slot = i % 2
    for k in range(2):
        pltpu.make_async_copy(ys_ref.at[pl.ds(0, tm)], buf.at[slot, k], sem.at[slot]).wait()
    meta = meta_ref[...]
    w1 = meta[:, ROUTE_W:ROUTE_W + 1]
    w2 = meta[:, ROUTE_W + 1:ROUTE_W + 2]
    y1_hi, y1_lo = _unpack_bf16_pairs(buf[slot, 0])
    y2_hi, y2_lo = _unpack_bf16_pairs(buf[slot, 1])
    half = y1_hi.shape[1]
    gate = gate_ref[0, 0]
    o_ref[:, :half] = x_ref[:, :half] + gate[:, :half] * (w1 * y1_hi + w2 * y2_hi)
    o_ref[:, half:] = x_ref[:, half:] + gate[:, half:] * (w1 * y1_lo + w2 * y2_lo)


def moe_combine(dest, meta, x, mods, which_gate, ys, n_lat, batch, tm):
    t, d = x.shape
    return pl.pallas_call(
        _combine_kernel,
        grid_spec=pltpu.PrefetchScalarGridSpec(
            num_scalar_prefetch=1, grid=(t // tm,),
            in_specs=[pl.BlockSpec((tm, LANES_V7X), lambda i, dest: (i, 0)),
                      pl.BlockSpec((tm, d), lambda i, dest: (i, 0)),
                      _mod_spec(which_gate, tm, n_lat, batch, d),
                      pl.BlockSpec(memory_space=pl.ANY)],
            out_specs=pl.BlockSpec((tm, d), lambda i, dest: (i, 0)),
            scratch_shapes=[pltpu.VMEM((2, 2, tm, ys.shape[1]), ys.dtype), pltpu.SemaphoreType.DMA((2,))]),
        out_shape=jax.ShapeDtypeStruct((t, d), F32),
        compiler_params=_cparams(1), name="moe_combine",
    )(dest, meta, x, mods, ys)


def _rope_table(n_tokens):
    rows = n_tokens // GRID_W
    row = jnp.repeat(jnp.arange(rows), GRID_W).astype(F32)
    col = jnp.tile(jnp.arange(GRID_W), rows).astype(F32)
    pairs = HEAD_DIM // 4
    inv = ROPE_THETA ** (-jnp.arange(pairs, dtype=F32) / pairs)
    ang = jnp.concatenate([row[:, None] * inv, col[:, None] * inv], axis=-1)
    cos, sin = jnp.cos(ang), jnp.sin(ang)
    return jnp.concatenate([cos, cos, -sin, sin], axis=-1)


def _dft_kernel(ca_ref, sa_ref, cr_ref, sr_ref, fw_ref, g_ref):
    tm = cr_ref.shape[0]
    length = fw_ref.shape[2]
    cr = cr_ref[...]
    sr = sr_ref[...]
    row = pl.program_id(0) * tm + lax.broadcasted_iota(jnp.int32, (tm, 1), 0)
    alt_row = (1 - 2 * (row & 1)).astype(F32)
    for a in range(length // LANES_V7X):
        blk = slice(a * LANES_V7X, (a + 1) * LANES_V7X)
        ca = ca_ref[:, a:a + 1]
        sa = sa_ref[:, a:a + 1]
        cos_blk = ca * cr - sa * sr
        nsin_blk = -(sa * cr + ca * sr)
        col = a * LANES_V7X + lax.broadcasted_iota(jnp.int32, (1, LANES_V7X), 1)
        alt_col = (1 - 2 * (col & 1)).astype(F32)
        fw_ref[0, :, blk] = cos_blk.astype(fw_ref.dtype)
        fw_ref[1, :, blk] = jnp.where(row == 0, alt_col, nsin_blk).astype(fw_ref.dtype)
        g_ref[:, blk] = cos_blk.astype(g_ref.dtype)
        g_ref[:, length + a * LANES_V7X:length + (a + 1) * LANES_V7X] = (
            jnp.where(col == 0, alt_row, nsin_blk).astype(g_ref.dtype))


def _dft_tables(length):
    k = jnp.arange(length, dtype=jnp.int32)[:, None]
    n_hi = jnp.arange(length // LANES_V7X, dtype=jnp.int32)[None, :] * LANES_V7X
    n_lo = jnp.arange(LANES_V7X, dtype=jnp.int32)[None, :]
    ang = lambda n: ((k * n) % (2 * length)).astype(F32) * (math.pi / length)
    tm = _pick(length, 256)
    n_a = length // LANES_V7X
    return pl.pallas_call(
        _dft_kernel,
        grid=(length // tm,),
        in_specs=[pl.BlockSpec((tm, n_a), lambda i: (i, 0)), pl.BlockSpec((tm, n_a), lambda i: (i, 0)),
                  pl.BlockSpec((tm, LANES_V7X), lambda i: (i, 0)),
                  pl.BlockSpec((tm, LANES_V7X), lambda i: (i, 0))],
        out_specs=[pl.BlockSpec((2, tm, length), lambda i: (0, i, 0)),
                   pl.BlockSpec((tm, 2 * length), lambda i: (i, 0))],
        out_shape=[jax.ShapeDtypeStruct((2, length, length), BF16),
                   jax.ShapeDtypeStruct((length, 2 * length), BF16)],
        compiler_params=_cparams(1), name="dft_tables",
    )(jnp.cos(ang(n_hi)), jnp.sin(ang(n_hi)), jnp.cos(ang(n_lo)), jnp.sin(ang(n_lo)))


def _hyena_features(length, k_pad):
    t = jnp.linspace(0.0, 1.0, length, dtype=F32)[:, None]
    w = 2.0 * math.pi * jnp.arange(length, dtype=F32)[:, None] / length
    f = jnp.linspace(1e-4, HYENA_BANDS - 1, HYENA_BANDS, dtype=F32)[None, :]
    z = jnp.concatenate([t, jnp.cos(f * w), -jnp.sin(f * w)], axis=-1)
    return jnp.pad(z, ((0, 0), (0, k_pad - z.shape[1])))


def _blockdiag_dense(w):
    nb, bs = w.shape[-3], w.shape[-2]
    eye = jnp.eye(nb, dtype=w.dtype)
    dense = jnp.einsum('...nde,nm->...ndme', w, eye)
    return dense.reshape(w.shape[:-3] + (nb * bs, nb * bs))


def kernel(x, c, ctx, c_ctx, w_ada, b_ada, g_mix, g_ffn, w_in, attn_sink, hy_short_w, hy_short_b, hy_w1, hy_b1, hy_freq, hy_w2, hy_b2, hy_w3, hy_skip, lru_conv_w, lru_conv_b, lru_wa, lru_ba, lru_wx, lru_bx, lru_lambda, w_br_attn, w_br_hy, w_br_lru, w_out, router_w, router_b, exp_w_gate, exp_w_up, exp_w_down, final_g):
    batch, n_lat, d = x.shape
    n_ctx = ctx.shape[1]
    depth = w_ada.shape[0]
    t_lat, t_ctx = batch * n_lat, batch * n_ctx
    hw = hy_skip.shape[2]
    lw = lru_lambda.shape[2]
    fh = hy_w2.shape[1]
    aw = N_Q_HEADS * HEAD_DIM
    kvw = N_KV_HEADS * HEAD_DIM
    q_off = 2 * kvw + lw
    hy_off = q_off + aw
    lru_y_off = hy_off + 3 * hw
    gate_off = lru_y_off + lw
    assert batch + 1 <= MOD_ROWS and t_lat % n_ctx == 0 and n_lat % ATTN_BLOCK == 0
    tile_base = math.gcd(n_lat, t_ctx)
    tm = _pick(tile_base, 512)
    tm_big = _pick(tile_base, 1024)

    xu = jnp.concatenate([x.reshape(t_lat, d), ctx.reshape(t_ctx, d)], axis=0)
    cc = jnp.concatenate([c, c_ctx[None, :], jnp.zeros((MOD_ROWS - batch - 1, d), F32)], axis=0)
    mods_all = ada_tables(cc, w_ada, b_ada).reshape(depth, MOD_ROWS, N_MOD, 1, d)

    rope_tab = _rope_table(n_lat)
    deltas = jnp.abs(jnp.linspace(math.log(HYENA_DECAY_TARGET) / HYENA_FAST_DECAY,
                                  math.log(HYENA_DECAY_TARGET) / HYENA_SLOW_DECAY, hw, dtype=F32))[None, :]
    seqs = []
    for length, row0 in ((n_lat, 0), (n_ctx, t_lat)):
        fw, g = _dft_tables(length)
        seqs.append((length, row0, fw, g, _hyena_features(length, fh)))
    hy_w1p = jnp.pad(hy_w1, ((0, 0), (0, fh - hy_w1.shape[1]), (0, 0)))

    sp = softplus_neg(lru_lambda)
    lru_w = jnp.concatenate([_blockdiag_dense(lru_wa), _blockdiag_dense(lru_wx)], axis=-1).astype(BF16)
    lru_b = jnp.concatenate([lru_ba, lru_bx], axis=-1)[:, :, None, :]
    wa_b, wh_b, wl_b, wo_b = (w.astype(BF16) for w in (w_br_attn, w_br_hy, w_br_lru, w_out))
    rw = jnp.pad(router_w, ((0, 0), (0, LANES_V7X - router_w.shape[1])))
    rb = jnp.pad(router_b, (0, LANES_V7X - router_b.shape[0]))[None, :]
    h_zero = jnp.zeros((batch, 2, lw), F32)
    t_all = t_lat + t_ctx
    n_exp = exp_w_gate.shape[1]
    exp_tile = _pick(2 * t_all, 512)
    n_exp_tiles = (2 * t_all) // exp_tile + n_exp
    tm_cmb = _pick(tile_base, 256)

    for l in range(depth):
        mods = mods_all[l]
        h = normmod(xu, g_mix[l], mods, 0, n_lat, batch, tm)
        p = in_proj(h, w_in, l, tm_big, 0, gate_off, F32)
        gates = in_proj(h, w_in, l, tm_big, gate_off, w_in.shape[2] - gate_off, BF16)

        o_att = latent_attention(p, attn_sink[l], rope_tab, batch, n_lat, n_ctx, q_off)
        o_att = context_attention(p, attn_sink[l], o_att, batch, n_lat, n_ctx, q_off)

        xc_c = dwconv(p, lru_conv_w[l], lru_conv_b[l], batch, n_ctx, t_lat, 2 * kvw, lw)
        hf, hb, h_end = rg_lru(xc_c, lru_w, l, lru_b[l], sp[l], h_zero, batch, n_ctx, t_all, t_lat)
        xc_l = dwconv(p, lru_conv_w[l], lru_conv_b[l], batch, n_lat, 0, 2 * kvw, lw)
        hf, hb, _ = rg_lru(xc_l, lru_w, l, lru_b[l], sp[l], h_end, batch, n_lat, t_all, 0, prev=(hf, hb))

        o_hy = None
        for length, row0, fw, g, feats in seqs:
            u = dwconv(p, hy_short_w[l], hy_short_b[l], batch, length, row0, hy_off, 3 * hw)
            hid = hyena_hidden(feats, hy_w1p[l], hy_b1[l], hy_freq[l], hy_w2[l], hy_b2[l])
            taps, kn = hyena_taps(hid, hy_w3[l], deltas)
            kspec = hyena_spectrum(fw, taps)
            yp = conv_forward(fw, u, 0, kspec, kn, 0, batch, length, hw)
            z1 = conv_inverse(g, yp, u, 0, u, hw, hy_skip[l], 0, batch, length, F32)
            yp = conv_forward(fw, z1, 0, kspec, kn, 1, batch, length, hw)
            o_hy = conv_inverse(g, yp, z1, 0, u, 2 * hw, hy_skip[l], 1, batch, length, BF16,
                                total_rows=t_all, row0=row0, prev=o_hy)

        m = merge_branches(o_att, o_hy, p, gates, hf, hb, wa_b, wh_b, wl_b, l, lru_y_off, tm)
        xu = out_proj_residual(m, wo_b, l, xu, mods, 2, n_lat, batch, tm)

        fl, route, counts = router(xu, g_ffn[l], mods, 3, rw, rb, n_lat, batch, tm)
        dest, tile_expert, n_used, zero_start = dispatch_plan(route, counts, n_exp, exp_tile, n_exp_tiles)
        xs = moe_dispatch(fl, dest, zero_start, n_exp_tiles * exp_tile, exp_tile, tm)
        ys = moe_experts(tile_expert, n_used, xs, exp_w_gate, exp_w_up, exp_w_down, l, exp_tile)
        xu = moe_combine(dest, route, xu, mods, 5, ys, n_lat, batch, tm_cmb)

    return final_norm(xu, final_g, t_lat, tm).reshape(batch, n_lat, d)
```

```python
import functools
import math

import jax
import jax.numpy as jnp
import numpy as np
from jax import lax
from jax.experimental import pallas as pl
from jax.experimental.pallas import tpu as pltpu

F32 = jnp.float32
BF16 = jnp.bfloat16
HIGHEST = lax.Precision.HIGHEST

LANES_V7X = 128
VMEM_LIMIT_V7X = 56 * 1024 * 1024

EPS = 1e-6
GRID_W = 64
HEAD_DIM = 128
N_Q_HEADS = 8
N_KV_HEADS = 2
Q_PER_KV = N_Q_HEADS // N_KV_HEADS
WINDOW = 128
ATTN_BLOCK = 128
ROPE_THETA = 10000.0
HYENA_BANDS = 16
HYENA_DECAY_TARGET = 1e-2
HYENA_FAST_DECAY = 0.3
HYENA_SLOW_DECAY = 1.5
LRU_C = 8.0
N_EXPERTS = 16
N_GROUPS = 4
EXPERTS_PER_GROUP = N_EXPERTS // N_GROUPS
N_MOD = 6
MOD_ROWS = 8


def _cparams(n_axes):
    return pltpu.CompilerParams(dimension_semantics=("arbitrary",) * n_axes,
                                vmem_limit_bytes=VMEM_LIMIT_V7X)


def _sigmoid(x):
    return 0.5 * (1.0 + jnp.tanh(0.5 * x))


def _pack_bf16_pairs(x):
    half = x.shape[1] // 2
    hi = lax.bitcast_convert_type(x[:, :half].astype(BF16).astype(F32), jnp.uint32)
    lo = lax.bitcast_convert_type(x[:, half:].astype(BF16).astype(F32), jnp.uint32)
    return hi | (lo >> 16)


def _unpack_bf16_pairs(w):
    hi = lax.bitcast_convert_type(w & jnp.uint32(0xFFFF0000), F32)
    lo = lax.bitcast_convert_type(w << 16, F32)
    return hi, lo


SUBLANES_V7X = 8


def _store_tile_rows(ref, row0, x):
    rows, width = x.shape
    s_per_row = width // LANES_V7X
    for s in range(s_per_row):
        ref[pl.ds(row0 + s, rows, stride=s_per_row), :] = x[:, s * LANES_V7X:(s + 1) * LANES_V7X]


def _load_tile_rows(ref, row0, rows, s_per_row):
    return jnp.concatenate([ref[pl.ds(row0 + s, rows, stride=s_per_row), :] for s in range(s_per_row)], axis=1)


def _pick(n, cap):
    t = cap
    while n % t:
        t //= 2
    return t


def _ada_kernel(c_ref, w_ref, b_ref, o_ref):
    c = c_ref[...]
    s = (c * jax.nn.sigmoid(c)).astype(BF16)
    o_ref[0] = jnp.dot(s, w_ref[0].astype(BF16), preferred_element_type=F32) + b_ref[0]


def ada_tables(cc, w_ada, b_ada):
    depth, d, n6 = w_ada.shape
    tn = _pick(n6, 1024)
    return pl.pallas_call(
        _ada_kernel,
        grid=(depth, n6 // tn),
        in_specs=[pl.BlockSpec((MOD_ROWS, d), lambda l, j: (0, 0)),
                  pl.BlockSpec((1, d, tn), lambda l, j: (l, 0, j)),
                  pl.BlockSpec((1, 1, tn), lambda l, j: (l, 0, j))],
        out_specs=pl.BlockSpec((1, MOD_ROWS, tn), lambda l, j: (l, 0, j)),
        out_shape=jax.ShapeDtypeStruct((depth, MOD_ROWS, n6), F32),
        compiler_params=_cparams(2), name="ada_tables",
    )(cc, w_ada, b_ada.reshape(depth, 1, n6))


def _mod_spec(which, tm, n_lat, batch, d):
    return pl.BlockSpec((1, 1, 1, d),
                        lambda i, *_: (jnp.minimum((i * tm) // n_lat, batch), which, 0, 0))


def _normmod(x, g, shift, scale):
    y = x * lax.rsqrt(jnp.mean(x * x, axis=-1, keepdims=True) + EPS) * g
    return y * (1.0 + scale) + shift


def _normmod_kernel(x_ref, g_ref, sh_ref, sc_ref, o_ref):
    o_ref[...] = _normmod(x_ref[...], g_ref[...], sh_ref[0, 0], sc_ref[0, 0]).astype(o_ref.dtype)


def normmod(x, g, mods, which_shift, n_lat, batch, tm):
    t, d = x.shape
    return pl.pallas_call(
        _normmod_kernel,
        grid=(t // tm,),
        in_specs=[pl.BlockSpec((tm, d), lambda i: (i, 0)),
                  pl.BlockSpec((1, d), lambda i: (0, 0)),
                  _mod_spec(which_shift, tm, n_lat, batch, d),
                  _mod_spec(which_shift + 1, tm, n_lat, batch, d)],
        out_specs=pl.BlockSpec((tm, d), lambda i: (i, 0)),
        out_shape=jax.ShapeDtypeStruct((t, d), BF16),
        compiler_params=_cparams(1), name="normmod",
    )(x, g.reshape(1, d), mods, mods)


def _final_norm_kernel(x_ref, g_ref, o_ref):
    x = x_ref[...]
    o_ref[...] = x * lax.rsqrt(jnp.mean(x * x, axis=-1, keepdims=True) + EPS) * g_ref[...]


def final_norm(x, g, rows, tm):
    d = x.shape[1]
    return pl.pallas_call(
        _final_norm_kernel,
        grid=(rows // tm,),
        in_specs=[pl.BlockSpec((tm, d), lambda i: (i, 0)), pl.BlockSpec((1, d), lambda i: (0, 0))],
        out_specs=pl.BlockSpec((tm, d), lambda i: (i, 0)),
        out_shape=jax.ShapeDtypeStruct((rows, d), F32),
        compiler_params=_cparams(1), name="final_norm",
    )(x, g.reshape(1, d))


def _proj_kernel(a_ref, w_ref, o_ref, wb_ref):
    @pl.when(pl.program_id(1) == 0)
    def _():
        wb_ref[...] = w_ref[0].astype(BF16)

    o_ref[...] = jnp.dot(a_ref[...], wb_ref[...], preferred_element_type=F32).astype(o_ref.dtype)


def in_proj(a, w, layer, tm, col0, n, out_dtype):
    t, k = a.shape
    tn = _pick(math.gcd(n, col0) if col0 else n, 1024)
    return pl.pallas_call(
        _proj_kernel,
        grid=(n // tn, t // tm),
        in_specs=[pl.BlockSpec((tm, k), lambda j, i: (i, 0)),
                  pl.BlockSpec((1, k, tn), lambda j, i: (layer, 0, col0 // tn + j))],
        out_specs=pl.BlockSpec((tm, tn), lambda j, i: (i, j)),
        out_shape=jax.ShapeDtypeStruct((t, n), out_dtype),
        scratch_shapes=[pltpu.VMEM((k, tn), BF16)],
        compiler_params=_cparams(2), name="in_proj",
    )(a, w)


def _rope(x, tab):
    return x * tab[:, :HEAD_DIM] + pltpu.roll(x, HEAD_DIM // 2, axis=1) * tab[:, HEAD_DIM:]


def _sink_column(sink_ref, h, rows):
    r = lax.broadcasted_iota(jnp.int32, (Q_PER_KV * rows, 1), 0)
    col = jnp.full((Q_PER_KV * rows, 1), sink_ref[h * Q_PER_KV], F32)
    for g in range(1, Q_PER_KV):
        col = jnp.where(r >= g * rows, sink_ref[h * Q_PER_KV + g], col)
    return col


def _nt(a, b):
    return lax.dot_general(a, b, (((1,), (1,)), ((), ())), preferred_element_type=F32)


def _band_bias(n_lat):
    blk = ATTN_BLOCK
    single = n_lat == blk
    qi = (jnp.arange(Q_PER_KV * blk) % blk)[:, None]
    kj = jnp.arange(3 * blk)[None, :]
    band = jnp.abs(qi + blk - kj) <= WINDOW
    kinds = []
    for no_prev, no_next in ((True, single), (False, False), (single, True)):
        ok = band & ((kj >= blk) | (not no_prev)) & ((kj < 2 * blk) | (not no_next))
        kinds.append(jnp.where(ok, 0.0, -jnp.inf))
    return jnp.stack(kinds).astype(F32)


def _lat_attn_kernel(sink_ref, q_ref, kvm_ref, kv0_ref, kvp_ref, kvc_ref, tm_ref, t0_ref, tp_ref, bias_ref,
                     o_ref):
    blk = ATTN_BLOCK
    kv_w = N_KV_HEADS * HEAD_DIM
    scale = HEAD_DIM ** -0.5
    tabs = (tm_ref[...], t0_ref[...], tp_ref[...])
    kvs = (kvm_ref, kv0_ref, kvp_ref)
    for h in range(N_KV_HEADS):
        ks = slice(h * HEAD_DIM, (h + 1) * HEAD_DIM)
        vs = slice(kv_w + h * HEAD_DIM, kv_w + (h + 1) * HEAD_DIM)
        k_loc = jnp.concatenate([_rope(kvs[n][:, ks].astype(F32), tabs[n]) for n in range(3)],
                                axis=0).astype(BF16)
        v_loc = jnp.concatenate([kvs[n][:, vs] for n in range(3)], axis=0).astype(BF16)
        k_ctx = kvc_ref[:, ks].astype(BF16)
        v_ctx = kvc_ref[:, vs].astype(BF16)
        q4 = jnp.concatenate(
            [_rope(q_ref[:, (h * Q_PER_KV + g) * HEAD_DIM:(h * Q_PER_KV + g + 1) * HEAD_DIM].astype(F32), tabs[1])
             for g in range(Q_PER_KV)], axis=0).astype(BF16)
        s_loc = _nt(q4, k_loc) * scale + bias_ref[0]
        s_ctx = _nt(q4, k_ctx) * scale
        sink = _sink_column(sink_ref, h, blk)
        m = jnp.maximum(jnp.maximum(jnp.max(s_loc, axis=-1, keepdims=True),
                                    jnp.max(s_ctx, axis=-1, keepdims=True)), sink)
        p_loc = jnp.exp(s_loc - m)
        p_ctx = jnp.exp(s_ctx - m)
        den = (jnp.sum(p_loc, axis=-1, keepdims=True) + jnp.sum(p_ctx, axis=-1, keepdims=True)
               + jnp.exp(sink - m))
        o = (jnp.dot(p_loc.astype(BF16), v_loc, preferred_element_type=F32)
             + jnp.dot(p_ctx.astype(BF16), v_ctx, preferred_element_type=F32)) / den
        for g in range(Q_PER_KV):
            hq = h * Q_PER_KV + g
            o_ref[:, hq * HEAD_DIM:(hq + 1) * HEAD_DIM] = o[g * blk:(g + 1) * blk].astype(o_ref.dtype)


def latent_attention(p, sink, rope_tab, band_bias, batch, n_lat, n_ctx, q_off):
    blk = ATTN_BLOCK
    nb = n_lat // blk
    aw = N_Q_HEADS * HEAD_DIM
    kvw = 2 * N_KV_HEADS * HEAD_DIM
    ctx_blk0 = (batch * n_lat) // n_ctx
    kv_spec = lambda off: pl.BlockSpec(
        (blk, kvw), lambda b, i: (b * nb + jnp.clip(i + off, 0, nb - 1), 0))
    tab_spec = lambda off: pl.BlockSpec(
        (blk, 2 * HEAD_DIM), lambda b, i: (jnp.clip(i + off, 0, nb - 1), 0))
    bias_kind = lambda b, i: (jnp.where(i == 0, 0, jnp.where(i == nb - 1, 2, 1)), 0, 0)
    return pl.pallas_call(
        _lat_attn_kernel,
        grid=(batch, nb),
        in_specs=[pl.BlockSpec(memory_space=pltpu.SMEM),
                  pl.BlockSpec((blk, aw), lambda b, i: (b * nb + i, q_off // aw)),
                  kv_spec(-1), kv_spec(0), kv_spec(1),
                  pl.BlockSpec((n_ctx, kvw), lambda b, i: (ctx_blk0 + b, 0)),
                  tab_spec(-1), tab_spec(0), tab_spec(1),
                  pl.BlockSpec((1, Q_PER_KV * blk, 3 * blk), bias_kind)],
        out_specs=pl.BlockSpec((blk, aw), lambda b, i: (b * nb + i, 0)),
        out_shape=jax.ShapeDtypeStruct((p.shape[0], aw), BF16),
        compiler_params=_cparams(2), name="latent_attention",
    )(sink, p, p, p, p, p, rope_tab, rope_tab, rope_tab, band_bias)


def _ctx_attn_kernel(sink_ref, q_ref, kv_ref, o_lat_ref, o_ref):
    del o_lat_ref
    rows = q_ref.shape[0]
    kv_w = N_KV_HEADS * HEAD_DIM
    scale = HEAD_DIM ** -0.5
    for h in range(N_KV_HEADS):
        k = kv_ref[:, h * HEAD_DIM:(h + 1) * HEAD_DIM].astype(BF16)
        v = kv_ref[:, kv_w + h * HEAD_DIM:kv_w + (h + 1) * HEAD_DIM].astype(BF16)
        q4 = jnp.concatenate(
            [q_ref[:, (h * Q_PER_KV + g) * HEAD_DIM:(h * Q_PER_KV + g + 1) * HEAD_DIM]
             for g in range(Q_PER_KV)], axis=0).astype(BF16)
        s = _nt(q4, k) * scale
        sink = _sink_column(sink_ref, h, rows)
        m = jnp.maximum(jnp.max(s, axis=-1, keepdims=True), sink)
        p = jnp.exp(s - m)
        den = jnp.sum(p, axis=-1, keepdims=True) + jnp.exp(sink - m)
        o = jnp.dot(p.astype(BF16), v, preferred_element_type=F32) / den
        for g in range(Q_PER_KV):
            hq = h * Q_PER_KV + g
            o_ref[:, hq * HEAD_DIM:(hq + 1) * HEAD_DIM] = o[g * rows:(g + 1) * rows].astype(o_ref.dtype)


def context_attention(p, sink, o_lat, batch, n_lat, n_ctx, q_off):
    aw = N_Q_HEADS * HEAD_DIM
    kvw = 2 * N_KV_HEADS * HEAD_DIM
    ctx_blk0 = (batch * n_lat) // n_ctx
    return pl.pallas_call(
        _ctx_attn_kernel,
        grid=(batch,),
        in_specs=[pl.BlockSpec(memory_space=pltpu.SMEM),
                  pl.BlockSpec((n_ctx, aw), lambda b: (ctx_blk0 + b, q_off // aw)),
                  pl.BlockSpec((n_ctx, kvw), lambda b: (ctx_blk0 + b, 0)),
                  pl.BlockSpec(memory_space=pl.ANY)],
        out_specs=pl.BlockSpec((n_ctx, aw), lambda b: (ctx_blk0 + b, 0)),
        out_shape=jax.ShapeDtypeStruct(o_lat.shape, o_lat.dtype),
        input_output_aliases={3: 0},
        compiler_params=_cparams(1), name="context_attention",
    )(sink, p, p, o_lat)


def _dwconv_kernel(u_ref, w_ref, b_ref, o_ref, *, width):
    x = u_ref[...].astype(F32)
    length = x.shape[0]
    left = width // 2
    t = lax.broadcasted_iota(jnp.int32, (length, 1), 0)
    acc = jnp.broadcast_to(b_ref[...], x.shape)
    for j in range(width):
        s = j - left
        if s == 0:
            xs = x
        else:
            xs = pltpu.roll(x, (-s) % length, axis=0)
            xs = jnp.where((t + s >= 0) & (t + s < length), xs, 0.0)
        acc = acc + xs * w_ref[j:j + 1, :]
    o_ref[...] = acc


def dwconv(p, w, b, n_seq, length, row0, col0, n_ch):
    width = w.shape[0]
    cb = _pick(n_ch, 256)
    return pl.pallas_call(
        functools.partial(_dwconv_kernel, width=width),
        grid=(n_seq, n_ch // cb),
        in_specs=[pl.BlockSpec((length, cb), lambda s, c: (row0 // length + s, col0 // cb + c)),
                  pl.BlockSpec((width, cb), lambda s, c: (0, c)),
                  pl.BlockSpec((1, cb), lambda s, c: (0, c))],
        out_specs=pl.BlockSpec((length, cb), lambda s, c: (s, c)),
        out_shape=jax.ShapeDtypeStruct((n_seq * length, n_ch), F32),
        compiler_params=_cparams(2), name="dwconv",
    )(p, w, b.reshape(1, n_ch))


def _lru_kernel(xf_ref, xb_ref, w_ref, bias_ref, sp_ref, h0_ref, *rest):
    hf_ref, hb_ref, hl_ref, af, bf, ab, bb, carry = rest[-8:]
    j = pl.program_id(1)
    width = xf_ref.shape[1]

    @pl.when(j == 0)
    def _():
        carry[...] = h0_ref[0]

    def gates(x, d):
        g = jnp.dot(x.astype(BF16), w_ref[0, d], preferred_element_type=F32) + bias_ref[d]
        r = _sigmoid(g[:, :width])
        gi = _sigmoid(g[:, width:])
        a = jnp.exp(-LRU_C * r * sp_ref[d])
        return a, jnp.sqrt(1.0 - a * a) * (gi * x)

    af[...], bf[...] = gates(xf_ref[...], 0)
    ab[...], bb[...] = gates(xb_ref[...], 1)
    tl = af.shape[0]

    def body(t, hs):
        hf, hb = hs
        hf = af[pl.ds(t, 1), :] * hf + bf[pl.ds(t, 1), :]
        hf_ref[pl.ds(t, 1), :] = hf
        tb = tl - 1 - t
        hb = ab[pl.ds(tb, 1), :] * hb + bb[pl.ds(tb, 1), :]
        hb_ref[pl.ds(tb, 1), :] = hb
        return hf, hb

    hf, hb = lax.fori_loop(0, tl, body, (carry[0:1, :], carry[1:2, :]), unroll=8)
    carry[0:1, :] = hf
    carry[1:2, :] = hb
    hl_ref[0] = carry[...]


def rg_lru(xc, w_gates, layer, b_gates, softplus_neg_lam, h0, n_seq, length, total_rows, row0, prev=None):
    width = xc.shape[1]
    tl = _pick(length, 512)
    nc = length // tl
    blk0 = row0 // tl
    in_specs = [pl.BlockSpec((tl, width), lambda s, j: (s * nc + j, 0)),
                pl.BlockSpec((tl, width), lambda s, j: (s * nc + nc - 1 - j, 0)),
                pl.BlockSpec((1, 2, width, 2 * width), lambda s, j: (layer, 0, 0, 0)),
                pl.BlockSpec((2, 1, 2 * width), lambda s, j: (0, 0, 0)),
                pl.BlockSpec((2, 1, width), lambda s, j: (0, 0, 0)),
                pl.BlockSpec((1, 2, width), lambda s, j: (s, 0, 0))]
    args = [xc, xc, w_gates, b_gates, softplus_neg_lam, h0]
    aliases = {}
    if prev is not None:
        in_specs += [pl.BlockSpec(memory_space=pl.ANY)] * 2
        aliases = {len(args): 0, len(args) + 1: 1}
        args += list(prev)
    return pl.pallas_call(
        _lru_kernel,
        grid=(n_seq, nc),
        in_specs=in_specs,
        out_specs=[pl.BlockSpec((tl, width), lambda s, j: (blk0 + s * nc + j, 0)),
                   pl.BlockSpec((tl, width), lambda s, j: (blk0 + s * nc + nc - 1 - j, 0)),
                   pl.BlockSpec((1, 2, width), lambda s, j: (s, 0, 0))],
        out_shape=[jax.ShapeDtypeStruct((total_rows, width), F32),
                   jax.ShapeDtypeStruct((total_rows, width), F32),
                   jax.ShapeDtypeStruct((n_seq, 2, width), F32)],
        scratch_shapes=[pltpu.VMEM((tl, width), F32)] * 4 + [pltpu.VMEM((2, width), F32)],
        input_output_aliases=aliases,
        compiler_params=_cparams(2), name="rg_lru",
    )(*args)


def _softplus_kernel(x_ref, o_ref):
    x = -x_ref[...]
    o_ref[...] = jnp.maximum(x, 0.0) + jnp.log(1.0 + jnp.exp(-jnp.abs(x)))


def softplus_neg(lam):
    depth, two, width = lam.shape
    x = lam.reshape(depth * two, width)
    out = pl.pallas_call(
        _softplus_kernel,
        out_shape=jax.ShapeDtypeStruct(x.shape, F32), name="softplus_neg",
    )(x)
    return out.reshape(depth, two, 1, width)


def _hy_hidden_kernel(z_ref, w1_ref, b1_ref, fr_ref, w2_ref, b2_ref, o_ref):
    fr = fr_ref[...]
    h = jnp.sin(fr * (jnp.dot(z_ref[...], w1_ref[...], precision=HIGHEST,
                              preferred_element_type=F32) + b1_ref[...]))
    o_ref[...] = jnp.sin(fr * (jnp.dot(h, w2_ref[...], precision=HIGHEST,
                                       preferred_element_type=F32) + b2_ref[...]))


def hyena_hidden(z, w1, b1, freq, w2, b2):
    length = z.shape[0]
    fh = w2.shape[0]
    return pl.pallas_call(
        _hy_hidden_kernel,
        out_shape=jax.ShapeDtypeStruct((length, fh), F32), name="hyena_hidden",
    )(z, w1, b1.reshape(1, fh), freq.reshape(1, fh), w2, b2.reshape(1, fh))


def _hy_taps_kernel(hid_ref, wf_ref, wb_ref, delta_ref, u_ref, kn_ref):
    hid = hid_ref[...]
    length = hid.shape[0]
    hf = jnp.dot(hid, wf_ref[...], precision=HIGHEST, preferred_element_type=F32)
    hb = jnp.dot(hid, wb_ref[...], precision=HIGHEST, preferred_element_type=F32)
    ti = lax.broadcasted_iota(jnp.int32, (length, 1), 0)
    win = jnp.exp(-(ti.astype(F32) * (1.0 / (length - 1))) * delta_ref[...])
    f = hf * win
    b = jnp.where(ti >= 1, hb * win, 0.0)
    sc = lax.rsqrt(jnp.sum(f * f + b * b, axis=0, keepdims=True) + EPS)
    u1 = (f + b) * sc
    u2 = (f - b) * sc
    sign = jnp.where((ti & 1) == 0, 1.0, -1.0)
    kn_ref[...] = jnp.sum(u1 * sign, axis=0, keepdims=True)
    u_ref[0] = u1.astype(u_ref.dtype)
    u_ref[1] = u2.astype(u_ref.dtype)


def hyena_taps(hid, w3, deltas):
    length, fh = hid.shape
    hw = deltas.shape[1]
    n_ord = w3.shape[1] // (2 * hw)
    cw = _pick(hw, 128)
    nc = hw // cw
    return pl.pallas_call(
        _hy_taps_kernel,
        grid=(n_ord, nc),
        in_specs=[pl.BlockSpec((length, fh), lambda o, c: (0, 0)),
                  pl.BlockSpec((fh, cw), lambda o, c: (0, o * 2 * nc + c)),
                  pl.BlockSpec((fh, cw), lambda o, c: (0, o * 2 * nc + nc + c)),
                  pl.BlockSpec((1, cw), lambda o, c: (0, c))],
        out_specs=[pl.BlockSpec((2, length, cw), lambda o, c: (0, 0, o * nc + c)),
                   pl.BlockSpec((1, cw), lambda o, c: (0, o * nc + c))],
        out_shape=[jax.ShapeDtypeStruct((2, length, n_ord * hw), BF16),
                   jax.ShapeDtypeStruct((1, n_ord * hw), F32)],
        compiler_params=_cparams(2), name="hyena_taps",
    )(hid, w3, w3, deltas)


def _spectrum_kernel(fw_ref, u_ref, o_ref):
    o_ref[0] = jnp.dot(fw_ref[0], u_ref[0], preferred_element_type=F32)


def hyena_spectrum(fw, u):
    _, length, n = u.shape
    tm = _pick(length, 512)
    return pl.pallas_call(
        _spectrum_kernel,
        grid=(2, length // tm),
        in_specs=[pl.BlockSpec((1, tm, length), lambda h, i: (h, i, 0)),
                  pl.BlockSpec((1, length, n), lambda h, i: (h, 0, 0))],
        out_specs=pl.BlockSpec((1, tm, n), lambda h, i: (h, i, 0)),
        out_shape=jax.ShapeDtypeStruct((2, length, n), F32),
        compiler_params=_cparams(2), name="hyena_spectrum",
    )(fw, u)


def _conv_fwd_kernel(fw_ref, z_ref, k_ref, kn_ref, y_ref, zb_ref):
    i = pl.program_id(1)

    @pl.when(i == 0)
    def _():
        zb_ref[...] = z_ref[...].astype(BF16)

    z = zb_ref[...]
    zr = jnp.dot(fw_ref[0], z, preferred_element_type=F32)
    zi = jnp.dot(fw_ref[1], z, preferred_element_type=F32)
    kr = k_ref[0]
    ki = k_ref[1]
    yr = zr * kr - zi * ki
    yi = zr * ki + zi * kr
    first = (lax.broadcasted_iota(jnp.int32, (zr.shape[0], 1), 0) == 0) & (i == 0)
    y_ref[0, 0] = jnp.where(first, 0.5 * zr * kr, yr).astype(y_ref.dtype)
    y_ref[0, 1] = jnp.where(first, 0.5 * zi * kn_ref[...], yi).astype(y_ref.dtype)


def conv_forward(fw, z, col0, kspec, kn, order, n_seq, length, hw):
    tm = _pick(length, 512)
    return pl.pallas_call(
        _conv_fwd_kernel,
        grid=(n_seq, length // tm),
        in_specs=[pl.BlockSpec((2, tm, length), lambda s, i: (0, i, 0)),
                  pl.BlockSpec((length, hw), lambda s, i: (s, col0 // hw)),
                  pl.BlockSpec((2, tm, hw), lambda s, i: (0, i, order)),
                  pl.BlockSpec((1, hw), lambda s, i: (0, order))],
        out_specs=pl.BlockSpec((1, 2, tm, hw), lambda s, i: (s, 0, i, 0)),
        out_shape=jax.ShapeDtypeStruct((n_seq, 2, length, hw), BF16),
        scratch_shapes=[pltpu.VMEM((length, hw), BF16)],
        compiler_params=_cparams(2), name="hyena_conv_fwd",
    )(fw, z, kspec, kn)


def _conv_inv_kernel(g_ref, y_ref, z_ref, gate_ref, skip_ref, *rest, inv_len):
    o_ref = rest[-1]
    y = jnp.dot(g_ref[...], y_ref[0], preferred_element_type=F32) * inv_len
    o_ref[...] = (gate_ref[...] * (y + skip_ref[0] * z_ref[...])).astype(o_ref.dtype)


def conv_inverse(g, yp, z, z_col0, gates, gate_col0, skip, order, n_seq, length, out_dtype,
                 total_rows=None, row0=0, prev=None):
    hw = yp.shape[3]
    tm = _pick(length, 512)
    nt = length // tm
    blk0 = row0 // tm
    total_rows = n_seq * length if total_rows is None else total_rows
    in_specs = [pl.BlockSpec((tm, 2 * length), lambda s, i: (i, 0)),
                pl.BlockSpec((1, 2 * length, hw), lambda s, i: (s, 0, 0)),
                pl.BlockSpec((tm, hw), lambda s, i: (s * nt + i, z_col0 // hw)),
                pl.BlockSpec((tm, hw), lambda s, i: (s * nt + i, gate_col0 // hw)),
                pl.BlockSpec((1, 1, hw), lambda s, i: (order, 0, 0))]
    args = [g, yp.reshape(n_seq, 2 * length, hw), z, gates, skip.reshape(skip.shape[0], 1, hw)]
    aliases = {}
    if prev is not None:
        in_specs.append(pl.BlockSpec(memory_space=pl.ANY))
        aliases = {len(args): 0}
        args.append(prev)
    return pl.pallas_call(
        functools.partial(_conv_inv_kernel, inv_len=1.0 / length),
        grid=(n_seq, nt),
        in_specs=in_specs,
        out_specs=pl.BlockSpec((tm, hw), lambda s, i: (blk0 + s * nt + i, 0)),
        out_shape=jax.ShapeDtypeStruct((total_rows, hw), out_dtype),
        input_output_aliases=aliases,
        compiler_params=_cparams(2), name="hyena_conv_inv",
    )(*args)


def _merge_kernel(oa_ref, oh_ref, y_ref, hf_ref, hb_ref, g_ref, wa_ref, wh_ref, wl_ref, o_ref, *, tn):
    d = o_ref.shape[1]
    ol = (jax.nn.gelu(y_ref[...].astype(F32), approximate=True) * (hf_ref[...] + hb_ref[...])).astype(BF16)
    oa = oa_ref[...]
    oh = oh_ref[...]
    for c0 in range(0, d, tn):
        cols = slice(c0, c0 + tn)
        gate = lambda k: _sigmoid(g_ref[:, k * d + c0:k * d + c0 + tn].astype(F32))
        m = (gate(0) * jnp.dot(oa, wa_ref[0, :, cols], preferred_element_type=F32)
             + gate(1) * jnp.dot(oh, wh_ref[0, :, cols], preferred_element_type=F32)
             + gate(2) * jnp.dot(ol, wl_ref[0, :, cols], preferred_element_type=F32))
        o_ref[:, cols] = m.astype(o_ref.dtype)


def merge_branches(o_att, o_hy, p, gates, hf, hb, wa, wh, wl, layer, lru_y_off, tm):
    t, aw = o_att.shape
    hw = o_hy.shape[1]
    lw = hf.shape[1]
    d = wa.shape[2]
    w_spec = lambda rows: pl.BlockSpec((1, rows, d), lambda i: (layer, 0, 0))
    return pl.pallas_call(
        functools.partial(_merge_kernel, tn=_pick(d, 512)),
        grid=(t // tm,),
        in_specs=[pl.BlockSpec((tm, aw), lambda i: (i, 0)),
                  pl.BlockSpec((tm, hw), lambda i: (i, 0)),
                  pl.BlockSpec((tm, lw), lambda i: (i, lru_y_off // lw)),
                  pl.BlockSpec((tm, lw), lambda i: (i, 0)),
                  pl.BlockSpec((tm, lw), lambda i: (i, 0)),
                  pl.BlockSpec((tm, 3 * d), lambda i: (i, 0)),
                  w_spec(aw), w_spec(hw), w_spec(lw)],
        out_specs=pl.BlockSpec((tm, d), lambda i: (i, 0)),
        out_shape=jax.ShapeDtypeStruct((t, d), BF16),
        compiler_params=_cparams(1), name="merge_branches",
    )(o_att, o_hy, p, hf, hb, gates, wa, wh, wl)


def _out_proj_kernel(m_ref, w_ref, x_ref, gate_ref, o_ref):
    o_ref[...] = x_ref[...] + gate_ref[0, 0] * jnp.dot(m_ref[...], w_ref[0],
                                                       preferred_element_type=F32)


def out_proj_residual(m, w, layer, x, mods, which_gate, n_lat, batch, tm):
    t, d = x.shape
    return pl.pallas_call(
        _out_proj_kernel,
        grid=(t // tm,),
        in_specs=[pl.BlockSpec((tm, d), lambda i: (i, 0)),
                  pl.BlockSpec((1, d, d), lambda i: (layer, 0, 0)),
                  pl.BlockSpec((tm, d), lambda i: (i, 0)),
                  _mod_spec(which_gate, tm, n_lat, batch, d)],
        out_specs=pl.BlockSpec((tm, d), lambda i: (i, 0)),
        out_shape=jax.ShapeDtypeStruct((t, d), F32),
        compiler_params=_cparams(1), name="out_proj_residual",
    )(m, w, x, mods)


def _router_kernel(x_ref, g_ref, sh_ref, sc_ref, rw_ref, rb_ref, h_ref, meta_ref, cnt_ref):
    h = _normmod(x_ref[...], g_ref[...], sh_ref[0, 0], sc_ref[0, 0])
    _store_tile_rows(h_ref, 0, _pack_bf16_pairs(h))
    w = rw_ref[...]
    h_hi, w_hi = h.astype(BF16), w.astype(BF16)
    h_lo = (h - h_hi.astype(F32)).astype(BF16)
    w_lo = (w - w_hi.astype(F32)).astype(BF16)
    logits = (jnp.dot(h_hi, w_hi, preferred_element_type=F32) + jnp.dot(h_lo, w_hi, preferred_element_type=F32)
              + jnp.dot(h_hi, w_lo, preferred_element_type=F32)) + rb_ref[...]
    lane = lax.broadcasted_iota(jnp.int32, logits.shape, 1)
    valid = lane < N_EXPERTS
    logits = jnp.where(valid, logits, -jnp.inf)
    ex = jnp.exp(logits - jnp.max(logits, axis=-1, keepdims=True))
    scores = ex / jnp.sum(ex, axis=-1, keepdims=True)
    grp = lax.shift_right_logical(lane, int(math.log2(EXPERTS_PER_GROUP)))
    best = jnp.zeros((logits.shape[0], 1), jnp.int32)
    best_max = jnp.max(jnp.where((grp == 0) & valid, scores, -1.0), axis=-1, keepdims=True)
    for g in range(1, N_GROUPS):
        gm = jnp.max(jnp.where((grp == g) & valid, scores, -1.0), axis=-1, keepdims=True)
        better = gm > best_max
        best = jnp.where(better, g, best)
        best_max = jnp.where(better, gm, best_max)
    s1 = jnp.where((grp == best) & valid, scores, -1.0)
    m1 = jnp.max(s1, axis=-1, keepdims=True)
    i1 = jnp.min(jnp.where(s1 == m1, lane, LANES_V7X), axis=-1, keepdims=True)
    s2 = jnp.where(lane == i1, -1.0, s1)
    m2 = jnp.max(s2, axis=-1, keepdims=True)
    i2 = jnp.min(jnp.where(s2 == m2, lane, LANES_V7X), axis=-1, keepdims=True)
    tot = m1 + m2
    @pl.when(pl.program_id(0) == 0)
    def _():
        cnt_ref[...] = jnp.zeros_like(cnt_ref)

    tm = logits.shape[0]
    onehot = ((lane == i1) | (lane == i2)).astype(BF16)
    lower = (lax.broadcasted_iota(jnp.int32, (tm, tm), 0)
             > lax.broadcasted_iota(jnp.int32, (tm, tm), 1)).astype(BF16)
    before = jnp.dot(lower, onehot, preferred_element_type=F32) + cnt_ref[...]
    r1 = jnp.sum(jnp.where(lane == i1, before, 0.0), axis=-1, keepdims=True)
    r2 = jnp.sum(jnp.where(lane == i2, before, 0.0), axis=-1, keepdims=True)
    cnt_ref[...] += jnp.sum(onehot.astype(F32), axis=0, keepdims=True)
    cols = (i1.astype(F32), i2.astype(F32), r1, r2, m1 / tot, m2 / tot)
    meta = jnp.zeros(logits.shape, F32)
    for k, col in enumerate(cols):
        meta = jnp.where(lane == k, col, meta)
    meta_ref[...] = meta


ROUTE_E, ROUTE_RANK, ROUTE_W = 0, 2, 4


def router(x, g, mods, which_shift, rw, rb, n_lat, batch, tm):
    t, d = x.shape
    spr = d // 2 // LANES_V7X
    return pl.pallas_call(
        _router_kernel,
        grid=(t // tm,),
        in_specs=[pl.BlockSpec((tm, d), lambda i: (i, 0)),
                  pl.BlockSpec((1, d), lambda i: (0, 0)),
                  _mod_spec(which_shift, tm, n_lat, batch, d),
                  _mod_spec(which_shift + 1, tm, n_lat, batch, d),
                  pl.BlockSpec((d, LANES_V7X), lambda i: (0, 0)),
                  pl.BlockSpec((1, LANES_V7X), lambda i: (0, 0))],
        out_specs=[pl.BlockSpec((tm * spr, LANES_V7X), lambda i: (i, 0)),
                   pl.BlockSpec((tm, LANES_V7X), lambda i: (i, 0)),
                   pl.BlockSpec((1, LANES_V7X), lambda i: (0, 0))],
        out_shape=[jax.ShapeDtypeStruct((t * spr, LANES_V7X), jnp.uint32),
                   jax.ShapeDtypeStruct((t, LANES_V7X), F32),
                   jax.ShapeDtypeStruct((1, LANES_V7X), F32)],
        compiler_params=_cparams(1), name="router",
    )(x, g.reshape(1, d), mods, mods, rw, rb)


def dispatch_plan(meta, counts, n_exp, tile, n_tiles):
    cnt = counts[0, :n_exp].astype(jnp.int32)
    tiles_e = (cnt + tile - 1) // tile
    tile_end = jnp.cumsum(tiles_e)
    row_start = (tile_end - tiles_e) * tile
    experts = meta[:, ROUTE_E:ROUTE_E + 2].astype(jnp.int32)
    ranks = meta[:, ROUTE_RANK:ROUTE_RANK + 2].astype(jnp.int32)
    dest = (row_start[experts] + ranks).reshape(-1)
    tile_expert = jnp.minimum(jnp.sum(jnp.arange(n_tiles)[:, None] >= tile_end[None, :], axis=1),
                              n_exp - 1).astype(jnp.int32)
    zero_start = jnp.minimum(row_start + cnt, (n_tiles - 1) * tile).astype(jnp.int32)
    return dest, tile_expert, tile_end[-1:].astype(jnp.int32), zero_start


def _dispatch_kernel(dest_ref, zero_ref, h_ref, xs_ref, zbuf, sem, zsem, *, spr):
    tm = h_ref.shape[0] // spr
    base = pl.program_id(0) * (2 * tm)
    token = lambda ref, t: ref.at[pl.ds(pl.multiple_of(t * spr, spr), spr)]

    @pl.when(pl.program_id(0) == 0)
    def _():
        zbuf[...] = jnp.zeros_like(zbuf)

        def clear(e, carry):
            start = pl.multiple_of(zero_ref[e] * spr, spr)
            copy = pltpu.make_async_copy(zbuf, xs_ref.at[pl.ds(start, zbuf.shape[0])], zsem)
            copy.start()
            copy.wait()
            return carry

        lax.fori_loop(0, zero_ref.shape[0], clear, 0)

    def issue(r, carry):
        for k in range(2):
            pltpu.make_async_copy(token(h_ref, r), token(xs_ref, dest_ref[base + 2 * r + k]), sem).start()
        return carry

    lax.fori_loop(0, tm, issue, 0, unroll=8)
    for k in range(2):
        pltpu.make_async_copy(h_ref, xs_ref.at[pl.ds(0, tm * spr)], sem).wait()


def moe_dispatch(h, dest, zero_start, n_rows, tile, tm, spr):
    assert spr == SUBLANES_V7X
    t = h.shape[0] // spr
    return pl.pallas_call(
        functools.partial(_dispatch_kernel, spr=spr),
        grid_spec=pltpu.PrefetchScalarGridSpec(
            num_scalar_prefetch=2, grid=(t // tm,),
            in_specs=[pl.BlockSpec((tm * spr, LANES_V7X), lambda i, dest, zs: (i, 0))],
            out_specs=pl.BlockSpec(memory_space=pl.ANY),
            scratch_shapes=[pltpu.VMEM((tile * spr, LANES_V7X), h.dtype),
                            pltpu.SemaphoreType.DMA, pltpu.SemaphoreType.DMA]),
        out_shape=jax.ShapeDtypeStruct((n_rows * spr, LANES_V7X), h.dtype),
        compiler_params=_cparams(1), name="moe_dispatch",
    )(dest, zero_start, h)


def _experts_kernel(te_ref, nu_ref, xs_ref, wg_ref, wu_ref, wd_ref, ys_ref, wgb, wub, wdb, *, spr):
    j = pl.program_id(0)

    @pl.when(j < nu_ref[0])
    def _():
        @pl.when((j == 0) | (te_ref[j] != te_ref[jnp.maximum(j - 1, 0)]))
        def _():
            wgb[...] = wg_ref[0, 0].astype(BF16)
            wub[...] = wu_ref[0, 0].astype(BF16)
            wdb[...] = wd_ref[0, 0].astype(BF16)

        tile = xs_ref.shape[0] // spr
        x_hi, x_lo = (v.astype(BF16) for v in _unpack_bf16_pairs(_load_tile_rows(xs_ref, 0, tile, spr)))
        half = x_hi.shape[1]

        def proj(w):
            return (jnp.dot(x_hi, w[:half, :], preferred_element_type=F32)
                    + jnp.dot(x_lo, w[half:, :], preferred_element_type=F32))

        gt = proj(wgb)
        act = (gt * _sigmoid(gt)) * proj(wub)
        _store_tile_rows(ys_ref, 0, _pack_bf16_pairs(
            jnp.dot(act.astype(BF16), wdb[...], preferred_element_type=F32)))


def moe_experts(tile_expert, n_used, xs, wg, wu, wd, layer, tile):
    d, de = wg.shape[2], wg.shape[3]
    spr = d // 2 // LANES_V7X
    n_tiles = xs.shape[0] // (tile * spr)
    used = lambda j, nu: jnp.minimum(j, nu[0] - 1)
    w_spec = lambda shape: pl.BlockSpec((1, 1) + shape, lambda j, te, nu: (layer, te[used(j, nu)], 0, 0))
    return pl.pallas_call(
        functools.partial(_experts_kernel, spr=spr),
        grid_spec=pltpu.PrefetchScalarGridSpec(
            num_scalar_prefetch=2, grid=(n_tiles,),
            in_specs=[pl.BlockSpec((tile * spr, LANES_V7X), lambda j, te, nu: (used(j, nu), 0)),
                      w_spec((d, de)), w_spec((d, de)), w_spec((de, d))],
            out_specs=pl.BlockSpec((tile * spr, LANES_V7X), lambda j, te, nu: (used(j, nu), 0)),
            scratch_shapes=[pltpu.VMEM((d, de), BF16), pltpu.VMEM((d, de), BF16),
                            pltpu.VMEM((de, d), BF16)]),
        out_shape=jax.ShapeDtypeStruct(xs.shape, xs.dtype),
        compiler_params=_cparams(1), name="moe_experts",
    )(tile_expert, n_used, xs, wg, wu, wd)


def _combine_kernel(dest_ref, meta_ref, x_ref, gate_ref, ys_ref, o_ref, buf, sem, *, spr):
    i = pl.program_id(0)
    tm = x_ref.shape[0]
    part = lambda slot, k: pl.multiple_of((slot * 2 + k) * (tm * spr), spr)

    def gather(tile, slot):
        base = tile * (2 * tm)

        def issue(r, carry):
            for k in range(2):
                row = dest_ref[base + 2 * r + k]
                pltpu.make_async_copy(ys_ref.at[pl.ds(pl.multiple_of(row * spr, spr), spr)],
                                      buf.at[pl.ds(part(slot, k) + r * spr, spr)], sem.at[slot]).start()
            return carry

        lax.fori_loop(0, tm, issue, 0, unroll=8)

    @pl.when(i == 0)
    def _():
        gather(0, 0)

    @pl.when(i + 1 < pl.num_programs(0))
    def _():
        gather(i + 1, (i + 1) % 2)

    slot = i % 2
    for k in range(2):
        pltpu.make_async_copy(ys_ref.at[pl.ds(0, tm * spr)], buf.at[pl.ds(part(slot, k), tm * spr)],
                              sem.at[slot]).wait()
    meta = meta_ref[...]
    w1 = meta[:, ROUTE_W:ROUTE_W + 1]
    w2 = meta[:, ROUTE_W + 1:ROUTE_W + 2]
    y1_hi, y1_lo = _unpack_bf16_pairs(_load_tile_rows(buf, part(slot, 0), tm, spr))
    y2_hi, y2_lo = _unpack_bf16_pairs(_load_tile_rows(buf, part(slot, 1), tm, spr))
    half = y1_hi.shape[1]
    gate = gate_ref[0, 0]
    o_ref[:, :half] = x_ref[:, :half] + gate[:, :half] * (w1 * y1_hi + w2 * y2_hi)
    o_ref[:, half:] = x_ref[:, half:] + gate[:, half:] * (w1 * y1_lo + w2 * y2_lo)


def moe_combine(dest, meta, x, mods, which_gate, ys, n_lat, batch, tm):
    t, d = x.shape
    spr = d // 2 // LANES_V7X
    return pl.pallas_call(
        functools.partial(_combine_kernel, spr=spr),
        grid_spec=pltpu.PrefetchScalarGridSpec(
            num_scalar_prefetch=1, grid=(t // tm,),
            in_specs=[pl.BlockSpec((tm, LANES_V7X), lambda i, dest: (i, 0)),
                      pl.BlockSpec((tm, d), lambda i, dest: (i, 0)),
                      _mod_spec(which_gate, tm, n_lat, batch, d),
                      pl.BlockSpec(memory_space=pl.ANY)],
            out_specs=pl.BlockSpec((tm, d), lambda i, dest: (i, 0)),
            scratch_shapes=[pltpu.VMEM((2 * 2 * tm * spr, LANES_V7X), ys.dtype),
                            pltpu.SemaphoreType.DMA((2,))]),
        out_shape=jax.ShapeDtypeStruct((t, d), F32),
        compiler_params=_cparams(1), name="moe_combine",
    )(dest, meta, x, mods, ys)


def _rope_table(n_tokens):
    rows = n_tokens // GRID_W
    row = jnp.repeat(jnp.arange(rows), GRID_W).astype(F32)
    col = jnp.tile(jnp.arange(GRID_W), rows).astype(F32)
    pairs = HEAD_DIM // 4
    inv = ROPE_THETA ** (-jnp.arange(pairs, dtype=F32) / pairs)
    ang = jnp.concatenate([row[:, None] * inv, col[:, None] * inv], axis=-1)
    cos, sin = jnp.cos(ang), jnp.sin(ang)
    return jnp.concatenate([cos, cos, -sin, sin], axis=-1)


def _dft_kernel(ca_ref, sa_ref, cr_ref, sr_ref, fw_ref, g_ref):
    tm = cr_ref.shape[0]
    length = fw_ref.shape[2]
    cr = cr_ref[...]
    sr = sr_ref[...]
    row = pl.program_id(0) * tm + lax.broadcasted_iota(jnp.int32, (tm, 1), 0)
    alt_row = (1 - 2 * (row & 1)).astype(F32)
    for a in range(length // LANES_V7X):
        blk = slice(a * LANES_V7X, (a + 1) * LANES_V7X)
        ca = ca_ref[:, a:a + 1]
        sa = sa_ref[:, a:a + 1]
        cos_blk = ca * cr - sa * sr
        nsin_blk = -(sa * cr + ca * sr)
        col = a * LANES_V7X + lax.broadcasted_iota(jnp.int32, (1, LANES_V7X), 1)
        alt_col = (1 - 2 * (col & 1)).astype(F32)
        fw_ref[0, :, blk] = cos_blk.astype(fw_ref.dtype)
        fw_ref[1, :, blk] = jnp.where(row == 0, alt_col, nsin_blk).astype(fw_ref.dtype)
        g_ref[:, blk] = cos_blk.astype(g_ref.dtype)
        g_ref[:, length + a * LANES_V7X:length + (a + 1) * LANES_V7X] = (
            jnp.where(col == 0, alt_row, nsin_blk).astype(g_ref.dtype))


def _dft_tables(length):
    k = jnp.arange(length, dtype=jnp.int32)[:, None]
    n_hi = jnp.arange(length // LANES_V7X, dtype=jnp.int32)[None, :] * LANES_V7X
    n_lo = jnp.arange(LANES_V7X, dtype=jnp.int32)[None, :]
    ang = lambda n: ((k * n) % (2 * length)).astype(F32) * (math.pi / length)
    tm = _pick(length, 256)
    n_a = length // LANES_V7X
    return pl.pallas_call(
        _dft_kernel,
        grid=(length // tm,),
        in_specs=[pl.BlockSpec((tm, n_a), lambda i: (i, 0)), pl.BlockSpec((tm, n_a), lambda i: (i, 0)),
                  pl.BlockSpec((tm, LANES_V7X), lambda i: (i, 0)),
                  pl.BlockSpec((tm, LANES_V7X), lambda i: (i, 0))],
        out_specs=[pl.BlockSpec((2, tm, length), lambda i: (0, i, 0)),
                   pl.BlockSpec((tm, 2 * length), lambda i: (i, 0))],
        out_shape=[jax.ShapeDtypeStruct((2, length, length), BF16),
                   jax.ShapeDtypeStruct((length, 2 * length), BF16)],
        compiler_params=_cparams(1), name="dft_tables",
    )(jnp.cos(ang(n_hi)), jnp.sin(ang(n_hi)), jnp.cos(ang(n_lo)), jnp.sin(ang(n_lo)))


def _hyena_features(length, k_pad):
    t = jnp.linspace(0.0, 1.0, length, dtype=F32)[:, None]
    w = 2.0 * math.pi * jnp.arange(length, dtype=F32)[:, None] / length
    f = jnp.linspace(1e-4, HYENA_BANDS - 1, HYENA_BANDS, dtype=F32)[None, :]
    z = jnp.concatenate([t, jnp.cos(f * w), -jnp.sin(f * w)], axis=-1)
    return jnp.pad(z, ((0, 0), (0, k_pad - z.shape[1])))


def _blockdiag_dense(w):
    nb, bs = w.shape[-3], w.shape[-2]
    eye = jnp.eye(nb, dtype=w.dtype)
    dense = jnp.einsum('...nde,nm->...ndme', w, eye)
    return dense.reshape(w.shape[:-3] + (nb * bs, nb * bs))


def kernel(x, c, ctx, c_ctx, w_ada, b_ada, g_mix, g_ffn, w_in, attn_sink, hy_short_w, hy_short_b, hy_w1, hy_b1, hy_freq, hy_w2, hy_b2, hy_w3, hy_skip, lru_conv_w, lru_conv_b, lru_wa, lru_ba, lru_wx, lru_bx, lru_lambda, w_br_attn, w_br_hy, w_br_lru, w_out, router_w, router_b, exp_w_gate, exp_w_up, exp_w_down, final_g):
    batch, n_lat, d = x.shape
    n_ctx = ctx.shape[1]
    depth = w_ada.shape[0]
    t_lat, t_ctx = batch * n_lat, batch * n_ctx
    hw = hy_skip.shape[2]
    lw = lru_lambda.shape[2]
    fh = hy_w2.shape[1]
    aw = N_Q_HEADS * HEAD_DIM
    kvw = N_KV_HEADS * HEAD_DIM
    q_off = 2 * kvw + lw
    hy_off = q_off + aw
    lru_y_off = hy_off + 3 * hw
    gate_off = lru_y_off + lw
    assert batch + 1 <= MOD_ROWS and t_lat % n_ctx == 0 and n_lat % ATTN_BLOCK == 0
    tile_base = math.gcd(n_lat, t_ctx)
    tm = _pick(tile_base, 512)
    tm_big = _pick(tile_base, 1024)

    xu = jnp.concatenate([x.reshape(t_lat, d), ctx.reshape(t_ctx, d)], axis=0)
    cc = jnp.concatenate([c, c_ctx[None, :], jnp.zeros((MOD_ROWS - batch - 1, d), F32)], axis=0)
    mods_all = ada_tables(cc, w_ada, b_ada).reshape(depth, MOD_ROWS, N_MOD, 1, d)

    rope_tab = _rope_table(n_lat)
    band_bias = _band_bias(n_lat)
    deltas = jnp.abs(jnp.linspace(math.log(HYENA_DECAY_TARGET) / HYENA_FAST_DECAY,
                                  math.log(HYENA_DECAY_TARGET) / HYENA_SLOW_DECAY, hw, dtype=F32))[None, :]
    seqs = []
    for length, row0 in ((n_lat, 0), (n_ctx, t_lat)):
        fw, g = _dft_tables(length)
        seqs.append((length, row0, fw, g, _hyena_features(length, fh)))
    hy_w1p = jnp.pad(hy_w1, ((0, 0), (0, fh - hy_w1.shape[1]), (0, 0)))

    sp = softplus_neg(lru_lambda)
    lru_w = jnp.concatenate([_blockdiag_dense(lru_wa), _blockdiag_dense(lru_wx)], axis=-1).astype(BF16)
    lru_b = jnp.concatenate([lru_ba, lru_bx], axis=-1)[:, :, None, :]
    wa_b, wh_b, wl_b, wo_b = (w.astype(BF16) for w in (w_br_attn, w_br_hy, w_br_lru, w_out))
    rw = jnp.pad(router_w, ((0, 0), (0, LANES_V7X - router_w.shape[1])))
    rb = jnp.pad(router_b, (0, LANES_V7X - router_b.shape[0]))[None, :]
    h_zero = jnp.zeros((batch, 2, lw), F32)
    t_all = t_lat + t_ctx
    n_exp = exp_w_gate.shape[1]
    exp_tile = _pick(2 * t_all, 512)
    n_exp_tiles = (2 * t_all) // exp_tile + n_exp
    tm_cmb = _pick(tile_base, 256)

    for l in range(depth):
        mods = mods_all[l]
        h = normmod(xu, g_mix[l], mods, 0, n_lat, batch, tm)
        p = in_proj(h, w_in, l, tm_big, 0, gate_off, BF16)
        gates = in_proj(h, w_in, l, tm_big, gate_off, w_in.shape[2] - gate_off, BF16)

        o_att = latent_attention(p, attn_sink[l], rope_tab, band_bias, batch, n_lat, n_ctx, q_off)
        o_att = context_attention(p, attn_sink[l], o_att, batch, n_lat, n_ctx, q_off)

        xc_c = dwconv(p, lru_conv_w[l], lru_conv_b[l], batch, n_ctx, t_lat, 2 * kvw, lw)
        hf, hb, h_end = rg_lru(xc_c, lru_w, l, lru_b[l], sp[l], h_zero, batch, n_ctx, t_all, t_lat)
        xc_l = dwconv(p, lru_conv_w[l], lru_conv_b[l], batch, n_lat, 0, 2 * kvw, lw)
        hf, hb, _ = rg_lru(xc_l, lru_w, l, lru_b[l], sp[l], h_end, batch, n_lat, t_all, 0, prev=(hf, hb))

        o_hy = None
        for length, row0, fw, g, feats in seqs:
            u = dwconv(p, hy_short_w[l], hy_short_b[l], batch, length, row0, hy_off, 3 * hw)
            hid = hyena_hidden(feats, hy_w1p[l], hy_b1[l], hy_freq[l], hy_w2[l], hy_b2[l])
            taps, kn = hyena_taps(hid, hy_w3[l], deltas)
            kspec = hyena_spectrum(fw, taps)
            yp = conv_forward(fw, u, 0, kspec, kn, 0, batch, length, hw)
            z1 = conv_inverse(g, yp, u, 0, u, hw, hy_skip[l], 0, batch, length, F32)
            yp = conv_forward(fw, z1, 0, kspec, kn, 1, batch, length, hw)
            o_hy = conv_inverse(g, yp, z1, 0, u, 2 * hw, hy_skip[l], 1, batch, length, BF16,
                                total_rows=t_all, row0=row0, prev=o_hy)

        m = merge_branches(o_att, o_hy, p, gates, hf, hb, wa_b, wh_b, wl_b, l, lru_y_off, tm)
        xu = out_proj_residual(m, wo_b, l, xu, mods, 2, n_lat, batch, tm)

        fl, route, counts = router(xu, g_ffn[l], mods, 3, rw, rb, n_lat, batch, tm)
        dest, tile_expert, n_used, zero_start = dispatch_plan(route, counts, n_exp, exp_tile, n_exp_tiles)
        xs = moe_dispatch(fl, dest, zero_start, n_exp_tiles * exp_tile, exp_tile, tm, d // 2 // LANES_V7X)
        ys = moe_experts(tile_expert, n_used, xs, exp_w_gate, exp_w_up, exp_w_down, l, exp_tile)
        xu = moe_combine(dest, route, xu, mods, 5, ys, n_lat, batch, tm_cmb)

    return final_norm(xu, final_g, t_lat, tm).reshape(batch, n_lat, d)
```

```python
import functools
import math

import jax
import jax.numpy as jnp
import numpy as np
from jax import lax
from jax.experimental import pallas as pl
from jax.experimental.pallas import tpu as pltpu

F32 = jnp.float32
BF16 = jnp.bfloat16
HIGHEST = lax.Precision.HIGHEST

LANES_V7X = 128
VMEM_LIMIT_V7X = 56 * 1024 * 1024

EPS = 1e-6
GRID_W = 64
HEAD_DIM = 128
N_Q_HEADS = 8
N_KV_HEADS = 2
Q_PER_KV = N_Q_HEADS // N_KV_HEADS
WINDOW = 128
ATTN_BLOCK = 128
ROPE_THETA = 10000.0
HYENA_BANDS = 16
HYENA_DECAY_TARGET = 1e-2
HYENA_FAST_DECAY = 0.3
HYENA_SLOW_DECAY = 1.5
LRU_C = 8.0
N_EXPERTS = 16
N_GROUPS = 4
EXPERTS_PER_GROUP = N_EXPERTS // N_GROUPS
N_MOD = 6
MOD_ROWS = 8


def _cparams(n_axes):
    return pltpu.CompilerParams(dimension_semantics=("arbitrary",) * n_axes,
                                vmem_limit_bytes=VMEM_LIMIT_V7X)


def _sigmoid(x):
    return 0.5 * (1.0 + jnp.tanh(0.5 * x))


def _pack_bf16_pairs(x):
    half = x.shape[1] // 2
    hi = lax.bitcast_convert_type(x[:, :half].astype(BF16).astype(F32), jnp.uint32)
    lo = lax.bitcast_convert_type(x[:, half:].astype(BF16).astype(F32), jnp.uint32)
    return hi | (lo >> 16)


def _unpack_bf16_pairs(w):
    hi = lax.bitcast_convert_type(w & jnp.uint32(0xFFFF0000), F32)
    lo = lax.bitcast_convert_type(w << 16, F32)
    return hi, lo


SUBLANES_V7X = 8


def _store_tile_rows(ref, row0, x):
    rows, width = x.shape
    s_per_row = width // LANES_V7X
    for s in range(s_per_row):
        ref[pl.ds(row0 + s, rows, stride=s_per_row), :] = x[:, s * LANES_V7X:(s + 1) * LANES_V7X]


def _load_tile_rows(ref, row0, rows, s_per_row):
    return jnp.concatenate([ref[pl.ds(row0 + s, rows, stride=s_per_row), :] for s in range(s_per_row)], axis=1)


def _pick(n, cap):
    t = cap
    while n % t:
        t //= 2
    return t


def _ada_kernel(c_ref, w_ref, b_ref, o_ref):
    c = c_ref[...]
    s = (c * jax.nn.sigmoid(c)).astype(BF16)
    o_ref[0] = jnp.dot(s, w_ref[0].astype(BF16), preferred_element_type=F32) + b_ref[0]


def ada_tables(cc, w_ada, b_ada):
    depth, d, n6 = w_ada.shape
    tn = _pick(n6, 1024)
    return pl.pallas_call(
        _ada_kernel,
        grid=(depth, n6 // tn),
        in_specs=[pl.BlockSpec((MOD_ROWS, d), lambda l, j: (0, 0)),
                  pl.BlockSpec((1, d, tn), lambda l, j: (l, 0, j)),
                  pl.BlockSpec((1, 1, tn), lambda l, j: (l, 0, j))],
        out_specs=pl.BlockSpec((1, MOD_ROWS, tn), lambda l, j: (l, 0, j)),
        out_shape=jax.ShapeDtypeStruct((depth, MOD_ROWS, n6), F32),
        compiler_params=_cparams(2), name="ada_tables",
    )(cc, w_ada, b_ada.reshape(depth, 1, n6))


def _mod_spec(which, tm, n_lat, batch, d):
    return pl.BlockSpec((1, 1, 1, d),
                        lambda i, *_: (jnp.minimum((i * tm) // n_lat, batch), which, 0, 0))


def _normmod(x, g, shift, scale):
    y = x * lax.rsqrt(jnp.mean(x * x, axis=-1, keepdims=True) + EPS) * g
    return y * (1.0 + scale) + shift


def _normmod_kernel(x_ref, g_ref, sh_ref, sc_ref, o_ref):
    o_ref[...] = _normmod(x_ref[...], g_ref[...], sh_ref[0, 0], sc_ref[0, 0]).astype(o_ref.dtype)


def normmod(x, g, mods, which_shift, n_lat, batch, tm):
    t, d = x.shape
    return pl.pallas_call(
        _normmod_kernel,
        grid=(t // tm,),
        in_specs=[pl.BlockSpec((tm, d), lambda i: (i, 0)),
                  pl.BlockSpec((1, d), lambda i: (0, 0)),
                  _mod_spec(which_shift, tm, n_lat, batch, d),
                  _mod_spec(which_shift + 1, tm, n_lat, batch, d)],
        out_specs=pl.BlockSpec((tm, d), lambda i: (i, 0)),
        out_shape=jax.ShapeDtypeStruct((t, d), BF16),
        compiler_params=_cparams(1), name="normmod",
    )(x, g.reshape(1, d), mods, mods)


def _final_norm_kernel(x_ref, g_ref, o_ref):
    x = x_ref[...]
    o_ref[...] = x * lax.rsqrt(jnp.mean(x * x, axis=-1, keepdims=True) + EPS) * g_ref[...]


def final_norm(x, g, rows, tm):
    d = x.shape[1]
    return pl.pallas_call(
        _final_norm_kernel,
        grid=(rows // tm,),
        in_specs=[pl.BlockSpec((tm, d), lambda i: (i, 0)), pl.BlockSpec((1, d), lambda i: (0, 0))],
        out_specs=pl.BlockSpec((tm, d), lambda i: (i, 0)),
        out_shape=jax.ShapeDtypeStruct((rows, d), F32),
        compiler_params=_cparams(1), name="final_norm",
    )(x, g.reshape(1, d))


def _proj_kernel(a_ref, w_ref, o_ref, wb_ref):
    @pl.when(pl.program_id(1) == 0)
    def _():
        wb_ref[...] = w_ref[0].astype(BF16)

    o_ref[...] = jnp.dot(a_ref[...], wb_ref[...], preferred_element_type=F32).astype(o_ref.dtype)


def in_proj(a, w, layer, tm, col0, n, out_dtype):
    t, k = a.shape
    tn = _pick(math.gcd(n, col0) if col0 else n, 1024)
    return pl.pallas_call(
        _proj_kernel,
        grid=(n // tn, t // tm),
        in_specs=[pl.BlockSpec((tm, k), lambda j, i: (i, 0)),
                  pl.BlockSpec((1, k, tn), lambda j, i: (layer, 0, col0 // tn + j))],
        out_specs=pl.BlockSpec((tm, tn), lambda j, i: (i, j)),
        out_shape=jax.ShapeDtypeStruct((t, n), out_dtype),
        scratch_shapes=[pltpu.VMEM((k, tn), BF16)],
        compiler_params=_cparams(2), name="in_proj",
    )(a, w)


def _rope(x, tab):
    return x * tab[:, :HEAD_DIM] + pltpu.roll(x, HEAD_DIM // 2, axis=1) * tab[:, HEAD_DIM:]


def _sink_column(sink_ref, h, rows):
    r = lax.broadcasted_iota(jnp.int32, (Q_PER_KV * rows, 1), 0)
    col = jnp.full((Q_PER_KV * rows, 1), sink_ref[h * Q_PER_KV], F32)
    for g in range(1, Q_PER_KV):
        col = jnp.where(r >= g * rows, sink_ref[h * Q_PER_KV + g], col)
    return col


def _nt(a, b):
    return lax.dot_general(a, b, (((1,), (1,)), ((), ())), preferred_element_type=F32)


def _band_bias(n_lat):
    blk = ATTN_BLOCK
    single = n_lat == blk
    qi = (jnp.arange(Q_PER_KV * blk) % blk)[:, None]
    kj = jnp.arange(3 * blk)[None, :]
    band = jnp.abs(qi + blk - kj) <= WINDOW
    kinds = []
    for no_prev, no_next in ((True, single), (False, False), (single, True)):
        ok = band & ((kj >= blk) | (not no_prev)) & ((kj < 2 * blk) | (not no_next))
        kinds.append(jnp.where(ok, 0.0, -jnp.inf))
    return jnp.stack(kinds).astype(F32)


def _lat_attn_kernel(sink_ref, q_ref, kvm_ref, kv0_ref, kvp_ref, kvc_ref, tm_ref, t0_ref, tp_ref, bias_ref,
                     o_ref):
    blk = ATTN_BLOCK
    kv_w = N_KV_HEADS * HEAD_DIM
    scale = HEAD_DIM ** -0.5
    tabs = (tm_ref[...], t0_ref[...], tp_ref[...])
    kvs = (kvm_ref, kv0_ref, kvp_ref)
    for h in range(N_KV_HEADS):
        ks = slice(h * HEAD_DIM, (h + 1) * HEAD_DIM)
        vs = slice(kv_w + h * HEAD_DIM, kv_w + (h + 1) * HEAD_DIM)
        k_loc = jnp.concatenate([_rope(kvs[n][:, ks].astype(F32), tabs[n]) for n in range(3)],
                                axis=0).astype(BF16)
        v_loc = jnp.concatenate([kvs[n][:, vs] for n in range(3)], axis=0).astype(BF16)
        k_ctx = kvc_ref[:, ks].astype(BF16)
        v_ctx = kvc_ref[:, vs].astype(BF16)
        q4 = jnp.concatenate(
            [_rope(q_ref[:, (h * Q_PER_KV + g) * HEAD_DIM:(h * Q_PER_KV + g + 1) * HEAD_DIM].astype(F32), tabs[1])
             for g in range(Q_PER_KV)], axis=0).astype(BF16)
        s_loc = _nt(q4, k_loc) * scale + bias_ref[0]
        s_ctx = _nt(q4, k_ctx) * scale
        sink = _sink_column(sink_ref, h, blk)
        m = jnp.maximum(jnp.maximum(jnp.max(s_loc, axis=-1, keepdims=True),
                                    jnp.max(s_ctx, axis=-1, keepdims=True)), sink)
        p_loc = jnp.exp(s_loc - m)
        p_ctx = jnp.exp(s_ctx - m)
        den = (jnp.sum(p_loc, axis=-1, keepdims=True) + jnp.sum(p_ctx, axis=-1, keepdims=True)
               + jnp.exp(sink - m))
        o = (jnp.dot(p_loc.astype(BF16), v_loc, preferred_element_type=F32)
             + jnp.dot(p_ctx.astype(BF16), v_ctx, preferred_element_type=F32)) / den
        for g in range(Q_PER_KV):
            hq = h * Q_PER_KV + g
            o_ref[:, hq * HEAD_DIM:(hq + 1) * HEAD_DIM] = o[g * blk:(g + 1) * blk].astype(o_ref.dtype)


def latent_attention(p, sink, rope_tab, band_bias, batch, n_lat, n_ctx, q_off):
    blk = ATTN_BLOCK
    nb = n_lat // blk
    aw = N_Q_HEADS * HEAD_DIM
    kvw = 2 * N_KV_HEADS * HEAD_DIM
    ctx_blk0 = (batch * n_lat) // n_ctx
    kv_spec = lambda off: pl.BlockSpec(
        (blk, kvw), lambda b, i: (b * nb + jnp.clip(i + off, 0, nb - 1), 0))
    tab_spec = lambda off: pl.BlockSpec(
        (blk, 2 * HEAD_DIM), lambda b, i: (jnp.clip(i + off, 0, nb - 1), 0))
    bias_kind = lambda b, i: (jnp.where(i == 0, 0, jnp.where(i == nb - 1, 2, 1)), 0, 0)
    return pl.pallas_call(
        _lat_attn_kernel,
        grid=(batch, nb),
        in_specs=[pl.BlockSpec(memory_space=pltpu.SMEM),
                  pl.BlockSpec((blk, aw), lambda b, i: (b * nb + i, q_off // aw)),
                  kv_spec(-1), kv_spec(0), kv_spec(1),
                  pl.BlockSpec((n_ctx, kvw), lambda b, i: (ctx_blk0 + b, 0)),
                  tab_spec(-1), tab_spec(0), tab_spec(1),
                  pl.BlockSpec((1, Q_PER_KV * blk, 3 * blk), bias_kind)],
        out_specs=pl.BlockSpec((blk, aw), lambda b, i: (b * nb + i, 0)),
        out_shape=jax.ShapeDtypeStruct((p.shape[0], aw), BF16),
        compiler_params=_cparams(2), name="latent_attention",
    )(sink, p, p, p, p, p, rope_tab, rope_tab, rope_tab, band_bias)


def _ctx_attn_kernel(sink_ref, q_ref, kv_ref, o_lat_ref, o_ref):
    del o_lat_ref
    rows = q_ref.shape[0]
    kv_w = N_KV_HEADS * HEAD_DIM
    scale = HEAD_DIM ** -0.5
    for h in range(N_KV_HEADS):
        k = kv_ref[:, h * HEAD_DIM:(h + 1) * HEAD_DIM].astype(BF16)
        v = kv_ref[:, kv_w + h * HEAD_DIM:kv_w + (h + 1) * HEAD_DIM].astype(BF16)
        q4 = jnp.concatenate(
            [q_ref[:, (h * Q_PER_KV + g) * HEAD_DIM:(h * Q_PER_KV + g + 1) * HEAD_DIM]
             for g in range(Q_PER_KV)], axis=0).astype(BF16)
        s = _nt(q4, k) * scale
        sink = _sink_column(sink_ref, h, rows)
        m = jnp.maximum(jnp.max(s, axis=-1, keepdims=True), sink)
        p = jnp.exp(s - m)
        den = jnp.sum(p, axis=-1, keepdims=True) + jnp.exp(sink - m)
        o = jnp.dot(p.astype(BF16), v, preferred_element_type=F32) / den
        for g in range(Q_PER_KV):
            hq = h * Q_PER_KV + g
            o_ref[:, hq * HEAD_DIM:(hq + 1) * HEAD_DIM] = o[g * rows:(g + 1) * rows].astype(o_ref.dtype)


def context_attention(p, sink, o_lat, batch, n_lat, n_ctx, q_off):
    aw = N_Q_HEADS * HEAD_DIM
    kvw = 2 * N_KV_HEADS * HEAD_DIM
    ctx_blk0 = (batch * n_lat) // n_ctx
    return pl.pallas_call(
        _ctx_attn_kernel,
        grid=(batch,),
        in_specs=[pl.BlockSpec(memory_space=pltpu.SMEM),
                  pl.BlockSpec((n_ctx, aw), lambda b: (ctx_blk0 + b, q_off // aw)),
                  pl.BlockSpec((n_ctx, kvw), lambda b: (ctx_blk0 + b, 0)),
                  pl.BlockSpec(memory_space=pl.ANY)],
        out_specs=pl.BlockSpec((n_ctx, aw), lambda b: (ctx_blk0 + b, 0)),
        out_shape=jax.ShapeDtypeStruct(o_lat.shape, o_lat.dtype),
        input_output_aliases={3: 0},
        compiler_params=_cparams(1), name="context_attention",
    )(sink, p, p, o_lat)


def _dwconv_kernel(u_ref, w_ref, b_ref, o_ref, *scratch, width):
    x = u_ref[...].astype(F32)
    length = x.shape[0]
    left = width // 2
    t = lax.broadcasted_iota(jnp.int32, (length, 1), 0)
    acc = jnp.broadcast_to(b_ref[...], x.shape)
    for j in range(width):
        s = j - left
        if s == 0:
            xs = x
        else:
            xs = pltpu.roll(x, (-s) % length, axis=0)
            xs = jnp.where((t + s >= 0) & (t + s < length), xs, 0.0)
        acc = acc + xs * w_ref[j:j + 1, :]
    if not scratch:
        o_ref[...] = acc
    else:
        half = length // 2
        scratch[0][...] = acc
        o_ref[:half, :] = scratch[0][pl.ds(0, half, stride=2), :].astype(o_ref.dtype)
        o_ref[half:, :] = scratch[0][pl.ds(1, half, stride=2), :].astype(o_ref.dtype)


def dwconv(p, w, b, n_seq, length, row0, col0, n_ch, parity_order=False):
    width = w.shape[0]
    cb = LANES_V7X if parity_order else _pick(n_ch, 256)
    return pl.pallas_call(
        functools.partial(_dwconv_kernel, width=width),
        grid=(n_seq, n_ch // cb),
        in_specs=[pl.BlockSpec((length, cb), lambda s, c: (row0 // length + s, col0 // cb + c)),
                  pl.BlockSpec((width, cb), lambda s, c: (0, c)),
                  pl.BlockSpec((1, cb), lambda s, c: (0, c))],
        out_specs=pl.BlockSpec((length, cb), lambda s, c: (s, c)),
        out_shape=jax.ShapeDtypeStruct((n_seq * length, n_ch), BF16 if parity_order else F32),
        scratch_shapes=[pltpu.VMEM((length, cb), F32)] if parity_order else [],
        compiler_params=_cparams(2), name="dwconv",
    )(p, w, b.reshape(1, n_ch))


def _lru_kernel(xf_ref, xb_ref, w_ref, bias_ref, sp_ref, h0_ref, *rest):
    hf_ref, hb_ref, hl_ref, af, bf, ab, bb, carry = rest[-8:]
    j = pl.program_id(1)
    width = xf_ref.shape[1]

    @pl.when(j == 0)
    def _():
        carry[...] = h0_ref[0]

    def gates(x, d):
        g = jnp.dot(x.astype(BF16), w_ref[0, d], preferred_element_type=F32) + bias_ref[d]
        r = _sigmoid(g[:, :width])
        gi = _sigmoid(g[:, width:])
        a = jnp.exp(-LRU_C * r * sp_ref[d])
        return a, jnp.sqrt(1.0 - a * a) * (gi * x)

    af[...], bf[...] = gates(xf_ref[...], 0)
    ab[...], bb[...] = gates(xb_ref[...], 1)
    tl = af.shape[0]

    def body(t, hs):
        hf, hb = hs
        hf = af[pl.ds(t, 1), :] * hf + bf[pl.ds(t, 1), :]
        hf_ref[pl.ds(t, 1), :] = hf
        tb = tl - 1 - t
        hb = ab[pl.ds(tb, 1), :] * hb + bb[pl.ds(tb, 1), :]
        hb_ref[pl.ds(tb, 1), :] = hb
        return hf, hb

    hf, hb = lax.fori_loop(0, tl, body, (carry[0:1, :], carry[1:2, :]), unroll=8)
    carry[0:1, :] = hf
    carry[1:2, :] = hb
    hl_ref[0] = carry[...]


def rg_lru(xc, w_gates, layer, b_gates, softplus_neg_lam, h0, n_seq, length, total_rows, row0, prev=None):
    width = xc.shape[1]
    tl = _pick(length, 512)
    nc = length // tl
    blk0 = row0 // tl
    in_specs = [pl.BlockSpec((tl, width), lambda s, j: (s * nc + j, 0)),
                pl.BlockSpec((tl, width), lambda s, j: (s * nc + nc - 1 - j, 0)),
                pl.BlockSpec((1, 2, width, 2 * width), lambda s, j: (layer, 0, 0, 0)),
                pl.BlockSpec((2, 1, 2 * width), lambda s, j: (0, 0, 0)),
                pl.BlockSpec((2, 1, width), lambda s, j: (0, 0, 0)),
                pl.BlockSpec((1, 2, width), lambda s, j: (s, 0, 0))]
    args = [xc, xc, w_gates, b_gates, softplus_neg_lam, h0]
    aliases = {}
    if prev is not None:
        in_specs += [pl.BlockSpec(memory_space=pl.ANY)] * 2
        aliases = {len(args): 0, len(args) + 1: 1}
        args += list(prev)
    return pl.pallas_call(
        _lru_kernel,
        grid=(n_seq, nc),
        in_specs=in_specs,
        out_specs=[pl.BlockSpec((tl, width), lambda s, j: (blk0 + s * nc + j, 0)),
                   pl.BlockSpec((tl, width), lambda s, j: (blk0 + s * nc + nc - 1 - j, 0)),
                   pl.BlockSpec((1, 2, width), lambda s, j: (s, 0, 0))],
        out_shape=[jax.ShapeDtypeStruct((total_rows, width), F32),
                   jax.ShapeDtypeStruct((total_rows, width), F32),
                   jax.ShapeDtypeStruct((n_seq, 2, width), F32)],
        scratch_shapes=[pltpu.VMEM((tl, width), F32)] * 4 + [pltpu.VMEM((2, width), F32)],
        input_output_aliases=aliases,
        compiler_params=_cparams(2), name="rg_lru",
    )(*args)


def _softplus_kernel(x_ref, o_ref):
    x = -x_ref[...]
    o_ref[...] = jnp.maximum(x, 0.0) + jnp.log(1.0 + jnp.exp(-jnp.abs(x)))


def softplus_neg(lam):
    depth, two, width = lam.shape
    x = lam.reshape(depth * two, width)
    out = pl.pallas_call(
        _softplus_kernel,
        out_shape=jax.ShapeDtypeStruct(x.shape, F32), name="softplus_neg",
    )(x)
    return out.reshape(depth, two, 1, width)


def _hy_hidden_kernel(z_ref, w1_ref, b1_ref, fr_ref, w2_ref, b2_ref, o_ref):
    fr = fr_ref[...]
    h = jnp.sin(fr * (jnp.dot(z_ref[...], w1_ref[...], precision=HIGHEST,
                              preferred_element_type=F32) + b1_ref[...]))
    o_ref[...] = jnp.sin(fr * (jnp.dot(h, w2_ref[...], precision=HIGHEST,
                                       preferred_element_type=F32) + b2_ref[...]))


def hyena_hidden(z, w1, b1, freq, w2, b2):
    length = z.shape[0]
    fh = w2.shape[0]
    return pl.pallas_call(
        _hy_hidden_kernel,
        out_shape=jax.ShapeDtypeStruct((length, fh), F32), name="hyena_hidden",
    )(z, w1, b1.reshape(1, fh), freq.reshape(1, fh), w2, b2.reshape(1, fh))


def _hy_taps_kernel(hid_ref, wf_ref, wb_ref, delta_ref, u_ref, kh_ref, tmp):
    hid = hid_ref[...]
    length = hid.shape[0]
    hf = jnp.dot(hid, wf_ref[...], precision=HIGHEST, preferred_element_type=F32)
    hb = jnp.dot(hid, wb_ref[...], precision=HIGHEST, preferred_element_type=F32)
    ti = lax.broadcasted_iota(jnp.int32, (length, 1), 0)
    win = jnp.exp(-(ti.astype(F32) * (1.0 / (length - 1))) * delta_ref[...])
    f = hf * win
    b = jnp.where(ti >= 1, hb * win, 0.0)
    sc = lax.rsqrt(jnp.sum(f * f + b * b, axis=0, keepdims=True) + EPS)
    u1 = (f + b) * sc
    u2 = (f - b) * sc
    quarter = jnp.where((ti & 1) == 0, (1 - (ti & 2)).astype(F32), 0.0)
    kh_ref[...] = jnp.sum(u1 * quarter, axis=0, keepdims=True)
    half = length // 2
    for k, u in enumerate((u1, u2)):
        tmp[...] = u
        u_ref[k, :half, :] = tmp[pl.ds(0, half, stride=2), :].astype(u_ref.dtype)
        u_ref[k, half:, :] = tmp[pl.ds(1, half, stride=2), :].astype(u_ref.dtype)


def hyena_taps(hid, w3, deltas):
    length, fh = hid.shape
    hw = deltas.shape[1]
    n_ord = w3.shape[1] // (2 * hw)
    cw = _pick(hw, 128)
    nc = hw // cw
    return pl.pallas_call(
        _hy_taps_kernel,
        grid=(n_ord, nc),
        in_specs=[pl.BlockSpec((length, fh), lambda o, c: (0, 0)),
                  pl.BlockSpec((fh, cw), lambda o, c: (0, o * 2 * nc + c)),
                  pl.BlockSpec((fh, cw), lambda o, c: (0, o * 2 * nc + nc + c)),
                  pl.BlockSpec((1, cw), lambda o, c: (0, c))],
        out_specs=[pl.BlockSpec((2, length, cw), lambda o, c: (0, 0, o * nc + c)),
                   pl.BlockSpec((1, cw), lambda o, c: (0, o * nc + c))],
        out_shape=[jax.ShapeDtypeStruct((2, length, n_ord * hw), BF16),
                   jax.ShapeDtypeStruct((1, n_ord * hw), F32)],
        scratch_shapes=[pltpu.VMEM((length, cw), F32)],
        compiler_params=_cparams(2), name="hyena_taps",
    )(hid, w3, w3, deltas)


def _dft_halves(fe_ref, fo_ref, xe_r, xo_r, xe_i, xo_i, first):
    e_r = jnp.dot(fe_ref[0], xe_r, preferred_element_type=F32)
    e_i = jnp.dot(fe_ref[1], xe_i, preferred_element_type=F32)
    o_r = jnp.dot(fo_ref[0], xo_r, preferred_element_type=F32)
    o_i = jnp.dot(fo_ref[1], xo_i, preferred_element_type=F32)
    return e_r + o_r, jnp.where(first, e_i, e_i + o_i), e_r - o_r, jnp.where(first, o_i, o_i - e_i)


def _first_row(rows, tile_index):
    return (lax.broadcasted_iota(jnp.int32, (rows, 1), 0) == 0) & (tile_index == 0)


def _spectrum_kernel(fe_ref, fo_ref, u_ref, o_ref):
    half = u_ref.shape[1] // 2
    first = _first_row(o_ref.shape[1], pl.program_id(1))
    planes = _dft_halves(fe_ref, fo_ref, u_ref[0, :half, :], u_ref[0, half:, :],
                         u_ref[1, :half, :], u_ref[1, half:, :], first)
    for k, plane in enumerate(planes):
        o_ref[k] = plane


def hyena_spectrum(fe, fo, u):
    _, length, n = u.shape
    half = length // 2
    tm = _pick(half, 256)
    tn = _pick(n, 512)
    return pl.pallas_call(
        _spectrum_kernel,
        grid=(n // tn, half // tm),
        in_specs=[pl.BlockSpec((2, tm, half), lambda c, i: (0, i, 0)),
                  pl.BlockSpec((2, tm, half), lambda c, i: (0, i, 0)),
                  pl.BlockSpec((2, length, tn), lambda c, i: (0, 0, c))],
        out_specs=pl.BlockSpec((4, tm, tn), lambda c, i: (0, i, c)),
        out_shape=jax.ShapeDtypeStruct((4, half, n), F32),
        compiler_params=_cparams(2), name="hyena_spectrum",
    )(fe, fo, u)


def _conv_fwd_kernel(fe_ref, fo_ref, z_ref, k_ref, kh_ref, y_ref):
    half = z_ref.shape[0] // 2
    first = _first_row(y_ref.shape[2], pl.program_id(1))
    ze = z_ref[:half, :]
    zo = z_ref[half:, :]
    a_r, a_i, b_r, b_i = _dft_halves(fe_ref, fo_ref, ze, zo, ze, zo, first)
    ka_r, ka_i, kb_r, kb_i = k_ref[0], k_ref[1], k_ref[2], k_ref[3]
    kh_r = kh_ref[...]
    ya_r = jnp.where(first, 0.5 * a_r * ka_r, a_r * ka_r - a_i * ka_i)
    yb_r = jnp.where(first, 0.5 * b_r * kb_r, b_r * kb_r - b_i * kb_i)
    ya_i = jnp.where(first, a_i * kh_r - b_i * kb_i, a_r * ka_i + a_i * ka_r)
    yb_i = jnp.where(first, a_i * kb_i + b_i * kh_r, b_r * kb_i + b_i * kb_r)
    y_ref[0, 0] = (ya_r + yb_r).astype(y_ref.dtype)
    y_ref[0, 1] = jnp.where(first, ya_i, ya_i - yb_i).astype(y_ref.dtype)
    y_ref[0, 2] = (ya_r - yb_r).astype(y_ref.dtype)
    y_ref[0, 3] = jnp.where(first, yb_i, ya_i + yb_i).astype(y_ref.dtype)


def conv_forward(fe, fo, z, col0, kspec, kh, order, n_seq, length, hw):
    half = length // 2
    tm = _pick(half, 512)
    return pl.pallas_call(
        _conv_fwd_kernel,
        grid=(n_seq, half // tm),
        in_specs=[pl.BlockSpec((2, tm, half), lambda s, i: (0, i, 0)),
                  pl.BlockSpec((2, tm, half), lambda s, i: (0, i, 0)),
                  pl.BlockSpec((length, hw), lambda s, i: (s, col0 // hw)),
                  pl.BlockSpec((4, tm, hw), lambda s, i: (0, i, order)),
                  pl.BlockSpec((1, hw), lambda s, i: (0, order))],
        out_specs=pl.BlockSpec((1, 4, tm, hw), lambda s, i: (s, 0, i, 0)),
        out_shape=jax.ShapeDtypeStruct((n_seq, 4, half, hw), BF16),
        compiler_params=_cparams(2), name="hyena_conv_fwd",
    )(fe, fo, z, kspec, kh)


def _conv_inv_kernel(ie_ref, io_ref, y_ref, z_ref, gate_ref, skip_ref, o_ref, *, inv_len):
    skip = skip_ref[0]
    for par, inv_ref in enumerate((ie_ref, io_ref)):
        y = jnp.dot(inv_ref[...], y_ref[0, par], preferred_element_type=F32) * inv_len
        o_ref[0, par] = (gate_ref[0, par].astype(F32) * (y + skip * z_ref[0, par].astype(F32))
                         ).astype(o_ref.dtype)


def conv_inverse(ie, io, pq, z, z_col0, gates, gate_col0, skip, order, n_seq, length, out_dtype):
    hw = pq.shape[3]
    half = length // 2
    tm = _pick(half, 512)
    par = lambda a: a.reshape(n_seq, 2, half, a.shape[1])
    out = pl.pallas_call(
        functools.partial(_conv_inv_kernel, inv_len=1.0 / length),
        grid=(n_seq, half // tm),
        in_specs=[pl.BlockSpec((tm, length), lambda s, i: (i, 0)),
                  pl.BlockSpec((tm, length), lambda s, i: (i, 0)),
                  pl.BlockSpec((1, 2, length, hw), lambda s, i: (s, 0, 0, 0)),
                  pl.BlockSpec((1, 2, tm, hw), lambda s, i: (s, 0, i, z_col0 // hw)),
                  pl.BlockSpec((1, 2, tm, hw), lambda s, i: (s, 0, i, gate_col0 // hw)),
                  pl.BlockSpec((1, 1, hw), lambda s, i: (order, 0, 0))],
        out_specs=pl.BlockSpec((1, 2, tm, hw), lambda s, i: (s, 0, i, 0)),
        out_shape=jax.ShapeDtypeStruct((n_seq, 2, half, hw), out_dtype),
        compiler_params=_cparams(2), name="hyena_conv_inv",
    )(ie, io, pq.reshape(n_seq, 2, length, hw), par(z), par(gates), skip.reshape(skip.shape[0], 1, hw))
    return out.reshape(n_seq * length, hw)


def _merge_kernel(oa_ref, oh_ref, y_ref, hf_ref, hb_ref, g_ref, wa_ref, wh_ref, wl_ref, o_ref, *, tn):
    d = o_ref.shape[1]
    ol = (jax.nn.gelu(y_ref[...].astype(F32), approximate=True) * (hf_ref[...] + hb_ref[...])).astype(BF16)
    oa = oa_ref[...]
    oh = oh_ref[...]
    for c0 in range(0, d, tn):
        cols = slice(c0, c0 + tn)
        gate = lambda k: _sigmoid(g_ref[:, k * d + c0:k * d + c0 + tn].astype(F32))
        m = (gate(0) * jnp.dot(oa, wa_ref[0, :, cols], preferred_element_type=F32)
             + gate(1) * jnp.dot(oh, wh_ref[0, :, cols], preferred_element_type=F32)
             + gate(2) * jnp.dot(ol, wl_ref[0, :, cols], preferred_element_type=F32))
        o_ref[:, cols] = m.astype(o_ref.dtype)


def merge_branches(o_att, o_hy, p, gates, hf, hb, wa, wh, wl, layer, lru_y_off, tm):
    t, aw = o_att.shape
    hw = o_hy.shape[1]
    lw = hf.shape[1]
    d = wa.shape[2]
    w_spec = lambda rows: pl.BlockSpec((1, rows, d), lambda i: (layer, 0, 0))
    return pl.pallas_call(
        functools.partial(_merge_kernel, tn=_pick(d, 512)),
        grid=(t // tm,),
        in_specs=[pl.BlockSpec((tm, aw), lambda i: (i, 0)),
                  pl.BlockSpec((tm, hw), lambda i: (i, 0)),
                  pl.BlockSpec((tm, lw), lambda i: (i, lru_y_off // lw)),
                  pl.BlockSpec((tm, lw), lambda i: (i, 0)),
                  pl.BlockSpec((tm, lw), lambda i: (i, 0)),
                  pl.BlockSpec((tm, 3 * d), lambda i: (i, 0)),
                  w_spec(aw), w_spec(hw), w_spec(lw)],
        out_specs=pl.BlockSpec((tm, d), lambda i: (i, 0)),
        out_shape=jax.ShapeDtypeStruct((t, d), BF16),
        compiler_params=_cparams(1), name="merge_branches",
    )(o_att, o_hy, p, hf, hb, gates, wa, wh, wl)


def _out_proj_kernel(m_ref, w_ref, x_ref, gate_ref, o_ref):
    o_ref[...] = x_ref[...] + gate_ref[0, 0] * jnp.dot(m_ref[...], w_ref[0],
                                                       preferred_element_type=F32)


def out_proj_residual(m, w, layer, x, mods, which_gate, n_lat, batch, tm):
    t, d = x.shape
    return pl.pallas_call(
        _out_proj_kernel,
        grid=(t // tm,),
        in_specs=[pl.BlockSpec((tm, d), lambda i: (i, 0)),
                  pl.BlockSpec((1, d, d), lambda i: (layer, 0, 0)),
                  pl.BlockSpec((tm, d), lambda i: (i, 0)),
                  _mod_spec(which_gate, tm, n_lat, batch, d)],
        out_specs=pl.BlockSpec((tm, d), lambda i: (i, 0)),
        out_shape=jax.ShapeDtypeStruct((t, d), F32),
        compiler_params=_cparams(1), name="out_proj_residual",
    )(m, w, x, mods)


def _router_kernel(x_ref, g_ref, sh_ref, sc_ref, rw_ref, rb_ref, h_ref, meta_ref, cnt_ref):
    h = _normmod(x_ref[...], g_ref[...], sh_ref[0, 0], sc_ref[0, 0])
    _store_tile_rows(h_ref, 0, _pack_bf16_pairs(h))
    w = rw_ref[...]
    h_hi, w_hi = h.astype(BF16), w.astype(BF16)
    h_lo = (h - h_hi.astype(F32)).astype(BF16)
    w_lo = (w - w_hi.astype(F32)).astype(BF16)
    logits = (jnp.dot(h_hi, w_hi, preferred_element_type=F32) + jnp.dot(h_lo, w_hi, preferred_element_type=F32)
              + jnp.dot(h_hi, w_lo, preferred_element_type=F32)) + rb_ref[...]
    lane = lax.broadcasted_iota(jnp.int32, logits.shape, 1)
    valid = lane < N_EXPERTS
    logits = jnp.where(valid, logits, -jnp.inf)
    ex = jnp.exp(logits - jnp.max(logits, axis=-1, keepdims=True))
    scores = ex / jnp.sum(ex, axis=-1, keepdims=True)
    grp = lax.shift_right_logical(lane, int(math.log2(EXPERTS_PER_GROUP)))
    best = jnp.zeros((logits.shape[0], 1), jnp.int32)
    best_max = jnp.max(jnp.where((grp == 0) & valid, scores, -1.0), axis=-1, keepdims=True)
    for g in range(1, N_GROUPS):
        gm = jnp.max(jnp.where((grp == g) & valid, scores, -1.0), axis=-1, keepdims=True)
        better = gm > best_max
        best = jnp.where(better, g, best)
        best_max = jnp.where(better, gm, best_max)
    s1 = jnp.where((grp == best) & valid, scores, -1.0)
    m1 = jnp.max(s1, axis=-1, keepdims=True)
    i1 = jnp.min(jnp.where(s1 == m1, lane, LANES_V7X), axis=-1, keepdims=True)
    s2 = jnp.where(lane == i1, -1.0, s1)
    m2 = jnp.max(s2, axis=-1, keepdims=True)
    i2 = jnp.min(jnp.where(s2 == m2, lane, LANES_V7X), axis=-1, keepdims=True)
    tot = m1 + m2
    @pl.when(pl.program_id(0) == 0)
    def _():
        cnt_ref[...] = jnp.zeros_like(cnt_ref)

    tm = logits.shape[0]
    onehot = ((lane == i1) | (lane == i2)).astype(BF16)
    lower = (lax.broadcasted_iota(jnp.int32, (tm, tm), 0)
             > lax.broadcasted_iota(jnp.int32, (tm, tm), 1)).astype(BF16)
    before = jnp.dot(lower, onehot, preferred_element_type=F32) + cnt_ref[...]
    r1 = jnp.sum(jnp.where(lane == i1, before, 0.0), axis=-1, keepdims=True)
    r2 = jnp.sum(jnp.where(lane == i2, before, 0.0), axis=-1, keepdims=True)
    cnt_ref[...] += jnp.sum(onehot.astype(F32), axis=0, keepdims=True)
    cols = (i1.astype(F32), i2.astype(F32), r1, r2, m1 / tot, m2 / tot)
    meta = jnp.zeros(logits.shape, F32)
    for k, col in enumerate(cols):
        meta = jnp.where(lane == k, col, meta)
    meta_ref[...] = meta


ROUTE_E, ROUTE_RANK, ROUTE_W = 0, 2, 4


def router(x, g, mods, which_shift, rw, rb, n_lat, batch, tm):
    t, d = x.shape
    spr = d // 2 // LANES_V7X
    return pl.pallas_call(
        _router_kernel,
        grid=(t // tm,),
        in_specs=[pl.BlockSpec((tm, d), lambda i: (i, 0)),
                  pl.BlockSpec((1, d), lambda i: (0, 0)),
                  _mod_spec(which_shift, tm, n_lat, batch, d),
                  _mod_spec(which_shift + 1, tm, n_lat, batch, d),
                  pl.BlockSpec((d, LANES_V7X), lambda i: (0, 0)),
                  pl.BlockSpec((1, LANES_V7X), lambda i: (0, 0))],
        out_specs=[pl.BlockSpec((tm * spr, LANES_V7X), lambda i: (i, 0)),
                   pl.BlockSpec((tm, LANES_V7X), lambda i: (i, 0)),
                   pl.BlockSpec((1, LANES_V7X), lambda i: (0, 0))],
        out_shape=[jax.ShapeDtypeStruct((t * spr, LANES_V7X), jnp.uint32),
                   jax.ShapeDtypeStruct((t, LANES_V7X), F32),
                   jax.ShapeDtypeStruct((1, LANES_V7X), F32)],
        compiler_params=_cparams(1), name="router",
    )(x, g.reshape(1, d), mods, mods, rw, rb)


def dispatch_plan(meta, counts, n_exp, tile, n_tiles):
    cnt = counts[0, :n_exp].astype(jnp.int32)
    tiles_e = (cnt + tile - 1) // tile
    tile_end = jnp.cumsum(tiles_e)
    row_start = (tile_end - tiles_e) * tile
    experts = meta[:, ROUTE_E:ROUTE_E + 2].astype(jnp.int32)
    ranks = meta[:, ROUTE_RANK:ROUTE_RANK + 2].astype(jnp.int32)
    dest = (row_start[experts] + ranks).reshape(-1)
    tile_expert = jnp.minimum(jnp.sum(jnp.arange(n_tiles)[:, None] >= tile_end[None, :], axis=1),
                              n_exp - 1).astype(jnp.int32)
    zero_start = jnp.minimum(row_start + cnt, (n_tiles - 1) * tile).astype(jnp.int32)
    return dest, tile_expert, tile_end[-1:].astype(jnp.int32), zero_start


def _dispatch_kernel(dest_ref, zero_ref, h_ref, xs_ref, zbuf, sem, zsem, *, spr):
    tm = h_ref.shape[0] // spr
    base = pl.program_id(0) * (2 * tm)
    token = lambda ref, t: ref.at[pl.ds(pl.multiple_of(t * spr, spr), spr)]

    @pl.when(pl.program_id(0) == 0)
    def _():
        zbuf[...] = jnp.zeros_like(zbuf)

        def clear(e, carry):
            start = pl.multiple_of(zero_ref[e] * spr, spr)
            copy = pltpu.make_async_copy(zbuf, xs_ref.at[pl.ds(start, zbuf.shape[0])], zsem)
            copy.start()
            copy.wait()
            return carry

        lax.fori_loop(0, zero_ref.shape[0], clear, 0)

    def issue(r, carry):
        for k in range(2):
            pltpu.make_async_copy(token(h_ref, r), token(xs_ref, dest_ref[base + 2 * r + k]), sem).start()
        return carry

    lax.fori_loop(0, tm, issue, 0, unroll=8)
    for k in range(2):
        pltpu.make_async_copy(h_ref, xs_ref.at[pl.ds(0, tm * spr)], sem).wait()


def moe_dispatch(h, dest, zero_start, n_rows, tile, tm, spr):
    assert spr == SUBLANES_V7X
    t = h.shape[0] // spr
    return pl.pallas_call(
        functools.partial(_dispatch_kernel, spr=spr),
        grid_spec=pltpu.PrefetchScalarGridSpec(
            num_scalar_prefetch=2, grid=(t // tm,),
            in_specs=[pl.BlockSpec((tm * spr, LANES_V7X), lambda i, dest, zs: (i, 0))],
            out_specs=pl.BlockSpec(memory_space=pl.ANY),
            scratch_shapes=[pltpu.VMEM((tile * spr, LANES_V7X), h.dtype),
                            pltpu.SemaphoreType.DMA, pltpu.SemaphoreType.DMA]),
        out_shape=jax.ShapeDtypeStruct((n_rows * spr, LANES_V7X), h.dtype),
        compiler_params=_cparams(1), name="moe_dispatch",
    )(dest, zero_start, h)


def _experts_kernel(te_ref, nu_ref, xs_ref, wg_ref, wu_ref, wd_ref, ys_ref, wgb, wub, wdb, *, spr):
    j = pl.program_id(0)

    @pl.when(j < nu_ref[0])
    def _():
        @pl.when((j == 0) | (te_ref[j] != te_ref[jnp.maximum(j - 1, 0)]))
        def _():
            wgb[...] = wg_ref[0, 0].astype(BF16)
            wub[...] = wu_ref[0, 0].astype(BF16)
            wdb[...] = wd_ref[0, 0].astype(BF16)

        tile = xs_ref.shape[0] // spr
        x_hi, x_lo = (v.astype(BF16) for v in _unpack_bf16_pairs(_load_tile_rows(xs_ref, 0, tile, spr)))
        half = x_hi.shape[1]

        def proj(w):
            return (jnp.dot(x_hi, w[:half, :], preferred_element_type=F32)
                    + jnp.dot(x_lo, w[half:, :], preferred_element_type=F32))

        gt = proj(wgb)
        act = (gt * _sigmoid(gt)) * proj(wub)
        _store_tile_rows(ys_ref, 0, _pack_bf16_pairs(
            jnp.dot(act.astype(BF16), wdb[...], preferred_element_type=F32)))


def moe_experts(tile_expert, n_used, xs, wg, wu, wd, layer, tile):
    d, de = wg.shape[2], wg.shape[3]
    spr = d // 2 // LANES_V7X
    n_tiles = xs.shape[0] // (tile * spr)
    used = lambda j, nu: jnp.minimum(j, nu[0] - 1)
    w_spec = lambda shape: pl.BlockSpec((1, 1) + shape, lambda j, te, nu: (layer, te[used(j, nu)], 0, 0))
    return pl.pallas_call(
        functools.partial(_experts_kernel, spr=spr),
        grid_spec=pltpu.PrefetchScalarGridSpec(
            num_scalar_prefetch=2, grid=(n_tiles,),
            in_specs=[pl.BlockSpec((tile * spr, LANES_V7X), lambda j, te, nu: (used(j, nu), 0)),
                      w_spec((d, de)), w_spec((d, de)), w_spec((de, d))],
            out_specs=pl.BlockSpec((tile * spr, LANES_V7X), lambda j, te, nu: (used(j, nu), 0)),
            scratch_shapes=[pltpu.VMEM((d, de), BF16), pltpu.VMEM((d, de), BF16),
                            pltpu.VMEM((de, d), BF16)]),
        out_shape=jax.ShapeDtypeStruct(xs.shape, xs.dtype),
        compiler_params=_cparams(1), name="moe_experts",
    )(tile_expert, n_used, xs, wg, wu, wd)


def _combine_kernel(dest_ref, meta_ref, x_ref, gate_ref, ys_ref, o_ref, buf, sem, *, spr):
    i = pl.program_id(0)
    tm = x_ref.shape[0]
    part = lambda slot, k: pl.multiple_of((slot * 2 + k) * (tm * spr), spr)

    def gather(tile, slot):
        base = tile * (2 * tm)

        def issue(r, carry):
            for k in range(2):
                row = dest_ref[base + 2 * r + k]
                pltpu.make_async_copy(ys_ref.at[pl.ds(pl.multiple_of(row * spr, spr), spr)],
                                      buf.at[pl.ds(part(slot, k) + r * spr, spr)], sem.at[slot]).start()
            return carry

        lax.fori_loop(0, tm, issue, 0, unroll=8)

    @pl.when(i == 0)
    def _():
        gather(0, 0)

    @pl.when(i + 1 < pl.num_programs(0))
    def _():
        gather(i + 1, (i + 1) % 2)

    slot = i % 2
    for k in range(2):
        pltpu.make_async_copy(ys_ref.at[pl.ds(0, tm * spr)], buf.at[pl.ds(part(slot, k), tm * spr)],
                              sem.at[slot]).wait()
    meta = meta_ref[...]
    w1 = meta[:, ROUTE_W:ROUTE_W + 1]
    w2 = meta[:, ROUTE_W + 1:ROUTE_W + 2]
    y1_hi, y1_lo = _unpack_bf16_pairs(_load_tile_rows(buf, part(slot, 0), tm, spr))
    y2_hi, y2_lo = _unpack_bf16_pairs(_load_tile_rows(buf, part(slot, 1), tm, spr))
    half = y1_hi.shape[1]
    gate = gate_ref[0, 0]
    o_ref[:, :half] = x_ref[:, :half] + gate[:, :half] * (w1 * y1_hi + w2 * y2_hi)
    o_ref[:, half:] = x_ref[:, half:] + gate[:, half:] * (w1 * y1_lo + w2 * y2_lo)


def moe_combine(dest, meta, x, mods, which_gate, ys, n_lat, batch, tm):
    t, d = x.shape
    spr = d // 2 // LANES_V7X
    return pl.pallas_call(
        functools.partial(_combine_kernel, spr=spr),
        grid_spec=pltpu.PrefetchScalarGridSpec(
            num_scalar_prefetch=1, grid=(t // tm,),
            in_specs=[pl.BlockSpec((tm, LANES_V7X), lambda i, dest: (i, 0)),
                      pl.BlockSpec((tm, d), lambda i, dest: (i, 0)),
                      _mod_spec(which_gate, tm, n_lat, batch, d),
                      pl.BlockSpec(memory_space=pl.ANY)],
            out_specs=pl.BlockSpec((tm, d), lambda i, dest: (i, 0)),
            scratch_shapes=[pltpu.VMEM((2 * 2 * tm * spr, LANES_V7X), ys.dtype),
                            pltpu.SemaphoreType.DMA((2,))]),
        out_shape=jax.ShapeDtypeStruct((t, d), F32),
        compiler_params=_cparams(1), name="moe_combine",
    )(dest, meta, x, mods, ys)


def _rope_table(n_tokens):
    rows = n_tokens // GRID_W
    row = jnp.repeat(jnp.arange(rows), GRID_W).astype(F32)
    col = jnp.tile(jnp.arange(GRID_W), rows).astype(F32)
    pairs = HEAD_DIM // 4
    inv = ROPE_THETA ** (-jnp.arange(pairs, dtype=F32) / pairs)
    ang = jnp.concatenate([row[:, None] * inv, col[:, None] * inv], axis=-1)
    cos, sin = jnp.cos(ang), jnp.sin(ang)
    return jnp.concatenate([cos, cos, -sin, sin], axis=-1)


def _trig_kernel(ca_ref, sa_ref, cr_ref, sr_ref, o_ref, *, sign, inverse):
    tm = cr_ref.shape[0]
    n_lanes = ca_ref.shape[1] * LANES_V7X
    cr = cr_ref[...]
    sr = sr_ref[...]
    row = pl.program_id(0) * tm + lax.broadcasted_iota(jnp.int32, (tm, 1), 0)
    alt_row = sign * (1 - 2 * (row & 1)).astype(F32)
    for a in range(n_lanes // LANES_V7X):
        blk = slice(a * LANES_V7X, (a + 1) * LANES_V7X)
        ca = ca_ref[:, a:a + 1]
        sa = sa_ref[:, a:a + 1]
        cos_blk = ca * cr - sa * sr
        nsin_blk = -(sa * cr + ca * sr)
        col = a * LANES_V7X + lax.broadcasted_iota(jnp.int32, (1, LANES_V7X), 1)
        if inverse:
            o_ref[:, blk] = cos_blk.astype(o_ref.dtype)
            o_ref[:, n_lanes + a * LANES_V7X:n_lanes + (a + 1) * LANES_V7X] = (
                jnp.where(col == 0, alt_row, nsin_blk).astype(o_ref.dtype))
        else:
            alt_col = sign * (1 - 2 * (col & 1)).astype(F32)
            o_ref[0, :, blk] = cos_blk.astype(o_ref.dtype)
            o_ref[1, :, blk] = jnp.where(row == 0, alt_col, nsin_blk).astype(o_ref.dtype)


def _dft_tables(length):
    half = length // 2
    n_a = half // LANES_V7X
    tm = _pick(half, 256)
    idx = jnp.arange(half, dtype=jnp.int32)[:, None]
    a_hi = jnp.arange(n_a, dtype=jnp.int32)[None, :] * LANES_V7X
    r_lo = jnp.arange(LANES_V7X, dtype=jnp.int32)[None, :]
    ang = lambda prod: (prod % (2 * length)).astype(F32) * (math.pi / length)

    def table(row_val, lane_hi, lane_lo, sign, inverse):
        out_shape = (half, 2 * half) if inverse else (2, half, half)
        out_block = (tm, 2 * half) if inverse else (2, tm, half)
        out_index = (lambda i: (i, 0)) if inverse else (lambda i: (0, i, 0))
        small = [f(ang(row_val * v)) for v in (lane_hi, lane_lo) for f in (jnp.cos, jnp.sin)]
        return pl.pallas_call(
            functools.partial(_trig_kernel, sign=sign, inverse=inverse),
            grid=(half // tm,),
            in_specs=[pl.BlockSpec((tm, n_a), lambda i: (i, 0)), pl.BlockSpec((tm, n_a), lambda i: (i, 0)),
                      pl.BlockSpec((tm, LANES_V7X), lambda i: (i, 0)),
                      pl.BlockSpec((tm, LANES_V7X), lambda i: (i, 0))],
            out_specs=pl.BlockSpec(out_block, out_index),
            out_shape=jax.ShapeDtypeStruct(out_shape, BF16),
            compiler_params=_cparams(1), name="dft_tables",
        )(*small)

    tables = []
    for par, sign in ((0, 1.0), (1, -1.0)):
        tables.append(table(idx, 2 * a_hi, 2 * r_lo + par, sign, False))
        tables.append(table(2 * idx + par, a_hi, r_lo, sign, True))
    fe, ie, fo, io = tables
    return fe, fo, ie, io


def _hyena_features(length, k_pad):
    t = jnp.linspace(0.0, 1.0, length, dtype=F32)[:, None]
    w = 2.0 * math.pi * jnp.arange(length, dtype=F32)[:, None] / length
    f = jnp.linspace(1e-4, HYENA_BANDS - 1, HYENA_BANDS, dtype=F32)[None, :]
    z = jnp.concatenate([t, jnp.cos(f * w), -jnp.sin(f * w)], axis=-1)
    return jnp.pad(z, ((0, 0), (0, k_pad - z.shape[1])))


def _blockdiag_dense(w):
    nb, bs = w.shape[-3], w.shape[-2]
    eye = jnp.eye(nb, dtype=w.dtype)
    dense = jnp.einsum('...nde,nm->...ndme', w, eye)
    return dense.reshape(w.shape[:-3] + (nb * bs, nb * bs))


def kernel(x, c, ctx, c_ctx, w_ada, b_ada, g_mix, g_ffn, w_in, attn_sink, hy_short_w, hy_short_b, hy_w1, hy_b1, hy_freq, hy_w2, hy_b2, hy_w3, hy_skip, lru_conv_w, lru_conv_b, lru_wa, lru_ba, lru_wx, lru_bx, lru_lambda, w_br_attn, w_br_hy, w_br_lru, w_out, router_w, router_b, exp_w_gate, exp_w_up, exp_w_down, final_g):
    batch, n_lat, d = x.shape
    n_ctx = ctx.shape[1]
    depth = w_ada.shape[0]
    t_lat, t_ctx = batch * n_lat, batch * n_ctx
    hw = hy_skip.shape[2]
    lw = lru_lambda.shape[2]
    fh = hy_w2.shape[1]
    aw = N_Q_HEADS * HEAD_DIM
    kvw = N_KV_HEADS * HEAD_DIM
    q_off = 2 * kvw + lw
    hy_off = q_off + aw
    lru_y_off = hy_off + 3 * hw
    gate_off = lru_y_off + lw
    assert batch + 1 <= MOD_ROWS and t_lat % n_ctx == 0 and n_lat % ATTN_BLOCK == 0
    tile_base = math.gcd(n_lat, t_ctx)
    tm = _pick(tile_base, 512)
    tm_big = _pick(tile_base, 1024)

    xu = jnp.concatenate([x.reshape(t_lat, d), ctx.reshape(t_ctx, d)], axis=0)
    cc = jnp.concatenate([c, c_ctx[None, :], jnp.zeros((MOD_ROWS - batch - 1, d), F32)], axis=0)
    mods_all = ada_tables(cc, w_ada, b_ada).reshape(depth, MOD_ROWS, N_MOD, 1, d)

    rope_tab = _rope_table(n_lat)
    band_bias = _band_bias(n_lat)
    deltas = jnp.abs(jnp.linspace(math.log(HYENA_DECAY_TARGET) / HYENA_FAST_DECAY,
                                  math.log(HYENA_DECAY_TARGET) / HYENA_SLOW_DECAY, hw, dtype=F32))[None, :]
    seqs = []
    for length, row0 in ((n_lat, 0), (n_ctx, t_lat)):
        seqs.append((length, row0, _dft_tables(length), _hyena_features(length, fh)))
    hy_w1p = jnp.pad(hy_w1, ((0, 0), (0, fh - hy_w1.shape[1]), (0, 0)))

    sp = softplus_neg(lru_lambda)
    lru_w = jnp.concatenate([_blockdiag_dense(lru_wa), _blockdiag_dense(lru_wx)], axis=-1).astype(BF16)
    lru_b = jnp.concatenate([lru_ba, lru_bx], axis=-1)[:, :, None, :]
    wa_b, wh_b, wl_b, wo_b = (w.astype(BF16) for w in (w_br_attn, w_br_hy, w_br_lru, w_out))
    rw = jnp.pad(router_w, ((0, 0), (0, LANES_V7X - router_w.shape[1])))
    rb = jnp.pad(router_b, (0, LANES_V7X - router_b.shape[0]))[None, :]
    h_zero = jnp.zeros((batch, 2, lw), F32)
    t_all = t_lat + t_ctx
    n_exp = exp_w_gate.shape[1]
    exp_tile = _pick(2 * t_all, 512)
    n_exp_tiles = (2 * t_all) // exp_tile + n_exp
    tm_cmb = _pick(tile_base, 256)

    for l in range(depth):
        mods = mods_all[l]
        h = normmod(xu, g_mix[l], mods, 0, n_lat, batch, tm)
        p = in_proj(h, w_in, l, tm_big, 0, gate_off, BF16)
        gates = in_proj(h, w_in, l, tm_big, gate_off, w_in.shape[2] - gate_off, BF16)

        o_att = latent_attention(p, attn_sink[l], rope_tab, band_bias, batch, n_lat, n_ctx, q_off)
        o_att = context_attention(p, attn_sink[l], o_att, batch, n_lat, n_ctx, q_off)

        xc_c = dwconv(p, lru_conv_w[l], lru_conv_b[l], batch, n_ctx, t_lat, 2 * kvw, lw)
        hf, hb, h_end = rg_lru(xc_c, lru_w, l, lru_b[l], sp[l], h_zero, batch, n_ctx, t_all, t_lat)
        xc_l = dwconv(p, lru_conv_w[l], lru_conv_b[l], batch, n_lat, 0, 2 * kvw, lw)
        hf, hb, _ = rg_lru(xc_l, lru_w, l, lru_b[l], sp[l], h_end, batch, n_lat, t_all, 0, prev=(hf, hb))

        o_hy = []
        for length, row0, (fe, fo, ie, io), feats in seqs:
            u = dwconv(p, hy_short_w[l], hy_short_b[l], batch, length, row0, hy_off, 3 * hw, parity_order=True)
            hid = hyena_hidden(feats, hy_w1p[l], hy_b1[l], hy_freq[l], hy_w2[l], hy_b2[l])
            taps, kh = hyena_taps(hid, hy_w3[l], deltas)
            kspec = hyena_spectrum(fe, fo, taps)
            pq = conv_forward(fe, fo, u, 0, kspec, kh, 0, batch, length, hw)
            z1 = conv_inverse(ie, io, pq, u, 0, u, hw, hy_skip[l], 0, batch, length, BF16)
            pq = conv_forward(fe, fo, z1, 0, kspec, kh, 1, batch, length, hw)
            z2 = conv_inverse(ie, io, pq, z1, 0, u, 2 * hw, hy_skip[l], 1, batch, length, BF16)
            o_hy.append(z2.reshape(batch, 2, length // 2, hw).transpose(0, 2, 1, 3).reshape(batch * length, hw))
        o_hy = jnp.concatenate(o_hy, axis=0)

        m = merge_branches(o_att, o_hy, p, gates, hf, hb, wa_b, wh_b, wl_b, l, lru_y_off, tm)
        xu = out_proj_residual(m, wo_b, l, xu, mods, 2, n_lat, batch, tm)

        fl, route, counts = router(xu, g_ffn[l], mods, 3, rw, rb, n_lat, batch, tm)
        dest, tile_expert, n_used, zero_start = dispatch_plan(route, counts, n_exp, exp_tile, n_exp_tiles)
        xs = moe_dispatch(fl, dest, zero_start, n_exp_tiles * exp_tile, exp_tile, tm, d // 2 // LANES_V7X)
        ys = moe_experts(tile_expert, n_used, xs, exp_w_gate, exp_w_up, exp_w_down, l, exp_tile)
        xu = moe_combine(dest, route, xu, mods, 5, ys, n_lat, batch, tm_cmb)

    return final_norm(xu, final_g, t_lat, tm).reshape(batch, n_lat, d)
```

```python
import functools
import math

import jax
import jax.numpy as jnp
import numpy as np
from jax import lax
from jax.experimental import pallas as pl
from jax.experimental.pallas import tpu as pltpu

F32 = jnp.float32
BF16 = jnp.bfloat16
HIGHEST = lax.Precision.HIGHEST

LANES_V7X = 128
VMEM_LIMIT_V7X = 56 * 1024 * 1024

EPS = 1e-6
GRID_W = 64
HEAD_DIM = 128
N_Q_HEADS = 8
N_KV_HEADS = 2
Q_PER_KV = N_Q_HEADS // N_KV_HEADS
WINDOW = 128
ATTN_BLOCK = 128
ROPE_THETA = 10000.0
HYENA_BANDS = 16
HYENA_DECAY_TARGET = 1e-2
HYENA_FAST_DECAY = 0.3
HYENA_SLOW_DECAY = 1.5
LRU_C = 8.0
N_EXPERTS = 16
N_GROUPS = 4
EXPERTS_PER_GROUP = N_EXPERTS // N_GROUPS
LOG2_E = math.log2(math.e)
N_MOD = 6
MOD_ROWS = 8


def _cparams(n_axes):
    return pltpu.CompilerParams(dimension_semantics=("arbitrary",) * n_axes,
                                vmem_limit_bytes=VMEM_LIMIT_V7X)


def _sigmoid(x):
    return 0.5 * (1.0 + jnp.tanh(0.5 * x))


def _pack_bf16_pairs(x):
    half = x.shape[1] // 2
    hi = lax.bitcast_convert_type(x[:, :half].astype(BF16).astype(F32), jnp.uint32)
    lo = lax.bitcast_convert_type(x[:, half:].astype(BF16).astype(F32), jnp.uint32)
    return hi | (lo >> 16)


def _unpack_bf16_pairs(w):
    hi = lax.bitcast_convert_type(w & jnp.uint32(0xFFFF0000), F32)
    lo = lax.bitcast_convert_type(w << 16, F32)
    return hi, lo


SUBLANES_V7X = 8


def _store_tile_rows(ref, row0, x):
    rows, width = x.shape
    s_per_row = width // LANES_V7X
    for s in range(s_per_row):
        ref[pl.ds(row0 + s, rows, stride=s_per_row), :] = x[:, s * LANES_V7X:(s + 1) * LANES_V7X]


def _load_tile_rows(ref, row0, rows, s_per_row):
    return jnp.concatenate([ref[pl.ds(row0 + s, rows, stride=s_per_row), :] for s in range(s_per_row)], axis=1)


def _pick(n, cap):
    t = cap
    while n % t:
        t //= 2
    return t


def _ada_kernel(c_ref, w_ref, b_ref, o_ref):
    c = c_ref[...]
    s = (c * jax.nn.sigmoid(c)).astype(BF16)
    o_ref[0] = jnp.dot(s, w_ref[0].astype(BF16), preferred_element_type=F32) + b_ref[0]


def ada_tables(cc, w_ada, b_ada):
    depth, d, n6 = w_ada.shape
    tn = _pick(n6, 1024)
    return pl.pallas_call(
        _ada_kernel,
        grid=(depth, n6 // tn),
        in_specs=[pl.BlockSpec((MOD_ROWS, d), lambda l, j: (0, 0)),
                  pl.BlockSpec((1, d, tn), lambda l, j: (l, 0, j)),
                  pl.BlockSpec((1, 1, tn), lambda l, j: (l, 0, j))],
        out_specs=pl.BlockSpec((1, MOD_ROWS, tn), lambda l, j: (l, 0, j)),
        out_shape=jax.ShapeDtypeStruct((depth, MOD_ROWS, n6), F32),
        compiler_params=_cparams(2), name="ada_tables",
    )(cc, w_ada, b_ada.reshape(depth, 1, n6))


def _mod_spec(which, tm, n_lat, batch, d):
    return pl.BlockSpec((1, 1, 1, d),
                        lambda i, *_: (jnp.minimum((i * tm) // n_lat, batch), which, 0, 0))


def _normmod(x, g, shift, scale):
    y = x * lax.rsqrt(jnp.mean(x * x, axis=-1, keepdims=True) + EPS) * g
    return y * (1.0 + scale) + shift


def _normmod_kernel(x_ref, g_ref, sh_ref, sc_ref, o_ref):
    o_ref[...] = _normmod(x_ref[...], g_ref[...], sh_ref[0, 0], sc_ref[0, 0]).astype(o_ref.dtype)


def normmod(x, g, mods, which_shift, n_lat, batch, tm):
    t, d = x.shape
    return pl.pallas_call(
        _normmod_kernel,
        grid=(t // tm,),
        in_specs=[pl.BlockSpec((tm, d), lambda i: (i, 0)),
                  pl.BlockSpec((1, d), lambda i: (0, 0)),
                  _mod_spec(which_shift, tm, n_lat, batch, d),
                  _mod_spec(which_shift + 1, tm, n_lat, batch, d)],
        out_specs=pl.BlockSpec((tm, d), lambda i: (i, 0)),
        out_shape=jax.ShapeDtypeStruct((t, d), BF16),
        compiler_params=_cparams(1), name="normmod",
    )(x, g.reshape(1, d), mods, mods)


def _final_norm_kernel(x_ref, g_ref, o_ref):
    x = x_ref[...]
    o_ref[...] = x * lax.rsqrt(jnp.mean(x * x, axis=-1, keepdims=True) + EPS) * g_ref[...]


def final_norm(x, g, rows, tm):
    d = x.shape[1]
    return pl.pallas_call(
        _final_norm_kernel,
        grid=(rows // tm,),
        in_specs=[pl.BlockSpec((tm, d), lambda i: (i, 0)), pl.BlockSpec((1, d), lambda i: (0, 0))],
        out_specs=pl.BlockSpec((tm, d), lambda i: (i, 0)),
        out_shape=jax.ShapeDtypeStruct((rows, d), F32),
        compiler_params=_cparams(1), name="final_norm",
    )(x, g.reshape(1, d))


def _proj_kernel(a_ref, w_ref, o_ref, wb_ref):
    @pl.when(pl.program_id(1) == 0)
    def _():
        wb_ref[...] = w_ref[0].astype(BF16)

    o_ref[...] = jnp.dot(a_ref[...], wb_ref[...], preferred_element_type=F32).astype(o_ref.dtype)


def in_proj(a, w, layer, tm, col0, n, out_dtype):
    t, k = a.shape
    tn = _pick(math.gcd(n, col0) if col0 else n, 1024)
    return pl.pallas_call(
        _proj_kernel,
        grid=(n // tn, t // tm),
        in_specs=[pl.BlockSpec((tm, k), lambda j, i: (i, 0)),
                  pl.BlockSpec((1, k, tn), lambda j, i: (layer, 0, col0 // tn + j))],
        out_specs=pl.BlockSpec((tm, tn), lambda j, i: (i, j)),
        out_shape=jax.ShapeDtypeStruct((t, n), out_dtype),
        scratch_shapes=[pltpu.VMEM((k, tn), BF16)],
        compiler_params=_cparams(2), name="in_proj",
    )(a, w)


def _rope(x, tab):
    return x * tab[:, :HEAD_DIM] + pltpu.roll(x, HEAD_DIM // 2, axis=1) * tab[:, HEAD_DIM:]


def _sink_column(sink_ref, h, rows):
    r = lax.broadcasted_iota(jnp.int32, (Q_PER_KV * rows, 1), 0)
    col = jnp.full((Q_PER_KV * rows, 1), sink_ref[h * Q_PER_KV], F32)
    for g in range(1, Q_PER_KV):
        col = jnp.where(r >= g * rows, sink_ref[h * Q_PER_KV + g], col)
    return col


def _nt(a, b):
    return lax.dot_general(a, b, (((1,), (1,)), ((), ())), preferred_element_type=F32)


def _band_bias(n_lat):
    blk = ATTN_BLOCK
    single = n_lat == blk
    qi = (jnp.arange(Q_PER_KV * blk) % blk)[:, None]
    kj = jnp.arange(3 * blk)[None, :]
    band = jnp.abs(qi + blk - kj) <= WINDOW
    kinds = []
    for no_prev, no_next in ((True, single), (False, False), (single, True)):
        ok = band & ((kj >= blk) | (not no_prev)) & ((kj < 2 * blk) | (not no_next))
        kinds.append(jnp.where(ok, 0.0, -jnp.inf))
    return jnp.stack(kinds).astype(F32)


def _lat_attn_kernel(sink_ref, q_ref, kvm_ref, kv0_ref, kvp_ref, kvc_ref, tm_ref, t0_ref, tp_ref, bias_ref,
                     o_ref):
    blk = ATTN_BLOCK
    kv_w = N_KV_HEADS * HEAD_DIM
    scale = HEAD_DIM ** -0.5
    tabs = (tm_ref[...], t0_ref[...], tp_ref[...])
    kvs = (kvm_ref, kv0_ref, kvp_ref)
    for h in range(N_KV_HEADS):
        ks = slice(h * HEAD_DIM, (h + 1) * HEAD_DIM)
        vs = slice(kv_w + h * HEAD_DIM, kv_w + (h + 1) * HEAD_DIM)
        k_loc = jnp.concatenate([_rope(kvs[n][:, ks].astype(F32), tabs[n]) for n in range(3)],
                                axis=0).astype(BF16)
        v_loc = jnp.concatenate([kvs[n][:, vs] for n in range(3)], axis=0).astype(BF16)
        k_ctx = kvc_ref[:, ks].astype(BF16)
        v_ctx = kvc_ref[:, vs].astype(BF16)
        q4 = jnp.concatenate(
            [_rope(q_ref[:, (h * Q_PER_KV + g) * HEAD_DIM:(h * Q_PER_KV + g + 1) * HEAD_DIM].astype(F32), tabs[1])
             * (scale * LOG2_E) for g in range(Q_PER_KV)], axis=0).astype(BF16)
        s_loc = _nt(q4, k_loc) + bias_ref[0]
        s_ctx = _nt(q4, k_ctx)
        sink = _sink_column(sink_ref, h, blk) * LOG2_E
        m = jnp.maximum(jnp.maximum(jnp.max(s_loc, axis=-1, keepdims=True),
                                    jnp.max(s_ctx, axis=-1, keepdims=True)), sink)
        p_loc = jnp.exp2(s_loc - m)
        p_ctx = jnp.exp2(s_ctx - m)
        den = (jnp.sum(p_loc, axis=-1, keepdims=True) + jnp.sum(p_ctx, axis=-1, keepdims=True)
               + jnp.exp2(sink - m))
        o = (jnp.dot(p_loc.astype(BF16), v_loc, preferred_element_type=F32)
             + jnp.dot(p_ctx.astype(BF16), v_ctx, preferred_element_type=F32)) / den
        for g in range(Q_PER_KV):
            hq = h * Q_PER_KV + g
            o_ref[:, hq * HEAD_DIM:(hq + 1) * HEAD_DIM] = o[g * blk:(g + 1) * blk].astype(o_ref.dtype)


def latent_attention(p, sink, rope_tab, band_bias, batch, n_lat, n_ctx, q_off):
    blk = ATTN_BLOCK
    nb = n_lat // blk
    aw = N_Q_HEADS * HEAD_DIM
    kvw = 2 * N_KV_HEADS * HEAD_DIM
    ctx_blk0 = (batch * n_lat) // n_ctx
    kv_spec = lambda off: pl.BlockSpec(
        (blk, kvw), lambda b, i: (b * nb + jnp.clip(i + off, 0, nb - 1), 0))
    tab_spec = lambda off: pl.BlockSpec(
        (blk, 2 * HEAD_DIM), lambda b, i: (jnp.clip(i + off, 0, nb - 1), 0))
    bias_kind = lambda b, i: (jnp.where(i == 0, 0, jnp.where(i == nb - 1, 2, 1)), 0, 0)
    return pl.pallas_call(
        _lat_attn_kernel,
        grid=(batch, nb),
        in_specs=[pl.BlockSpec(memory_space=pltpu.SMEM),
                  pl.BlockSpec((blk, aw), lambda b, i: (b * nb + i, q_off // aw)),
                  kv_spec(-1), kv_spec(0), kv_spec(1),
                  pl.BlockSpec((n_ctx, kvw), lambda b, i: (ctx_blk0 + b, 0)),
                  tab_spec(-1), tab_spec(0), tab_spec(1),
                  pl.BlockSpec((1, Q_PER_KV * blk, 3 * blk), bias_kind)],
        out_specs=pl.BlockSpec((blk, aw), lambda b, i: (b * nb + i, 0)),
        out_shape=jax.ShapeDtypeStruct((p.shape[0], aw), BF16),
        compiler_params=_cparams(2), name="latent_attention",
    )(sink, p, p, p, p, p, rope_tab, rope_tab, rope_tab, band_bias)


def _ctx_attn_kernel(sink_ref, q_ref, kv_ref, o_lat_ref, o_ref):
    del o_lat_ref
    rows = q_ref.shape[0]
    kv_w = N_KV_HEADS * HEAD_DIM
    scale = HEAD_DIM ** -0.5
    for h in range(N_KV_HEADS):
        k = kv_ref[:, h * HEAD_DIM:(h + 1) * HEAD_DIM].astype(BF16)
        v = kv_ref[:, kv_w + h * HEAD_DIM:kv_w + (h + 1) * HEAD_DIM].astype(BF16)
        q4 = jnp.concatenate(
            [q_ref[:, (h * Q_PER_KV + g) * HEAD_DIM:(h * Q_PER_KV + g + 1) * HEAD_DIM]
             for g in range(Q_PER_KV)], axis=0).astype(BF16)
        s = _nt(q4, k) * scale
        sink = _sink_column(sink_ref, h, rows)
        m = jnp.maximum(jnp.max(s, axis=-1, keepdims=True), sink)
        p = jnp.exp(s - m)
        den = jnp.sum(p, axis=-1, keepdims=True) + jnp.exp(sink - m)
        o = jnp.dot(p.astype(BF16), v, preferred_element_type=F32) / den
        for g in range(Q_PER_KV):
            hq = h * Q_PER_KV + g
            o_ref[:, hq * HEAD_DIM:(hq + 1) * HEAD_DIM] = o[g * rows:(g + 1) * rows].astype(o_ref.dtype)


def context_attention(p, sink, o_lat, batch, n_lat, n_ctx, q_off):
    aw = N_Q_HEADS * HEAD_DIM
    kvw = 2 * N_KV_HEADS * HEAD_DIM
    ctx_blk0 = (batch * n_lat) // n_ctx
    return pl.pallas_call(
        _ctx_attn_kernel,
        grid=(batch,),
        in_specs=[pl.BlockSpec(memory_space=pltpu.SMEM),
                  pl.BlockSpec((n_ctx, aw), lambda b: (ctx_blk0 + b, q_off // aw)),
                  pl.BlockSpec((n_ctx, kvw), lambda b: (ctx_blk0 + b, 0)),
                  pl.BlockSpec(memory_space=pl.ANY)],
        out_specs=pl.BlockSpec((n_ctx, aw), lambda b: (ctx_blk0 + b, 0)),
        out_shape=jax.ShapeDtypeStruct(o_lat.shape, o_lat.dtype),
        input_output_aliases={3: 0},
        compiler_params=_cparams(1), name="context_attention",
    )(sink, p, p, o_lat)


def _dwconv_kernel(u_ref, w_ref, b_ref, o_ref, *scratch, width):
    x = u_ref[...].astype(F32)
    length = x.shape[0]
    left = width // 2
    t = lax.broadcasted_iota(jnp.int32, (length, 1), 0)
    acc = jnp.broadcast_to(b_ref[...], x.shape)
    for j in range(width):
        s = j - left
        if s == 0:
            xs = x
        else:
            xs = pltpu.roll(x, (-s) % length, axis=0)
            xs = jnp.where((t + s >= 0) & (t + s < length), xs, 0.0)
        acc = acc + xs * w_ref[j:j + 1, :]
    if not scratch:
        o_ref[...] = acc
    else:
        half = length // 2
        scratch[0][...] = acc
        o_ref[:half, :] = scratch[0][pl.ds(0, half, stride=2), :].astype(o_ref.dtype)
        o_ref[half:, :] = scratch[0][pl.ds(1, half, stride=2), :].astype(o_ref.dtype)


def dwconv(p, w, b, n_seq, length, row0, col0, n_ch, parity_order=False):
    width = w.shape[0]
    cb = LANES_V7X if parity_order else _pick(n_ch, 256)
    return pl.pallas_call(
        functools.partial(_dwconv_kernel, width=width),
        grid=(n_seq, n_ch // cb),
        in_specs=[pl.BlockSpec((length, cb), lambda s, c: (row0 // length + s, col0 // cb + c)),
                  pl.BlockSpec((width, cb), lambda s, c: (0, c)),
                  pl.BlockSpec((1, cb), lambda s, c: (0, c))],
        out_specs=pl.BlockSpec((length, cb), lambda s, c: (s, c)),
        out_shape=jax.ShapeDtypeStruct((n_seq * length, n_ch), BF16 if parity_order else F32),
        scratch_shapes=[pltpu.VMEM((length, cb), F32)] if parity_order else [],
        compiler_params=_cparams(2), name="dwconv",
    )(p, w, b.reshape(1, n_ch))


def _lru_kernel(xf_ref, xb_ref, w_ref, bias_ref, sp_ref, h0_ref, *rest):
    hf_ref, hb_ref, hl_ref, af, bf, ab, bb, carry = rest[-8:]
    j = pl.program_id(1)
    width = xf_ref.shape[1]

    @pl.when(j == 0)
    def _():
        carry[...] = h0_ref[0]

    def gates(x, d):
        g = jnp.dot(x.astype(BF16), w_ref[0, d], preferred_element_type=F32) + bias_ref[d]
        r = _sigmoid(g[:, :width])
        gi = _sigmoid(g[:, width:])
        a = jnp.exp(-LRU_C * r * sp_ref[d])
        return a, jnp.sqrt(1.0 - a * a) * (gi * x)

    af[...], bf[...] = gates(xf_ref[...], 0)
    ab[...], bb[...] = gates(xb_ref[...], 1)
    tl = af.shape[0]

    def body(t, hs):
        hf, hb = hs
        hf = af[pl.ds(t, 1), :] * hf + bf[pl.ds(t, 1), :]
        hf_ref[pl.ds(t, 1), :] = hf
        tb = tl - 1 - t
        hb = ab[pl.ds(tb, 1), :] * hb + bb[pl.ds(tb, 1), :]
        hb_ref[pl.ds(tb, 1), :] = hb
        return hf, hb

    hf, hb = lax.fori_loop(0, tl, body, (carry[0:1, :], carry[1:2, :]), unroll=8)
    carry[0:1, :] = hf
    carry[1:2, :] = hb
    hl_ref[0] = carry[...]


def rg_lru(xc, w_gates, layer, b_gates, softplus_neg_lam, h0, n_seq, length, total_rows, row0, prev=None):
    width = xc.shape[1]
    tl = _pick(length, 512)
    nc = length // tl
    blk0 = row0 // tl
    in_specs = [pl.BlockSpec((tl, width), lambda s, j: (s * nc + j, 0)),
                pl.BlockSpec((tl, width), lambda s, j: (s * nc + nc - 1 - j, 0)),
                pl.BlockSpec((1, 2, width, 2 * width), lambda s, j: (layer, 0, 0, 0)),
                pl.BlockSpec((2, 1, 2 * width), lambda s, j: (0, 0, 0)),
                pl.BlockSpec((2, 1, width), lambda s, j: (0, 0, 0)),
                pl.BlockSpec((1, 2, width), lambda s, j: (s, 0, 0))]
    args = [xc, xc, w_gates, b_gates, softplus_neg_lam, h0]
    aliases = {}
    if prev is not None:
        in_specs += [pl.BlockSpec(memory_space=pl.ANY)] * 2
        aliases = {len(args): 0, len(args) + 1: 1}
        args += list(prev)
    return pl.pallas_call(
        _lru_kernel,
        grid=(n_seq, nc),
        in_specs=in_specs,
        out_specs=[pl.BlockSpec((tl, width), lambda s, j: (blk0 + s * nc + j, 0)),
                   pl.BlockSpec((tl, width), lambda s, j: (blk0 + s * nc + nc - 1 - j, 0)),
                   pl.BlockSpec((1, 2, width), lambda s, j: (s, 0, 0))],
        out_shape=[jax.ShapeDtypeStruct((total_rows, width), F32),
                   jax.ShapeDtypeStruct((total_rows, width), F32),
                   jax.ShapeDtypeStruct((n_seq, 2, width), F32)],
        scratch_shapes=[pltpu.VMEM((tl, width), F32)] * 4 + [pltpu.VMEM((2, width), F32)],
        input_output_aliases=aliases,
        compiler_params=_cparams(2), name="rg_lru",
    )(*args)


def _softplus_kernel(x_ref, o_ref):
    x = -x_ref[...]
    o_ref[...] = jnp.maximum(x, 0.0) + jnp.log(1.0 + jnp.exp(-jnp.abs(x)))


def softplus_neg(lam):
    depth, two, width = lam.shape
    x = lam.reshape(depth * two, width)
    out = pl.pallas_call(
        _softplus_kernel,
        out_shape=jax.ShapeDtypeStruct(x.shape, F32), name="softplus_neg",
    )(x)
    return out.reshape(depth, two, 1, width)


def _hy_hidden_kernel(z_ref, w1_ref, b1_ref, fr_ref, w2_ref, b2_ref, o_ref):
    fr = fr_ref[...]
    h = jnp.sin(fr * (jnp.dot(z_ref[...], w1_ref[...], precision=HIGHEST,
                              preferred_element_type=F32) + b1_ref[...]))
    o_ref[...] = jnp.sin(fr * (jnp.dot(h, w2_ref[...], precision=HIGHEST,
                                       preferred_element_type=F32) + b2_ref[...]))


def hyena_hidden(z, w1, b1, freq, w2, b2):
    length = z.shape[0]
    fh = w2.shape[0]
    return pl.pallas_call(
        _hy_hidden_kernel,
        out_shape=jax.ShapeDtypeStruct((length, fh), F32), name="hyena_hidden",
    )(z, w1, b1.reshape(1, fh), freq.reshape(1, fh), w2, b2.reshape(1, fh))


def _hy_taps_kernel(hid_ref, wf_ref, wb_ref, delta_ref, u_ref, kh_ref, tmp):
    hid = hid_ref[...]
    length = hid.shape[0]
    hf = jnp.dot(hid, wf_ref[...], precision=HIGHEST, preferred_element_type=F32)
    hb = jnp.dot(hid, wb_ref[...], precision=HIGHEST, preferred_element_type=F32)
    ti = lax.broadcasted_iota(jnp.int32, (length, 1), 0)
    win = jnp.exp(-(ti.astype(F32) * (1.0 / (length - 1))) * delta_ref[...])
    f = hf * win
    b = jnp.where(ti >= 1, hb * win, 0.0)
    sc = lax.rsqrt(jnp.sum(f * f + b * b, axis=0, keepdims=True) + EPS)
    u1 = (f + b) * sc
    u2 = (f - b) * sc
    quarter = jnp.where((ti & 1) == 0, (1 - (ti & 2)).astype(F32), 0.0)
    kh_ref[...] = jnp.sum(u1 * quarter, axis=0, keepdims=True)
    half = length // 2
    for k, u in enumerate((u1, u2)):
        tmp[...] = u
        u_ref[k, :half, :] = tmp[pl.ds(0, half, stride=2), :].astype(u_ref.dtype)
        u_ref[k, half:, :] = tmp[pl.ds(1, half, stride=2), :].astype(u_ref.dtype)


def hyena_taps(hid, w3, deltas):
    length, fh = hid.shape
    hw = deltas.shape[1]
    n_ord = w3.shape[1] // (2 * hw)
    cw = _pick(hw, 128)
    nc = hw // cw
    return pl.pallas_call(
        _hy_taps_kernel,
        grid=(n_ord, nc),
        in_specs=[pl.BlockSpec((length, fh), lambda o, c: (0, 0)),
                  pl.BlockSpec((fh, cw), lambda o, c: (0, o * 2 * nc + c)),
                  pl.BlockSpec((fh, cw), lambda o, c: (0, o * 2 * nc + nc + c)),
                  pl.BlockSpec((1, cw), lambda o, c: (0, c))],
        out_specs=[pl.BlockSpec((2, length, cw), lambda o, c: (0, 0, o * nc + c)),
                   pl.BlockSpec((1, cw), lambda o, c: (0, o * nc + c))],
        out_shape=[jax.ShapeDtypeStruct((2, length, n_ord * hw), BF16),
                   jax.ShapeDtypeStruct((1, n_ord * hw), F32)],
        scratch_shapes=[pltpu.VMEM((length, cw), F32)],
        compiler_params=_cparams(2), name="hyena_taps",
    )(hid, w3, w3, deltas)


def _dft_halves(fe_ref, fo_ref, xe_r, xo_r, xe_i, xo_i, first):
    e_r = jnp.dot(fe_ref[0], xe_r, preferred_element_type=F32)
    e_i = jnp.dot(fe_ref[1], xe_i, preferred_element_type=F32)
    o_r = jnp.dot(fo_ref[0], xo_r, preferred_element_type=F32)
    o_i = jnp.dot(fo_ref[1], xo_i, preferred_element_type=F32)
    return e_r + o_r, jnp.where(first, e_i, e_i + o_i), e_r - o_r, jnp.where(first, o_i, o_i - e_i)


def _first_row(rows, tile_index):
    return (lax.broadcasted_iota(jnp.int32, (rows, 1), 0) == 0) & (tile_index == 0)


def _spectrum_kernel(fe_ref, fo_ref, u_ref, o_ref):
    half = u_ref.shape[1] // 2
    first = _first_row(o_ref.shape[1], pl.program_id(1))
    planes = _dft_halves(fe_ref, fo_ref, u_ref[0, :half, :], u_ref[0, half:, :],
                         u_ref[1, :half, :], u_ref[1, half:, :], first)
    for k, plane in enumerate(planes):
        o_ref[k] = plane


def hyena_spectrum(fe, fo, u):
    _, length, n = u.shape
    half = length // 2
    tm = _pick(half, 256)
    tn = _pick(n, 512)
    return pl.pallas_call(
        _spectrum_kernel,
        grid=(n // tn, half // tm),
        in_specs=[pl.BlockSpec((2, tm, half), lambda c, i: (0, i, 0)),
                  pl.BlockSpec((2, tm, half), lambda c, i: (0, i, 0)),
                  pl.BlockSpec((2, length, tn), lambda c, i: (0, 0, c))],
        out_specs=pl.BlockSpec((4, tm, tn), lambda c, i: (0, i, c)),
        out_shape=jax.ShapeDtypeStruct((4, half, n), F32),
        compiler_params=_cparams(2), name="hyena_spectrum",
    )(fe, fo, u)


def _conv_fwd_kernel(fe_ref, fo_ref, z_ref, k_ref, kh_ref, y_ref):
    half = z_ref.shape[0] // 2
    first = _first_row(y_ref.shape[2], pl.program_id(1))
    ze = z_ref[:half, :]
    zo = z_ref[half:, :]
    a_r, a_i, b_r, b_i = _dft_halves(fe_ref, fo_ref, ze, zo, ze, zo, first)
    ka_r, ka_i, kb_r, kb_i = k_ref[0], k_ref[1], k_ref[2], k_ref[3]
    kh_r = kh_ref[...]
    ya_r = jnp.where(first, 0.5 * a_r * ka_r, a_r * ka_r - a_i * ka_i)
    yb_r = jnp.where(first, 0.5 * b_r * kb_r, b_r * kb_r - b_i * kb_i)
    ya_i = jnp.where(first, a_i * kh_r - b_i * kb_i, a_r * ka_i + a_i * ka_r)
    yb_i = jnp.where(first, a_i * kb_i + b_i * kh_r, b_r * kb_i + b_i * kb_r)
    y_ref[0, 0] = (ya_r + yb_r).astype(y_ref.dtype)
    y_ref[0, 1] = jnp.where(first, ya_i, ya_i - yb_i).astype(y_ref.dtype)
    y_ref[0, 2] = (ya_r - yb_r).astype(y_ref.dtype)
    y_ref[0, 3] = jnp.where(first, yb_i, ya_i + yb_i).astype(y_ref.dtype)


def conv_forward(fe, fo, z, col0, kspec, kh, order, n_seq, length, hw):
    half = length // 2
    tm = _pick(half, 512)
    return pl.pallas_call(
        _conv_fwd_kernel,
        grid=(n_seq, half // tm),
        in_specs=[pl.BlockSpec((2, tm, half), lambda s, i: (0, i, 0)),
                  pl.BlockSpec((2, tm, half), lambda s, i: (0, i, 0)),
                  pl.BlockSpec((length, hw), lambda s, i: (s, col0 // hw)),
                  pl.BlockSpec((4, tm, hw), lambda s, i: (0, i, order)),
                  pl.BlockSpec((1, hw), lambda s, i: (0, order))],
        out_specs=pl.BlockSpec((1, 4, tm, hw), lambda s, i: (s, 0, i, 0)),
        out_shape=jax.ShapeDtypeStruct((n_seq, 4, half, hw), BF16),
        compiler_params=_cparams(2), name="hyena_conv_fwd",
    )(fe, fo, z, kspec, kh)


def _conv_inv_kernel(ie_ref, io_ref, y_ref, z_ref, gate_ref, skip_ref, *rest, inv_len, token_order):
    skip = skip_ref[0]
    halves = []
    for par, inv_ref in enumerate((ie_ref, io_ref)):
        y = jnp.dot(inv_ref[...], y_ref[0, par], preferred_element_type=F32) * inv_len
        halves.append(gate_ref[0, par].astype(F32) * (y + skip * z_ref[0, par].astype(F32)))
    if not token_order:
        o_ref = rest[-1]
        for par in range(2):
            o_ref[0, par] = halves[par].astype(o_ref.dtype)
    else:
        o_ref, mix = rest[-2:]
        tm = halves[0].shape[0]
        for c0 in range(0, o_ref.shape[1], LANES_V7X):
            cols = slice(c0, c0 + LANES_V7X)
            for par in range(2):
                mix[pl.ds(par, tm, stride=2), :] = halves[par][:, cols]
            o_ref[:, cols] = mix[...].astype(o_ref.dtype)


def conv_inverse(ie, io, pq, z, z_col0, gates, gate_col0, skip, order, n_seq, length, out_dtype,
                 token_order=False, total_rows=None, row0=0, prev=None):
    hw = pq.shape[3]
    half = length // 2
    tm = _pick(half, 512)
    nt = half // tm
    par = lambda a: a.reshape(n_seq, 2, half, a.shape[1])
    in_specs = [pl.BlockSpec((tm, length), lambda s, i: (i, 0)),
                pl.BlockSpec((tm, length), lambda s, i: (i, 0)),
                pl.BlockSpec((1, 2, length, hw), lambda s, i: (s, 0, 0, 0)),
                pl.BlockSpec((1, 2, tm, hw), lambda s, i: (s, 0, i, z_col0 // hw)),
                pl.BlockSpec((1, 2, tm, hw), lambda s, i: (s, 0, i, gate_col0 // hw)),
                pl.BlockSpec((1, 1, hw), lambda s, i: (order, 0, 0))]
    args = [ie, io, pq.reshape(n_seq, 2, length, hw), par(z), par(gates), skip.reshape(skip.shape[0], 1, hw)]
    aliases, scratch = {}, []
    if token_order:
        blk0 = row0 // (2 * tm)
        out_spec = pl.BlockSpec((2 * tm, hw), lambda s, i: (blk0 + s * nt + i, 0))
        out_shape = jax.ShapeDtypeStruct((total_rows, hw), out_dtype)
        scratch = [pltpu.VMEM((2 * tm, LANES_V7X), F32)]
        if prev is not None:
            in_specs.append(pl.BlockSpec(memory_space=pl.ANY))
            aliases = {len(args): 0}
            args.append(prev)
    else:
        out_spec = pl.BlockSpec((1, 2, tm, hw), lambda s, i: (s, 0, i, 0))
        out_shape = jax.ShapeDtypeStruct((n_seq, 2, half, hw), out_dtype)
    out = pl.pallas_call(
        functools.partial(_conv_inv_kernel, inv_len=1.0 / length, token_order=token_order),
        grid=(n_seq, nt),
        in_specs=in_specs,
        out_specs=out_spec,
        out_shape=out_shape,
        scratch_shapes=scratch,
        input_output_aliases=aliases,
        compiler_params=_cparams(2), name="hyena_conv_inv",
    )(*args)
    return out if token_order else out.reshape(n_seq * length, hw)


def _merge_kernel(oa_ref, oh_ref, y_ref, hf_ref, hb_ref, g_ref, wa_ref, wh_ref, wl_ref, o_ref, *, tn):
    d = o_ref.shape[1]
    ol = (jax.nn.gelu(y_ref[...].astype(F32), approximate=True) * (hf_ref[...] + hb_ref[...])).astype(BF16)
    oa = oa_ref[...]
    oh = oh_ref[...]
    for c0 in range(0, d, tn):
        cols = slice(c0, c0 + tn)
        gate = lambda k: _sigmoid(g_ref[:, k * d + c0:k * d + c0 + tn].astype(F32))
        m = (gate(0) * jnp.dot(oa, wa_ref[0, :, cols], preferred_element_type=F32)
             + gate(1) * jnp.dot(oh, wh_ref[0, :, cols], preferred_element_type=F32)
             + gate(2) * jnp.dot(ol, wl_ref[0, :, cols], preferred_element_type=F32))
        o_ref[:, cols] = m.astype(o_ref.dtype)


def merge_branches(o_att, o_hy, p, gates, hf, hb, wa, wh, wl, layer, lru_y_off, tm):
    t, aw = o_att.shape
    hw = o_hy.shape[1]
    lw = hf.shape[1]
    d = wa.shape[2]
    w_spec = lambda rows: pl.BlockSpec((1, rows, d), lambda i: (layer, 0, 0))
    return pl.pallas_call(
        functools.partial(_merge_kernel, tn=_pick(d, 512)),
        grid=(t // tm,),
        in_specs=[pl.BlockSpec((tm, aw), lambda i: (i, 0)),
                  pl.BlockSpec((tm, hw), lambda i: (i, 0)),
                  pl.BlockSpec((tm, lw), lambda i: (i, lru_y_off // lw)),
                  pl.BlockSpec((tm, lw), lambda i: (i, 0)),
                  pl.BlockSpec((tm, lw), lambda i: (i, 0)),
                  pl.BlockSpec((tm, 3 * d), lambda i: (i, 0)),
                  w_spec(aw), w_spec(hw), w_spec(lw)],
        out_specs=pl.BlockSpec((tm, d), lambda i: (i, 0)),
        out_shape=jax.ShapeDtypeStruct((t, d), BF16),
        compiler_params=_cparams(1), name="merge_branches",
    )(o_att, o_hy, p, hf, hb, gates, wa, wh, wl)


def _out_proj_kernel(m_ref, w_ref, x_ref, gate_ref, o_ref):
    o_ref[...] = x_ref[...] + gate_ref[0, 0] * jnp.dot(m_ref[...], w_ref[0],
                                                       preferred_element_type=F32)


def out_proj_residual(m, w, layer, x, mods, which_gate, n_lat, batch, tm):
    t, d = x.shape
    return pl.pallas_call(
        _out_proj_kernel,
        grid=(t // tm,),
        in_specs=[pl.BlockSpec((tm, d), lambda i: (i, 0)),
                  pl.BlockSpec((1, d, d), lambda i: (layer, 0, 0)),
                  pl.BlockSpec((tm, d), lambda i: (i, 0)),
                  _mod_spec(which_gate, tm, n_lat, batch, d)],
        out_specs=pl.BlockSpec((tm, d), lambda i: (i, 0)),
        out_shape=jax.ShapeDtypeStruct((t, d), F32),
        compiler_params=_cparams(1), name="out_proj_residual",
    )(m, w, x, mods)


def _router_kernel(x_ref, g_ref, sh_ref, sc_ref, rw_ref, rb_ref, h_ref, meta_ref, cnt_ref):
    h = _normmod(x_ref[...], g_ref[...], sh_ref[0, 0], sc_ref[0, 0])
    _store_tile_rows(h_ref, 0, _pack_bf16_pairs(h))
    w = rw_ref[...]
    h_hi, w_hi = h.astype(BF16), w.astype(BF16)
    h_lo = (h - h_hi.astype(F32)).astype(BF16)
    w_lo = (w - w_hi.astype(F32)).astype(BF16)
    logits = (jnp.dot(h_hi, w_hi, preferred_element_type=F32) + jnp.dot(h_lo, w_hi, preferred_element_type=F32)
              + jnp.dot(h_hi, w_lo, preferred_element_type=F32)) + rb_ref[...]
    lane = lax.broadcasted_iota(jnp.int32, logits.shape, 1)
    valid = lane < N_EXPERTS
    logits = jnp.where(valid, logits, -jnp.inf)
    ex = jnp.exp(logits - jnp.max(logits, axis=-1, keepdims=True))
    scores = ex / jnp.sum(ex, axis=-1, keepdims=True)
    grp = lax.shift_right_logical(lane, int(math.log2(EXPERTS_PER_GROUP)))
    best = jnp.zeros((logits.shape[0], 1), jnp.int32)
    best_max = jnp.max(jnp.where((grp == 0) & valid, scores, -1.0), axis=-1, keepdims=True)
    for g in range(1, N_GROUPS):
        gm = jnp.max(jnp.where((grp == g) & valid, scores, -1.0), axis=-1, keepdims=True)
        better = gm > best_max
        best = jnp.where(better, g, best)
        best_max = jnp.where(better, gm, best_max)
    s1 = jnp.where((grp == best) & valid, scores, -1.0)
    m1 = jnp.max(s1, axis=-1, keepdims=True)
    i1 = jnp.min(jnp.where(s1 == m1, lane, LANES_V7X), axis=-1, keepdims=True)
    s2 = jnp.where(lane == i1, -1.0, s1)
    m2 = jnp.max(s2, axis=-1, keepdims=True)
    i2 = jnp.min(jnp.where(s2 == m2, lane, LANES_V7X), axis=-1, keepdims=True)
    tot = m1 + m2
    @pl.when(pl.program_id(0) == 0)
    def _():
        cnt_ref[...] = jnp.zeros_like(cnt_ref)

    tm = logits.shape[0]
    onehot = ((lane == i1) | (lane == i2)).astype(BF16)
    lower = (lax.broadcasted_iota(jnp.int32, (tm, tm), 0)
             > lax.broadcasted_iota(jnp.int32, (tm, tm), 1)).astype(BF16)
    before = jnp.dot(lower, onehot, preferred_element_type=F32) + cnt_ref[...]
    r1 = jnp.sum(jnp.where(lane == i1, before, 0.0), axis=-1, keepdims=True)
    r2 = jnp.sum(jnp.where(lane == i2, before, 0.0), axis=-1, keepdims=True)
    cnt_ref[...] += jnp.sum(onehot.astype(F32), axis=0, keepdims=True)
    cols = (i1.astype(F32), i2.astype(F32), r1, r2, m1 / tot, m2 / tot)
    meta = jnp.zeros(logits.shape, F32)
    for k, col in enumerate(cols):
        meta = jnp.where(lane == k, col, meta)
    meta_ref[...] = meta


ROUTE_E, ROUTE_RANK, ROUTE_W = 0, 2, 4


def router(x, g, mods, which_shift, rw, rb, n_lat, batch, tm):
    t, d = x.shape
    spr = d // 2 // LANES_V7X
    return pl.pallas_call(
        _router_kernel,
        grid=(t // tm,),
        in_specs=[pl.BlockSpec((tm, d), lambda i: (i, 0)),
                  pl.BlockSpec((1, d), lambda i: (0, 0)),
                  _mod_spec(which_shift, tm, n_lat, batch, d),
                  _mod_spec(which_shift + 1, tm, n_lat, batch, d),
                  pl.BlockSpec((d, LANES_V7X), lambda i: (0, 0)),
                  pl.BlockSpec((1, LANES_V7X), lambda i: (0, 0))],
        out_specs=[pl.BlockSpec((tm * spr, LANES_V7X), lambda i: (i, 0)),
                   pl.BlockSpec((tm, LANES_V7X), lambda i: (i, 0)),
                   pl.BlockSpec((1, LANES_V7X), lambda i: (0, 0))],
        out_shape=[jax.ShapeDtypeStruct((t * spr, LANES_V7X), jnp.uint32),
                   jax.ShapeDtypeStruct((t, LANES_V7X), F32),
                   jax.ShapeDtypeStruct((1, LANES_V7X), F32)],
        compiler_params=_cparams(1), name="router",
    )(x, g.reshape(1, d), mods, mods, rw, rb)


def dispatch_plan(meta, counts, n_exp, tile, n_tiles):
    cnt = counts[0, :n_exp].astype(jnp.int32)
    tiles_e = (cnt + tile - 1) // tile
    tile_end = jnp.cumsum(tiles_e)
    row_start = (tile_end - tiles_e) * tile
    experts = meta[:, ROUTE_E:ROUTE_E + 2].astype(jnp.int32)
    ranks = meta[:, ROUTE_RANK:ROUTE_RANK + 2].astype(jnp.int32)
    dest = (row_start[experts] + ranks).reshape(-1)
    tile_expert = jnp.minimum(jnp.sum(jnp.arange(n_tiles)[:, None] >= tile_end[None, :], axis=1),
                              n_exp - 1).astype(jnp.int32)
    zero_start = jnp.minimum(row_start + cnt, (n_tiles - 1) * tile).astype(jnp.int32)
    return dest, tile_expert, tile_end[-1:].astype(jnp.int32), zero_start


def _dispatch_kernel(dest_ref, zero_ref, h_ref, xs_ref, zbuf, sem, zsem, *, spr):
    tm = h_ref.shape[0] // spr
    base = pl.program_id(0) * (2 * tm)
    token = lambda ref, t: ref.at[pl.ds(pl.multiple_of(t * spr, spr), spr)]

    @pl.when(pl.program_id(0) == 0)
    def _():
        zbuf[...] = jnp.zeros_like(zbuf)

        def clear(e, carry):
            start = pl.multiple_of(zero_ref[e] * spr, spr)
            copy = pltpu.make_async_copy(zbuf, xs_ref.at[pl.ds(start, zbuf.shape[0])], zsem)
            copy.start()
            copy.wait()
            return carry

        lax.fori_loop(0, zero_ref.shape[0], clear, 0)

    def issue(r, carry):
        for k in range(2):
            pltpu.make_async_copy(token(h_ref, r), token(xs_ref, dest_ref[base + 2 * r + k]), sem).start()
        return carry

    lax.fori_loop(0, tm, issue, 0, unroll=8)
    for k in range(2):
        pltpu.make_async_copy(h_ref, xs_ref.at[pl.ds(0, tm * spr)], sem).wait()


def moe_dispatch(h, dest, zero_start, n_rows, tile, tm, spr):
    assert spr == SUBLANES_V7X
    t = h.shape[0] // spr
    return pl.pallas_call(
        functools.partial(_dispatch_kernel, spr=spr),
        grid_spec=pltpu.PrefetchScalarGridSpec(
            num_scalar_prefetch=2, grid=(t // tm,),
            in_specs=[pl.BlockSpec((tm * spr, LANES_V7X), lambda i, dest, zs: (i, 0))],
            out_specs=pl.BlockSpec(memory_space=pl.ANY),
            scratch_shapes=[pltpu.VMEM((tile * spr, LANES_V7X), h.dtype),
                            pltpu.SemaphoreType.DMA, pltpu.SemaphoreType.DMA]),
        out_shape=jax.ShapeDtypeStruct((n_rows * spr, LANES_V7X), h.dtype),
        compiler_params=_cparams(1), name="moe_dispatch",
    )(dest, zero_start, h)


def _experts_kernel(te_ref, nu_ref, xs_ref, wg_ref, wu_ref, wd_ref, ys_ref, wgb, wub, wdb, *, spr):
    j = pl.program_id(0)

    @pl.when(j < nu_ref[0])
    def _():
        @pl.when((j == 0) | (te_ref[j] != te_ref[jnp.maximum(j - 1, 0)]))
        def _():
            wgb[...] = wg_ref[0, 0].astype(BF16)
            wub[...] = wu_ref[0, 0].astype(BF16)
            wdb[...] = wd_ref[0, 0].astype(BF16)

        tile = xs_ref.shape[0] // spr
        x_hi, x_lo = (v.astype(BF16) for v in _unpack_bf16_pairs(_load_tile_rows(xs_ref, 0, tile, spr)))
        half = x_hi.shape[1]

        def proj(w):
            return (jnp.dot(x_hi, w[:half, :], preferred_element_type=F32)
                    + jnp.dot(x_lo, w[half:, :], preferred_element_type=F32))

        gt = proj(wgb)
        act = (gt * _sigmoid(gt)) * proj(wub)
        _store_tile_rows(ys_ref, 0, _pack_bf16_pairs(
            jnp.dot(act.astype(BF16), wdb[...], preferred_element_type=F32)))


def moe_experts(tile_expert, n_used, xs, wg, wu, wd, layer, tile):
    d, de = wg.shape[2], wg.shape[3]
    spr = d // 2 // LANES_V7X
    n_tiles = xs.shape[0] // (tile * spr)
    used = lambda j, nu: jnp.minimum(j, nu[0] - 1)
    w_spec = lambda shape: pl.BlockSpec((1, 1) + shape, lambda j, te, nu: (layer, te[used(j, nu)], 0, 0))
    return pl.pallas_call(
        functools.partial(_experts_kernel, spr=spr),
        grid_spec=pltpu.PrefetchScalarGridSpec(
            num_scalar_prefetch=2, grid=(n_tiles,),
            in_specs=[pl.BlockSpec((tile * spr, LANES_V7X), lambda j, te, nu: (used(j, nu), 0)),
                      w_spec((d, de)), w_spec((d, de)), w_spec((de, d))],
            out_specs=pl.BlockSpec((tile * spr, LANES_V7X), lambda j, te, nu: (used(j, nu), 0)),
            scratch_shapes=[pltpu.VMEM((d, de), BF16), pltpu.VMEM((d, de), BF16),
                            pltpu.VMEM((de, d), BF16)]),
        out_shape=jax.ShapeDtypeStruct(xs.shape, xs.dtype),
        compiler_params=_cparams(1), name="moe_experts",
    )(tile_expert, n_used, xs, wg, wu, wd)


def _combine_kernel(dest_ref, meta_ref, x_ref, gate_ref, ys_ref, o_ref, buf, sem, *, spr):
    i = pl.program_id(0)
    tm = x_ref.shape[0]
    part = lambda slot, k: pl.multiple_of((slot * 2 + k) * (tm * spr), spr)

    def gather(tile, slot):
        base = tile * (2 * tm)

        def issue(r, carry):
            for k in range(2):
                row = dest_ref[base + 2 * r + k]
                pltpu.make_async_copy(ys_ref.at[pl.ds(pl.multiple_of(row * spr, spr), spr)],
                                      buf.at[pl.ds(part(slot, k) + r * spr, spr)], sem.at[slot]).start()
            return carry

        lax.fori_loop(0, tm, issue, 0, unroll=8)

    @pl.when(i == 0)
    def _():
        gather(0, 0)

    @pl.when(i + 1 < pl.num_programs(0))
    def _():
        gather(i + 1, (i + 1) % 2)

    slot = i % 2
    for k in range(2):
        pltpu.make_async_copy(ys_ref.at[pl.ds(0, tm * spr)], buf.at[pl.ds(part(slot, k), tm * spr)],
                              sem.at[slot]).wait()
    meta = meta_ref[...]
    w1 = meta[:, ROUTE_W:ROUTE_W + 1]
    w2 = meta[:, ROUTE_W + 1:ROUTE_W + 2]
    y1_hi, y1_lo = _unpack_bf16_pairs(_load_tile_rows(buf, part(slot, 0), tm, spr))
    y2_hi, y2_lo = _unpack_bf16_pairs(_load_tile_rows(buf, part(slot, 1), tm, spr))
    half = y1_hi.shape[1]
    gate = gate_ref[0, 0]
    o_ref[:, :half] = x_ref[:, :half] + gate[:, :half] * (w1 * y1_hi + w2 * y2_hi)
    o_ref[:, half:] = x_ref[:, half:] + gate[:, half:] * (w1 * y1_lo + w2 * y2_lo)


def moe_combine(dest, meta, x, mods, which_gate, ys, n_lat, batch, tm):
    t, d = x.shape
    spr = d // 2 // LANES_V7X
    return pl.pallas_call(
        functools.partial(_combine_kernel, spr=spr),
        grid_spec=pltpu.PrefetchScalarGridSpec(
            num_scalar_prefetch=1, grid=(t // tm,),
            in_specs=[pl.BlockSpec((tm, LANES_V7X), lambda i, dest: (i, 0)),
                      pl.BlockSpec((tm, d), lambda i, dest: (i, 0)),
                      _mod_spec(which_gate, tm, n_lat, batch, d),
                      pl.BlockSpec(memory_space=pl.ANY)],
            out_specs=pl.BlockSpec((tm, d), lambda i, dest: (i, 0)),
            scratch_shapes=[pltpu.VMEM((2 * 2 * tm * spr, LANES_V7X), ys.dtype),
                            pltpu.SemaphoreType.DMA((2,))]),
        out_shape=jax.ShapeDtypeStruct((t, d), F32),
        compiler_params=_cparams(1), name="moe_combine",
    )(dest, meta, x, mods, ys)


def _rope_table(n_tokens):
    rows = n_tokens // GRID_W
    row = jnp.repeat(jnp.arange(rows), GRID_W).astype(F32)
    col = jnp.tile(jnp.arange(GRID_W), rows).astype(F32)
    pairs = HEAD_DIM // 4
    inv = ROPE_THETA ** (-jnp.arange(pairs, dtype=F32) / pairs)
    ang = jnp.concatenate([row[:, None] * inv, col[:, None] * inv], axis=-1)
    cos, sin = jnp.cos(ang), jnp.sin(ang)
    return jnp.concatenate([cos, cos, -sin, sin], axis=-1)


def _trig_kernel(ca_ref, sa_ref, cr_ref, sr_ref, o_ref, *, sign, inverse):
    tm = cr_ref.shape[0]
    n_lanes = ca_ref.shape[1] * LANES_V7X
    cr = cr_ref[...]
    sr = sr_ref[...]
    row = pl.program_id(0) * tm + lax.broadcasted_iota(jnp.int32, (tm, 1), 0)
    alt_row = sign * (1 - 2 * (row & 1)).astype(F32)
    for a in range(n_lanes // LANES_V7X):
        blk = slice(a * LANES_V7X, (a + 1) * LANES_V7X)
        ca = ca_ref[:, a:a + 1]
        sa = sa_ref[:, a:a + 1]
        cos_blk = ca * cr - sa * sr
        nsin_blk = -(sa * cr + ca * sr)
        col = a * LANES_V7X + lax.broadcasted_iota(jnp.int32, (1, LANES_V7X), 1)
        if inverse:
            o_ref[:, blk] = cos_blk.astype(o_ref.dtype)
            o_ref[:, n_lanes + a * LANES_V7X:n_lanes + (a + 1) * LANES_V7X] = (
                jnp.where(col == 0, alt_row, nsin_blk).astype(o_ref.dtype))
        else:
            alt_col = sign * (1 - 2 * (col & 1)).astype(F32)
            o_ref[0, :, blk] = cos_blk.astype(o_ref.dtype)
            o_ref[1, :, blk] = jnp.where(row == 0, alt_col, nsin_blk).astype(o_ref.dtype)


def _dft_tables(length):
    half = length // 2
    n_a = half // LANES_V7X
    tm = _pick(half, 256)
    idx = jnp.arange(half, dtype=jnp.int32)[:, None]
    a_hi = jnp.arange(n_a, dtype=jnp.int32)[None, :] * LANES_V7X
    r_lo = jnp.arange(LANES_V7X, dtype=jnp.int32)[None, :]
    ang = lambda prod: (prod % (2 * length)).astype(F32) * (math.pi / length)

    def table(row_val, lane_hi, lane_lo, sign, inverse):
        out_shape = (half, 2 * half) if inverse else (2, half, half)
        out_block = (tm, 2 * half) if inverse else (2, tm, half)
        out_index = (lambda i: (i, 0)) if inverse else (lambda i: (0, i, 0))
        small = [f(ang(row_val * v)) for v in (lane_hi, lane_lo) for f in (jnp.cos, jnp.sin)]
        return pl.pallas_call(
            functools.partial(_trig_kernel, sign=sign, inverse=inverse),
            grid=(half // tm,),
            in_specs=[pl.BlockSpec((tm, n_a), lambda i: (i, 0)), pl.BlockSpec((tm, n_a), lambda i: (i, 0)),
                      pl.BlockSpec((tm, LANES_V7X), lambda i: (i, 0)),
                      pl.BlockSpec((tm, LANES_V7X), lambda i: (i, 0))],
            out_specs=pl.BlockSpec(out_block, out_index),
            out_shape=jax.ShapeDtypeStruct(out_shape, BF16),
            compiler_params=_cparams(1), name="dft_tables",
        )(*small)

    tables = []
    for par, sign in ((0, 1.0), (1, -1.0)):
        tables.append(table(idx, 2 * a_hi, 2 * r_lo + par, sign, False))
        tables.append(table(2 * idx + par, a_hi, r_lo, sign, True))
    fe, ie, fo, io = tables
    return fe, fo, ie, io


def _hyena_features(length, k_pad):
    t = jnp.linspace(0.0, 1.0, length, dtype=F32)[:, None]
    w = 2.0 * math.pi * jnp.arange(length, dtype=F32)[:, None] / length
    f = jnp.linspace(1e-4, HYENA_BANDS - 1, HYENA_BANDS, dtype=F32)[None, :]
    z = jnp.concatenate([t, jnp.cos(f * w), -jnp.sin(f * w)], axis=-1)
    return jnp.pad(z, ((0, 0), (0, k_pad - z.shape[1])))


def _blockdiag_dense(w):
    nb, bs = w.shape[-3], w.shape[-2]
    eye = jnp.eye(nb, dtype=w.dtype)
    dense = jnp.einsum('...nde,nm->...ndme', w, eye)
    return dense.reshape(w.shape[:-3] + (nb * bs, nb * bs))


def kernel(x, c, ctx, c_ctx, w_ada, b_ada, g_mix, g_ffn, w_in, attn_sink, hy_short_w, hy_short_b, hy_w1, hy_b1, hy_freq, hy_w2, hy_b2, hy_w3, hy_skip, lru_conv_w, lru_conv_b, lru_wa, lru_ba, lru_wx, lru_bx, lru_lambda, w_br_attn, w_br_hy, w_br_lru, w_out, router_w, router_b, exp_w_gate, exp_w_up, exp_w_down, final_g):
    batch, n_lat, d = x.shape
    n_ctx = ctx.shape[1]
    depth = w_ada.shape[0]
    t_lat, t_ctx = batch * n_lat, batch * n_ctx
    hw = hy_skip.shape[2]
    lw = lru_lambda.shape[2]
    fh = hy_w2.shape[1]
    aw = N_Q_HEADS * HEAD_DIM
    kvw = N_KV_HEADS * HEAD_DIM
    q_off = 2 * kvw + lw
    hy_off = q_off + aw
    lru_y_off = hy_off + 3 * hw
    gate_off = lru_y_off + lw
    assert batch + 1 <= MOD_ROWS and t_lat % n_ctx == 0 and n_lat % ATTN_BLOCK == 0
    tile_base = math.gcd(n_lat, t_ctx)
    tm = _pick(tile_base, 512)
    tm_big = _pick(tile_base, 1024)

    xu = jnp.concatenate([x.reshape(t_lat, d), ctx.reshape(t_ctx, d)], axis=0)
    cc = jnp.concatenate([c, c_ctx[None, :], jnp.zeros((MOD_ROWS - batch - 1, d), F32)], axis=0)
    mods_all = ada_tables(cc, w_ada, b_ada).reshape(depth, MOD_ROWS, N_MOD, 1, d)

    rope_tab = _rope_table(n_lat)
    band_bias = _band_bias(n_lat)
    deltas = jnp.abs(jnp.linspace(math.log(HYENA_DECAY_TARGET) / HYENA_FAST_DECAY,
                                  math.log(HYENA_DECAY_TARGET) / HYENA_SLOW_DECAY, hw, dtype=F32))[None, :]
    seqs = []
    for length, row0 in ((n_lat, 0), (n_ctx, t_lat)):
        seqs.append((length, row0, _dft_tables(length), _hyena_features(length, fh)))
    hy_w1p = jnp.pad(hy_w1, ((0, 0), (0, fh - hy_w1.shape[1]), (0, 0)))

    sp = softplus_neg(lru_lambda)
    lru_w = jnp.concatenate([_blockdiag_dense(lru_wa), _blockdiag_dense(lru_wx)], axis=-1).astype(BF16)
    lru_b = jnp.concatenate([lru_ba, lru_bx], axis=-1)[:, :, None, :]
    wa_b, wh_b, wl_b, wo_b = (w.astype(BF16) for w in (w_br_attn, w_br_hy, w_br_lru, w_out))
    rw = jnp.pad(router_w, ((0, 0), (0, LANES_V7X - router_w.shape[1])))
    rb = jnp.pad(router_b, (0, LANES_V7X - router_b.shape[0]))[None, :]
    h_zero = jnp.zeros((batch, 2, lw), F32)
    t_all = t_lat + t_ctx
    n_exp = exp_w_gate.shape[1]
    exp_tile = _pick(2 * t_all, 512)
    n_exp_tiles = (2 * t_all) // exp_tile + n_exp
    tm_cmb = _pick(tile_base, 256)

    for l in range(depth):
        mods = mods_all[l]
        h = normmod(xu, g_mix[l], mods, 0, n_lat, batch, tm)
        p = in_proj(h, w_in, l, tm_big, 0, gate_off, BF16)
        gates = in_proj(h, w_in, l, tm_big, gate_off, w_in.shape[2] - gate_off, BF16)

        o_att = latent_attention(p, attn_sink[l], rope_tab, band_bias, batch, n_lat, n_ctx, q_off)
        o_att = context_attention(p, attn_sink[l], o_att, batch, n_lat, n_ctx, q_off)

        xc_c = dwconv(p, lru_conv_w[l], lru_conv_b[l], batch, n_ctx, t_lat, 2 * kvw, lw)
        hf, hb, h_end = rg_lru(xc_c, lru_w, l, lru_b[l], sp[l], h_zero, batch, n_ctx, t_all, t_lat)
        xc_l = dwconv(p, lru_conv_w[l], lru_conv_b[l], batch, n_lat, 0, 2 * kvw, lw)
        hf, hb, _ = rg_lru(xc_l, lru_w, l, lru_b[l], sp[l], h_end, batch, n_lat, t_all, 0, prev=(hf, hb))

        o_hy = None
        for length, row0, (fe, fo, ie, io), feats in seqs:
            u = dwconv(p, hy_short_w[l], hy_short_b[l], batch, length, row0, hy_off, 3 * hw, parity_order=True)
            hid = hyena_hidden(feats, hy_w1p[l], hy_b1[l], hy_freq[l], hy_w2[l], hy_b2[l])
            taps, kh = hyena_taps(hid, hy_w3[l], deltas)
            kspec = hyena_spectrum(fe, fo, taps)
            pq = conv_forward(fe, fo, u, 0, kspec, kh, 0, batch, length, hw)
            z1 = conv_inverse(ie, io, pq, u, 0, u, hw, hy_skip[l], 0, batch, length, BF16)
            pq = conv_forward(fe, fo, z1, 0, kspec, kh, 1, batch, length, hw)
            o_hy = conv_inverse(ie, io, pq, z1, 0, u, 2 * hw, hy_skip[l], 1, batch, length, BF16,
                                token_order=True, total_rows=t_all, row0=row0, prev=o_hy)

        m = merge_branches(o_att, o_hy, p, gates, hf, hb, wa_b, wh_b, wl_b, l, lru_y_off, tm)
        xu = out_proj_residual(m, wo_b, l, xu, mods, 2, n_lat, batch, tm)

        fl, route, counts = router(xu, g_ffn[l], mods, 3, rw, rb, n_lat, batch, tm)
        dest, tile_expert, n_used, zero_start = dispatch_plan(route, counts, n_exp, exp_tile, n_exp_tiles)
        xs = moe_dispatch(fl, dest, zero_start, n_exp_tiles * exp_tile, exp_tile, tm, d // 2 // LANES_V7X)
        ys = moe_experts(tile_expert, n_used, xs, exp_w_gate, exp_w_up, exp_w_down, l, exp_tile)
        xu = moe_combine(dest, route, xu, mods, 5, ys, n_lat, batch, tm_cmb)

    return final_norm(xu, final_g, t_lat, tm).reshape(batch, n_lat, d)
```

```python
import functools
import math

import jax
import jax.numpy as jnp
import numpy as np
from jax import lax
from jax.experimental import pallas as pl
from jax.experimental.pallas import tpu as pltpu

F32 = jnp.float32
BF16 = jnp.bfloat16
HIGHEST = lax.Precision.HIGHEST

LANES_V7X = 128
VMEM_LIMIT_V7X = 56 * 1024 * 1024

EPS = 1e-6
GRID_W = 64
HEAD_DIM = 128
N_Q_HEADS = 8
N_KV_HEADS = 2
Q_PER_KV = N_Q_HEADS // N_KV_HEADS
WINDOW = 128
ATTN_BLOCK = 128
ROPE_THETA = 10000.0
HYENA_BANDS = 16
HYENA_DECAY_TARGET = 1e-2
HYENA_FAST_DECAY = 0.3
HYENA_SLOW_DECAY = 1.5
LRU_C = 8.0
N_EXPERTS = 16
N_GROUPS = 4
EXPERTS_PER_GROUP = N_EXPERTS // N_GROUPS
LOG2_E = math.log2(math.e)
N_MOD = 6
MOD_ROWS = 8


def _cparams(n_axes):
    return pltpu.CompilerParams(dimension_semantics=("arbitrary",) * n_axes,
                                vmem_limit_bytes=VMEM_LIMIT_V7X)


def _sigmoid(x):
    return 0.5 * (1.0 + jnp.tanh(0.5 * x))


def _pack_bf16_pairs(x):
    half = x.shape[1] // 2
    hi = lax.bitcast_convert_type(x[:, :half].astype(BF16).astype(F32), jnp.uint32)
    lo = lax.bitcast_convert_type(x[:, half:].astype(BF16).astype(F32), jnp.uint32)
    return hi | (lo >> 16)


def _unpack_bf16_pairs(w):
    hi = lax.bitcast_convert_type(w & jnp.uint32(0xFFFF0000), F32)
    lo = lax.bitcast_convert_type(w << 16, F32)
    return hi, lo


SUBLANES_V7X = 8


def _store_tile_rows(ref, row0, x):
    rows, width = x.shape
    s_per_row = width // LANES_V7X
    for s in range(s_per_row):
        ref[pl.ds(row0 + s, rows, stride=s_per_row), :] = x[:, s * LANES_V7X:(s + 1) * LANES_V7X]


def _load_tile_rows(ref, row0, rows, s_per_row):
    return jnp.concatenate([ref[pl.ds(row0 + s, rows, stride=s_per_row), :] for s in range(s_per_row)], axis=1)


def _pick(n, cap):
    t = cap
    while n % t:
        t //= 2
    return t


def _ada_kernel(c_ref, w_ref, b_ref, o_ref):
    c = c_ref[...]
    s = (c * jax.nn.sigmoid(c)).astype(BF16)
    o_ref[0] = jnp.dot(s, w_ref[0].astype(BF16), preferred_element_type=F32) + b_ref[0]


def ada_tables(cc, w_ada, b_ada):
    depth, d, n6 = w_ada.shape
    tn = _pick(n6, 1024)
    return pl.pallas_call(
        _ada_kernel,
        grid=(depth, n6 // tn),
        in_specs=[pl.BlockSpec((MOD_ROWS, d), lambda l, j: (0, 0)),
                  pl.BlockSpec((1, d, tn), lambda l, j: (l, 0, j)),
                  pl.BlockSpec((1, 1, tn), lambda l, j: (l, 0, j))],
        out_specs=pl.BlockSpec((1, MOD_ROWS, tn), lambda l, j: (l, 0, j)),
        out_shape=jax.ShapeDtypeStruct((depth, MOD_ROWS, n6), F32),
        compiler_params=_cparams(2), name="ada_tables",
    )(cc, w_ada, b_ada.reshape(depth, 1, n6))


def _mod_spec(which, tm, n_lat, batch, d):
    return pl.BlockSpec((1, 1, 1, d),
                        lambda i, *_: (jnp.minimum((i * tm) // n_lat, batch), which, 0, 0))


def _normmod(x, g, shift, scale):
    y = x * lax.rsqrt(jnp.mean(x * x, axis=-1, keepdims=True) + EPS) * g
    return y * (1.0 + scale) + shift


def _normmod_kernel(x_ref, g_ref, sh_ref, sc_ref, o_ref):
    o_ref[...] = _normmod(x_ref[...], g_ref[...], sh_ref[0, 0], sc_ref[0, 0]).astype(o_ref.dtype)


def normmod(x, g, mods, which_shift, n_lat, batch, tm):
    t, d = x.shape
    return pl.pallas_call(
        _normmod_kernel,
        grid=(t // tm,),
        in_specs=[pl.BlockSpec((tm, d), lambda i: (i, 0)),
                  pl.BlockSpec((1, d), lambda i: (0, 0)),
                  _mod_spec(which_shift, tm, n_lat, batch, d),
                  _mod_spec(which_shift + 1, tm, n_lat, batch, d)],
        out_specs=pl.BlockSpec((tm, d), lambda i: (i, 0)),
        out_shape=jax.ShapeDtypeStruct((t, d), BF16),
        compiler_params=_cparams(1), name="normmod",
    )(x, g.reshape(1, d), mods, mods)


def _final_norm_kernel(x_ref, g_ref, o_ref):
    x = x_ref[...]
    o_ref[...] = x * lax.rsqrt(jnp.mean(x * x, axis=-1, keepdims=True) + EPS) * g_ref[...]


def final_norm(x, g, rows, tm):
    d = x.shape[1]
    return pl.pallas_call(
        _final_norm_kernel,
        grid=(rows // tm,),
        in_specs=[pl.BlockSpec((tm, d), lambda i: (i, 0)), pl.BlockSpec((1, d), lambda i: (0, 0))],
        out_specs=pl.BlockSpec((tm, d), lambda i: (i, 0)),
        out_shape=jax.ShapeDtypeStruct((rows, d), F32),
        compiler_params=_cparams(1), name="final_norm",
    )(x, g.reshape(1, d))


def _proj_kernel(a_ref, w_ref, o_ref, wb_ref):
    @pl.when(pl.program_id(1) == 0)
    def _():
        wb_ref[...] = w_ref[0].astype(BF16)

    o_ref[...] = jnp.dot(a_ref[...], wb_ref[...], preferred_element_type=F32).astype(o_ref.dtype)


def in_proj(a, w, layer, tm, col0, n, out_dtype, rows=None):
    t, k = a.shape
    tn = _pick(math.gcd(n, col0) if col0 else n, 1024)
    return pl.pallas_call(
        _proj_kernel,
        grid=(n // tn, (t if rows is None else rows) // tm),
        in_specs=[pl.BlockSpec((tm, k), lambda j, i: (i, 0)),
                  pl.BlockSpec((1, k, tn), lambda j, i: (layer, 0, col0 // tn + j))],
        out_specs=pl.BlockSpec((tm, tn), lambda j, i: (i, j)),
        out_shape=jax.ShapeDtypeStruct((t, n), out_dtype),
        scratch_shapes=[pltpu.VMEM((k, tn), BF16)],
        compiler_params=_cparams(2), name="in_proj",
    )(a, w)


def _rope(x, tab):
    return x * tab[:, :HEAD_DIM] + pltpu.roll(x, HEAD_DIM // 2, axis=1) * tab[:, HEAD_DIM:]


def _sink_column(sink_ref, h, rows):
    r = lax.broadcasted_iota(jnp.int32, (Q_PER_KV * rows, 1), 0)
    col = jnp.full((Q_PER_KV * rows, 1), sink_ref[h * Q_PER_KV], F32)
    for g in range(1, Q_PER_KV):
        col = jnp.where(r >= g * rows, sink_ref[h * Q_PER_KV + g], col)
    return col


def _nt(a, b):
    return lax.dot_general(a, b, (((1,), (1,)), ((), ())), preferred_element_type=F32)


def _band_bias(n_lat):
    blk = ATTN_BLOCK
    single = n_lat == blk
    qi = (jnp.arange(Q_PER_KV * blk) % blk)[:, None]
    kj = jnp.arange(3 * blk)[None, :]
    band = jnp.abs(qi + blk - kj) <= WINDOW
    kinds = []
    for no_prev, no_next in ((True, single), (False, False), (single, True)):
        ok = band & ((kj >= blk) | (not no_prev)) & ((kj < 2 * blk) | (not no_next))
        kinds.append(jnp.where(ok, 0.0, -jnp.inf))
    return jnp.stack(kinds).astype(F32)


def _lat_attn_kernel(sink_ref, q_ref, kvm_ref, kv0_ref, kvp_ref, kvc_ref, tm_ref, t0_ref, tp_ref, bias_ref,
                     o_ref):
    blk = ATTN_BLOCK
    kv_w = N_KV_HEADS * HEAD_DIM
    scale = HEAD_DIM ** -0.5
    tabs = (tm_ref[...], t0_ref[...], tp_ref[...])
    kvs = (kvm_ref, kv0_ref, kvp_ref)
    for h in range(N_KV_HEADS):
        ks = slice(h * HEAD_DIM, (h + 1) * HEAD_DIM)
        vs = slice(kv_w + h * HEAD_DIM, kv_w + (h + 1) * HEAD_DIM)
        k_loc = jnp.concatenate([_rope(kvs[n][:, ks].astype(F32), tabs[n]) for n in range(3)],
                                axis=0).astype(BF16)
        v_loc = jnp.concatenate([kvs[n][:, vs] for n in range(3)], axis=0).astype(BF16)
        k_ctx = kvc_ref[:, ks].astype(BF16)
        v_ctx = kvc_ref[:, vs].astype(BF16)
        q4 = jnp.concatenate(
            [_rope(q_ref[:, (h * Q_PER_KV + g) * HEAD_DIM:(h * Q_PER_KV + g + 1) * HEAD_DIM].astype(F32), tabs[1])
             * (scale * LOG2_E) for g in range(Q_PER_KV)], axis=0).astype(BF16)
        s_loc = _nt(q4, k_loc) + bias_ref[0]
        s_ctx = _nt(q4, k_ctx)
        sink = _sink_column(sink_ref, h, blk) * LOG2_E
        m = jnp.maximum(jnp.maximum(jnp.max(s_loc, axis=-1, keepdims=True),
                                    jnp.max(s_ctx, axis=-1, keepdims=True)), sink)
        p_loc = jnp.exp2(s_loc - m)
        p_ctx = jnp.exp2(s_ctx - m)
        den = (jnp.sum(p_loc, axis=-1, keepdims=True) + jnp.sum(p_ctx, axis=-1, keepdims=True)
               + jnp.exp2(sink - m))
        o = (jnp.dot(p_loc.astype(BF16), v_loc, preferred_element_type=F32)
             + jnp.dot(p_ctx.astype(BF16), v_ctx, preferred_element_type=F32)) / den
        for g in range(Q_PER_KV):
            hq = h * Q_PER_KV + g
            o_ref[:, hq * HEAD_DIM:(hq + 1) * HEAD_DIM] = o[g * blk:(g + 1) * blk].astype(o_ref.dtype)


def latent_attention(p, sink, rope_tab, band_bias, batch, n_lat, n_ctx, q_off):
    blk = ATTN_BLOCK
    nb = n_lat // blk
    aw = N_Q_HEADS * HEAD_DIM
    kvw = 2 * N_KV_HEADS * HEAD_DIM
    ctx_blk0 = (batch * n_lat) // n_ctx
    kv_spec = lambda off: pl.BlockSpec(
        (blk, kvw), lambda b, i: (b * nb + jnp.clip(i + off, 0, nb - 1), 0))
    tab_spec = lambda off: pl.BlockSpec(
        (blk, 2 * HEAD_DIM), lambda b, i: (jnp.clip(i + off, 0, nb - 1), 0))
    bias_kind = lambda b, i: (jnp.where(i == 0, 0, jnp.where(i == nb - 1, 2, 1)), 0, 0)
    return pl.pallas_call(
        _lat_attn_kernel,
        grid=(batch, nb),
        in_specs=[pl.BlockSpec(memory_space=pltpu.SMEM),
                  pl.BlockSpec((blk, aw), lambda b, i: (b * nb + i, q_off // aw)),
                  kv_spec(-1), kv_spec(0), kv_spec(1),
                  pl.BlockSpec((n_ctx, kvw), lambda b, i: (ctx_blk0 + b, 0)),
                  tab_spec(-1), tab_spec(0), tab_spec(1),
                  pl.BlockSpec((1, Q_PER_KV * blk, 3 * blk), bias_kind)],
        out_specs=pl.BlockSpec((blk, aw), lambda b, i: (b * nb + i, 0)),
        out_shape=jax.ShapeDtypeStruct((p.shape[0], aw), BF16),
        compiler_params=_cparams(2), name="latent_attention",
    )(sink, p, p, p, p, p, rope_tab, rope_tab, rope_tab, band_bias)


def _ctx_attn_kernel(sink_ref, q_ref, kv_ref, o_lat_ref, o_ref):
    del o_lat_ref
    rows = q_ref.shape[0]
    kv_w = N_KV_HEADS * HEAD_DIM
    scale = HEAD_DIM ** -0.5
    for h in range(N_KV_HEADS):
        k = kv_ref[:, h * HEAD_DIM:(h + 1) * HEAD_DIM].astype(BF16)
        v = kv_ref[:, kv_w + h * HEAD_DIM:kv_w + (h + 1) * HEAD_DIM].astype(BF16)
        q4 = jnp.concatenate(
            [q_ref[:, (h * Q_PER_KV + g) * HEAD_DIM:(h * Q_PER_KV + g + 1) * HEAD_DIM]
             for g in range(Q_PER_KV)], axis=0).astype(BF16)
        s = _nt(q4, k) * scale
        sink = _sink_column(sink_ref, h, rows)
        m = jnp.maximum(jnp.max(s, axis=-1, keepdims=True), sink)
        p = jnp.exp(s - m)
        den = jnp.sum(p, axis=-1, keepdims=True) + jnp.exp(sink - m)
        o = jnp.dot(p.astype(BF16), v, preferred_element_type=F32) / den
        for g in range(Q_PER_KV):
            hq = h * Q_PER_KV + g
            o_ref[:, hq * HEAD_DIM:(hq + 1) * HEAD_DIM] = o[g * rows:(g + 1) * rows].astype(o_ref.dtype)


def context_attention(p, sink, o_lat, batch, n_lat, n_ctx, q_off):
    aw = N_Q_HEADS * HEAD_DIM
    kvw = 2 * N_KV_HEADS * HEAD_DIM
    ctx_blk0 = (batch * n_lat) // n_ctx
    return pl.pallas_call(
        _ctx_attn_kernel,
        grid=(batch,),
        in_specs=[pl.BlockSpec(memory_space=pltpu.SMEM),
                  pl.BlockSpec((n_ctx, aw), lambda b: (ctx_blk0 + b, q_off // aw)),
                  pl.BlockSpec((n_ctx, kvw), lambda b: (ctx_blk0 + b, 0)),
                  pl.BlockSpec(memory_space=pl.ANY)],
        out_specs=pl.BlockSpec((n_ctx, aw), lambda b: (ctx_blk0 + b, 0)),
        out_shape=jax.ShapeDtypeStruct(o_lat.shape, o_lat.dtype),
        input_output_aliases={3: 0},
        compiler_params=_cparams(1), name="context_attention",
    )(sink, p, p, o_lat)


def _dwconv_kernel(u_ref, w_ref, b_ref, o_ref, *scratch, width):
    x = u_ref[...].astype(F32)
    length = x.shape[0]
    left = width // 2
    t = lax.broadcasted_iota(jnp.int32, (length, 1), 0)
    acc = jnp.broadcast_to(b_ref[...], x.shape)
    for j in range(width):
        s = j - left
        if s == 0:
            xs = x
        else:
            xs = pltpu.roll(x, (-s) % length, axis=0)
            xs = jnp.where((t + s >= 0) & (t + s < length), xs, 0.0)
        acc = acc + xs * w_ref[j:j + 1, :]
    if not scratch:
        o_ref[...] = acc
    else:
        half = length // 2
        scratch[0][...] = acc
        o_ref[:half, :] = scratch[0][pl.ds(0, half, stride=2), :].astype(o_ref.dtype)
        o_ref[half:, :] = scratch[0][pl.ds(1, half, stride=2), :].astype(o_ref.dtype)


def dwconv(p, w, b, n_seq, length, row0, col0, n_ch, parity_order=False):
    width = w.shape[0]
    cb = LANES_V7X if parity_order else _pick(n_ch, 256)
    return pl.pallas_call(
        functools.partial(_dwconv_kernel, width=width),
        grid=(n_seq, n_ch // cb),
        in_specs=[pl.BlockSpec((length, cb), lambda s, c: (row0 // length + s, col0 // cb + c)),
                  pl.BlockSpec((width, cb), lambda s, c: (0, c)),
                  pl.BlockSpec((1, cb), lambda s, c: (0, c))],
        out_specs=pl.BlockSpec((length, cb), lambda s, c: (s, c)),
        out_shape=jax.ShapeDtypeStruct((n_seq * length, n_ch), BF16 if parity_order else F32),
        scratch_shapes=[pltpu.VMEM((length, cb), F32)] if parity_order else [],
        compiler_params=_cparams(2), name="dwconv",
    )(p, w, b.reshape(1, n_ch))


def _lru_kernel(xf_ref, xb_ref, w_ref, bias_ref, sp_ref, h0_ref, *rest):
    hf_ref, hb_ref, hl_ref, af, bf, ab, bb, carry = rest[-8:]
    j = pl.program_id(1)
    width = xf_ref.shape[1]

    @pl.when(j == 0)
    def _():
        carry[...] = h0_ref[0]

    def gates(x, d):
        g = jnp.dot(x.astype(BF16), w_ref[0, d], preferred_element_type=F32) + bias_ref[d]
        a = jnp.exp2((-0.5 * LRU_C * LOG2_E * sp_ref[d]) * (1.0 + jnp.tanh(0.5 * g[:, :width])))
        gx = (1.0 + jnp.tanh(0.5 * g[:, width:])) * (0.5 * x)
        return a, jnp.sqrt(1.0 - a * a) * gx

    af[...], bf[...] = gates(xf_ref[...], 0)
    ab[...], bb[...] = gates(xb_ref[...], 1)
    tl = af.shape[0]

    def body(t, hs):
        hf, hb = hs
        hf = af[pl.ds(t, 1), :] * hf + bf[pl.ds(t, 1), :]
        hf_ref[pl.ds(t, 1), :] = hf
        tb = tl - 1 - t
        hb = ab[pl.ds(tb, 1), :] * hb + bb[pl.ds(tb, 1), :]
        hb_ref[pl.ds(tb, 1), :] = hb
        return hf, hb

    hf, hb = lax.fori_loop(0, tl, body, (carry[0:1, :], carry[1:2, :]), unroll=8)
    carry[0:1, :] = hf
    carry[1:2, :] = hb
    hl_ref[0] = carry[...]


def rg_lru(xc, w_gates, layer, b_gates, softplus_neg_lam, h0, n_seq, length, total_rows, row0, prev=None):
    width = xc.shape[1]
    tl = _pick(length, 512)
    nc = length // tl
    blk0 = row0 // tl
    in_specs = [pl.BlockSpec((tl, width), lambda s, j: (s * nc + j, 0)),
                pl.BlockSpec((tl, width), lambda s, j: (s * nc + nc - 1 - j, 0)),
                pl.BlockSpec((1, 2, width, 2 * width), lambda s, j: (layer, 0, 0, 0)),
                pl.BlockSpec((2, 1, 2 * width), lambda s, j: (0, 0, 0)),
                pl.BlockSpec((2, 1, width), lambda s, j: (0, 0, 0)),
                pl.BlockSpec((1, 2, width), lambda s, j: (s, 0, 0))]
    args = [xc, xc, w_gates, b_gates, softplus_neg_lam, h0]
    aliases = {}
    if prev is not None:
        in_specs += [pl.BlockSpec(memory_space=pl.ANY)] * 2
        aliases = {len(args): 0, len(args) + 1: 1}
        args += list(prev)
    return pl.pallas_call(
        _lru_kernel,
        grid=(n_seq, nc),
        in_specs=in_specs,
        out_specs=[pl.BlockSpec((tl, width), lambda s, j: (blk0 + s * nc + j, 0)),
                   pl.BlockSpec((tl, width), lambda s, j: (blk0 + s * nc + nc - 1 - j, 0)),
                   pl.BlockSpec((1, 2, width), lambda s, j: (s, 0, 0))],
        out_shape=[jax.ShapeDtypeStruct((total_rows, width), F32),
                   jax.ShapeDtypeStruct((total_rows, width), F32),
                   jax.ShapeDtypeStruct((n_seq, 2, width), F32)],
        scratch_shapes=[pltpu.VMEM((tl, width), F32)] * 4 + [pltpu.VMEM((2, width), F32)],
        input_output_aliases=aliases,
        compiler_params=_cparams(2), name="rg_lru",
    )(*args)


def _softplus_kernel(x_ref, o_ref):
    x = -x_ref[...]
    o_ref[...] = jnp.maximum(x, 0.0) + jnp.log(1.0 + jnp.exp(-jnp.abs(x)))


def softplus_neg(lam):
    depth, two, width = lam.shape
    x = lam.reshape(depth * two, width)
    out = pl.pallas_call(
        _softplus_kernel,
        out_shape=jax.ShapeDtypeStruct(x.shape, F32), name="softplus_neg",
    )(x)
    return out.reshape(depth, two, 1, width)


def _hy_hidden_kernel(z_ref, w1_ref, b1_ref, fr_ref, w2_ref, b2_ref, o_ref):
    fr = fr_ref[...]
    h = jnp.sin(fr * (jnp.dot(z_ref[...], w1_ref[...], precision=HIGHEST,
                              preferred_element_type=F32) + b1_ref[...]))
    o_ref[...] = jnp.sin(fr * (jnp.dot(h, w2_ref[...], precision=HIGHEST,
                                       preferred_element_type=F32) + b2_ref[...]))


def hyena_hidden(z, w1, b1, freq, w2, b2):
    length = z.shape[0]
    fh = w2.shape[0]
    return pl.pallas_call(
        _hy_hidden_kernel,
        out_shape=jax.ShapeDtypeStruct((length, fh), F32), name="hyena_hidden",
    )(z, w1, b1.reshape(1, fh), freq.reshape(1, fh), w2, b2.reshape(1, fh))


def _hy_taps_kernel(hid_ref, wf_ref, wb_ref, delta_ref, u_ref, kh_ref, tmp):
    hid = hid_ref[...]
    length = hid.shape[0]
    hf = jnp.dot(hid, wf_ref[...], precision=HIGHEST, preferred_element_type=F32)
    hb = jnp.dot(hid, wb_ref[...], precision=HIGHEST, preferred_element_type=F32)
    ti = lax.broadcasted_iota(jnp.int32, (length, 1), 0)
    win = jnp.exp(-(ti.astype(F32) * (1.0 / (length - 1))) * delta_ref[...])
    f = hf * win
    b = jnp.where(ti >= 1, hb * win, 0.0)
    sc = lax.rsqrt(jnp.sum(f * f + b * b, axis=0, keepdims=True) + EPS)
    u1 = (f + b) * sc
    u2 = (f - b) * sc
    quarter = jnp.where((ti & 1) == 0, (1 - (ti & 2)).astype(F32), 0.0)
    kh_ref[...] = jnp.sum(u1 * quarter, axis=0, keepdims=True)
    half = length // 2
    for k, u in enumerate((u1, u2)):
        tmp[...] = u
        u_ref[k, :half, :] = tmp[pl.ds(0, half, stride=2), :].astype(u_ref.dtype)
        u_ref[k, half:, :] = tmp[pl.ds(1, half, stride=2), :].astype(u_ref.dtype)


def hyena_taps(hid, w3, deltas):
    length, fh = hid.shape
    hw = deltas.shape[1]
    n_ord = w3.shape[1] // (2 * hw)
    cw = _pick(hw, 128)
    nc = hw // cw
    return pl.pallas_call(
        _hy_taps_kernel,
        grid=(n_ord, nc),
        in_specs=[pl.BlockSpec((length, fh), lambda o, c: (0, 0)),
                  pl.BlockSpec((fh, cw), lambda o, c: (0, o * 2 * nc + c)),
                  pl.BlockSpec((fh, cw), lambda o, c: (0, o * 2 * nc + nc + c)),
                  pl.BlockSpec((1, cw), lambda o, c: (0, c))],
        out_specs=[pl.BlockSpec((2, length, cw), lambda o, c: (0, 0, o * nc + c)),
                   pl.BlockSpec((1, cw), lambda o, c: (0, o * nc + c))],
        out_shape=[jax.ShapeDtypeStruct((2, length, n_ord * hw), BF16),
                   jax.ShapeDtypeStruct((1, n_ord * hw), F32)],
        scratch_shapes=[pltpu.VMEM((length, cw), F32)],
        compiler_params=_cparams(2), name="hyena_taps",
    )(hid, w3, w3, deltas)


def _dft_halves(fe_ref, fo_ref, xe_r, xo_r, xe_i, xo_i, first):
    e_r = jnp.dot(fe_ref[0], xe_r, preferred_element_type=F32)
    e_i = jnp.dot(fe_ref[1], xe_i, preferred_element_type=F32)
    o_r = jnp.dot(fo_ref[0], xo_r, preferred_element_type=F32)
    o_i = jnp.dot(fo_ref[1], xo_i, preferred_element_type=F32)
    return e_r + o_r, jnp.where(first, e_i, e_i + o_i), e_r - o_r, jnp.where(first, o_i, o_i - e_i)


def _first_row(rows, tile_index):
    return (lax.broadcasted_iota(jnp.int32, (rows, 1), 0) == 0) & (tile_index == 0)


def _spectrum_kernel(fe_ref, fo_ref, u_ref, o_ref):
    half = u_ref.shape[1] // 2
    first = _first_row(o_ref.shape[1], pl.program_id(1))
    planes = _dft_halves(fe_ref, fo_ref, u_ref[0, :half, :], u_ref[0, half:, :],
                         u_ref[1, :half, :], u_ref[1, half:, :], first)
    for k, plane in enumerate(planes):
        o_ref[k] = plane


def hyena_spectrum(fe, fo, u):
    _, length, n = u.shape
    half = length // 2
    tm = _pick(half, 256)
    tn = _pick(n, 512)
    return pl.pallas_call(
        _spectrum_kernel,
        grid=(n // tn, half // tm),
        in_specs=[pl.BlockSpec((2, tm, half), lambda c, i: (0, i, 0)),
                  pl.BlockSpec((2, tm, half), lambda c, i: (0, i, 0)),
                  pl.BlockSpec((2, length, tn), lambda c, i: (0, 0, c))],
        out_specs=pl.BlockSpec((4, tm, tn), lambda c, i: (0, i, c)),
        out_shape=jax.ShapeDtypeStruct((4, half, n), F32),
        compiler_params=_cparams(2), name="hyena_spectrum",
    )(fe, fo, u)


def _conv_fwd_kernel(fe_ref, fo_ref, z_ref, k_ref, kh_ref, y_ref):
    half = z_ref.shape[0] // 2
    first = _first_row(y_ref.shape[2], pl.program_id(1))
    ze = z_ref[:half, :]
    zo = z_ref[half:, :]
    a_r, a_i, b_r, b_i = _dft_halves(fe_ref, fo_ref, ze, zo, ze, zo, first)
    ka_r, ka_i, kb_r, kb_i = k_ref[0], k_ref[1], k_ref[2], k_ref[3]
    kh_r = kh_ref[...]
    ya_r = jnp.where(first, 0.5 * a_r * ka_r, a_r * ka_r - a_i * ka_i)
    yb_r = jnp.where(first, 0.5 * b_r * kb_r, b_r * kb_r - b_i * kb_i)
    ya_i = jnp.where(first, a_i * kh_r - b_i * kb_i, a_r * ka_i + a_i * ka_r)
    yb_i = jnp.where(first, a_i * kb_i + b_i * kh_r, b_r * kb_i + b_i * kb_r)
    y_ref[0, 0] = (ya_r + yb_r).astype(y_ref.dtype)
    y_ref[0, 1] = jnp.where(first, ya_i, ya_i - yb_i).astype(y_ref.dtype)
    y_ref[0, 2] = (ya_r - yb_r).astype(y_ref.dtype)
    y_ref[0, 3] = jnp.where(first, yb_i, ya_i + yb_i).astype(y_ref.dtype)


def conv_forward(fe, fo, z, col0, kspec, kh, order, n_seq, length, hw):
    half = length // 2
    tm = _pick(half, 512)
    return pl.pallas_call(
        _conv_fwd_kernel,
        grid=(n_seq, half // tm),
        in_specs=[pl.BlockSpec((2, tm, half), lambda s, i: (0, i, 0)),
                  pl.BlockSpec((2, tm, half), lambda s, i: (0, i, 0)),
                  pl.BlockSpec((length, hw), lambda s, i: (s, col0 // hw)),
                  pl.BlockSpec((4, tm, hw), lambda s, i: (0, i, order)),
                  pl.BlockSpec((1, hw), lambda s, i: (0, order))],
        out_specs=pl.BlockSpec((1, 4, tm, hw), lambda s, i: (s, 0, i, 0)),
        out_shape=jax.ShapeDtypeStruct((n_seq, 4, half, hw), BF16),
        compiler_params=_cparams(2), name="hyena_conv_fwd",
    )(fe, fo, z, kspec, kh)


def _conv_inv_kernel(ie_ref, io_ref, y_ref, z_ref, gate_ref, skip_ref, *rest, inv_len, token_order):
    skip = skip_ref[0]
    halves = []
    for par, inv_ref in enumerate((ie_ref, io_ref)):
        y = jnp.dot(inv_ref[...], y_ref[0, par], preferred_element_type=F32) * inv_len
        halves.append(gate_ref[0, par].astype(F32) * (y + skip * z_ref[0, par].astype(F32)))
    if not token_order:
        o_ref = rest[-1]
        for par in range(2):
            o_ref[0, par] = halves[par].astype(o_ref.dtype)
    else:
        o_ref, mix = rest[-2:]
        tm = halves[0].shape[0]
        for c0 in range(0, o_ref.shape[1], LANES_V7X):
            cols = slice(c0, c0 + LANES_V7X)
            for par in range(2):
                mix[pl.ds(par, tm, stride=2), :] = halves[par][:, cols]
            o_ref[:, cols] = mix[...].astype(o_ref.dtype)


def conv_inverse(ie, io, pq, z, z_col0, gates, gate_col0, skip, order, n_seq, length, out_dtype,
                 token_order=False, total_rows=None, row0=0, prev=None):
    hw = pq.shape[3]
    half = length // 2
    tm = _pick(half, 512)
    nt = half // tm
    par = lambda a: a.reshape(n_seq, 2, half, a.shape[1])
    in_specs = [pl.BlockSpec((tm, length), lambda s, i: (i, 0)),
                pl.BlockSpec((tm, length), lambda s, i: (i, 0)),
                pl.BlockSpec((1, 2, length, hw), lambda s, i: (s, 0, 0, 0)),
                pl.BlockSpec((1, 2, tm, hw), lambda s, i: (s, 0, i, z_col0 // hw)),
                pl.BlockSpec((1, 2, tm, hw), lambda s, i: (s, 0, i, gate_col0 // hw)),
                pl.BlockSpec((1, 1, hw), lambda s, i: (order, 0, 0))]
    args = [ie, io, pq.reshape(n_seq, 2, length, hw), par(z), par(gates), skip.reshape(skip.shape[0], 1, hw)]
    aliases, scratch = {}, []
    if token_order:
        blk0 = row0 // (2 * tm)
        out_spec = pl.BlockSpec((2 * tm, hw), lambda s, i: (blk0 + s * nt + i, 0))
        out_shape = jax.ShapeDtypeStruct((total_rows, hw), out_dtype)
        scratch = [pltpu.VMEM((2 * tm, LANES_V7X), F32)]
        if prev is not None:
            in_specs.append(pl.BlockSpec(memory_space=pl.ANY))
            aliases = {len(args): 0}
            args.append(prev)
    else:
        out_spec = pl.BlockSpec((1, 2, tm, hw), lambda s, i: (s, 0, i, 0))
        out_shape = jax.ShapeDtypeStruct((n_seq, 2, half, hw), out_dtype)
    out = pl.pallas_call(
        functools.partial(_conv_inv_kernel, inv_len=1.0 / length, token_order=token_order),
        grid=(n_seq, nt),
        in_specs=in_specs,
        out_specs=out_spec,
        out_shape=out_shape,
        scratch_shapes=scratch,
        input_output_aliases=aliases,
        compiler_params=_cparams(2), name="hyena_conv_inv",
    )(*args)
    return out if token_order else out.reshape(n_seq * length, hw)


def _merge_kernel(oa_ref, oh_ref, y_ref, hf_ref, hb_ref, g_ref, wa_ref, wh_ref, wl_ref, o_ref, *, tn):
    d = o_ref.shape[1]
    ol = (jax.nn.gelu(y_ref[...].astype(F32), approximate=True) * (hf_ref[...] + hb_ref[...])).astype(BF16)
    oa = oa_ref[...]
    oh = oh_ref[...]
    for c0 in range(0, d, tn):
        cols = slice(c0, c0 + tn)
        gate = lambda k: _sigmoid(g_ref[:, k * d + c0:k * d + c0 + tn].astype(F32))
        m = (gate(0) * jnp.dot(oa, wa_ref[0, :, cols], preferred_element_type=F32)
             + gate(1) * jnp.dot(oh, wh_ref[0, :, cols], preferred_element_type=F32)
             + gate(2) * jnp.dot(ol, wl_ref[0, :, cols], preferred_element_type=F32))
        o_ref[:, cols] = m.astype(o_ref.dtype)


def merge_branches(o_att, o_hy, p, gates, hf, hb, wa, wh, wl, layer, lru_y_off, tm, rows=None):
    t, aw = o_att.shape
    rows = t if rows is None else rows
    hw = o_hy.shape[1]
    lw = hf.shape[1]
    d = wa.shape[2]
    w_spec = lambda rows: pl.BlockSpec((1, rows, d), lambda i: (layer, 0, 0))
    return pl.pallas_call(
        functools.partial(_merge_kernel, tn=_pick(d, 512)),
        grid=(rows // tm,),
        in_specs=[pl.BlockSpec((tm, aw), lambda i: (i, 0)),
                  pl.BlockSpec((tm, hw), lambda i: (i, 0)),
                  pl.BlockSpec((tm, lw), lambda i: (i, lru_y_off // lw)),
                  pl.BlockSpec((tm, lw), lambda i: (i, 0)),
                  pl.BlockSpec((tm, lw), lambda i: (i, 0)),
                  pl.BlockSpec((tm, 3 * d), lambda i: (i, 0)),
                  w_spec(aw), w_spec(hw), w_spec(lw)],
        out_specs=pl.BlockSpec((tm, d), lambda i: (i, 0)),
        out_shape=jax.ShapeDtypeStruct((t, d), BF16),
        compiler_params=_cparams(1), name="merge_branches",
    )(o_att, o_hy, p, hf, hb, gates, wa, wh, wl)


def _out_proj_kernel(m_ref, w_ref, x_ref, gate_ref, o_ref):
    o_ref[...] = x_ref[...] + gate_ref[0, 0] * jnp.dot(m_ref[...], w_ref[0],
                                                       preferred_element_type=F32)


def out_proj_residual(m, w, layer, x, mods, which_gate, n_lat, batch, tm, rows=None):
    t, d = x.shape
    return pl.pallas_call(
        _out_proj_kernel,
        grid=((t if rows is None else rows) // tm,),
        in_specs=[pl.BlockSpec((tm, d), lambda i: (i, 0)),
                  pl.BlockSpec((1, d, d), lambda i: (layer, 0, 0)),
                  pl.BlockSpec((tm, d), lambda i: (i, 0)),
                  _mod_spec(which_gate, tm, n_lat, batch, d)],
        out_specs=pl.BlockSpec((tm, d), lambda i: (i, 0)),
        out_shape=jax.ShapeDtypeStruct((t, d), F32),
        compiler_params=_cparams(1), name="out_proj_residual",
    )(m, w, x, mods)


def _router_kernel(x_ref, g_ref, sh_ref, sc_ref, rw_ref, rb_ref, h_ref, meta_ref, cnt_ref):
    h = _normmod(x_ref[...], g_ref[...], sh_ref[0, 0], sc_ref[0, 0])
    _store_tile_rows(h_ref, 0, _pack_bf16_pairs(h))
    w = rw_ref[...]
    h_hi, w_hi = h.astype(BF16), w.astype(BF16)
    h_lo = (h - h_hi.astype(F32)).astype(BF16)
    w_lo = (w - w_hi.astype(F32)).astype(BF16)
    logits = (jnp.dot(h_hi, w_hi, preferred_element_type=F32) + jnp.dot(h_lo, w_hi, preferred_element_type=F32)
              + jnp.dot(h_hi, w_lo, preferred_element_type=F32)) + rb_ref[...]
    lane = lax.broadcasted_iota(jnp.int32, logits.shape, 1)
    valid = lane < N_EXPERTS
    logits = jnp.where(valid, logits, -jnp.inf)
    ex = jnp.exp(logits - jnp.max(logits, axis=-1, keepdims=True))
    scores = ex / jnp.sum(ex, axis=-1, keepdims=True)
    grp = lax.shift_right_logical(lane, int(math.log2(EXPERTS_PER_GROUP)))
    best = jnp.zeros((logits.shape[0], 1), jnp.int32)
    best_max = jnp.max(jnp.where((grp == 0) & valid, scores, -1.0), axis=-1, keepdims=True)
    for g in range(1, N_GROUPS):
        gm = jnp.max(jnp.where((grp == g) & valid, scores, -1.0), axis=-1, keepdims=True)
        better = gm > best_max
        best = jnp.where(better, g, best)
        best_max = jnp.where(better, gm, best_max)
    s1 = jnp.where((grp == best) & valid, scores, -1.0)
    m1 = jnp.max(s1, axis=-1, keepdims=True)
    i1 = jnp.min(jnp.where(s1 == m1, lane, LANES_V7X), axis=-1, keepdims=True)
    s2 = jnp.where(lane == i1, -1.0, s1)
    m2 = jnp.max(s2, axis=-1, keepdims=True)
    i2 = jnp.min(jnp.where(s2 == m2, lane, LANES_V7X), axis=-1, keepdims=True)
    tot = m1 + m2
    @pl.when(pl.program_id(0) == 0)
    def _():
        cnt_ref[...] = jnp.zeros_like(cnt_ref)

    tm = logits.shape[0]
    onehot = ((lane == i1) | (lane == i2)).astype(BF16)
    lower = (lax.broadcasted_iota(jnp.int32, (tm, tm), 0)
             > lax.broadcasted_iota(jnp.int32, (tm, tm), 1)).astype(BF16)
    before = jnp.dot(lower, onehot, preferred_element_type=F32) + cnt_ref[...]
    r1 = jnp.sum(jnp.where(lane == i1, before, 0.0), axis=-1, keepdims=True)
    r2 = jnp.sum(jnp.where(lane == i2, before, 0.0), axis=-1, keepdims=True)
    cnt_ref[...] += jnp.sum(onehot.astype(F32), axis=0, keepdims=True)
    cols = (i1.astype(F32), i2.astype(F32), r1, r2, m1 / tot, m2 / tot)
    meta = jnp.zeros(logits.shape, F32)
    for k, col in enumerate(cols):
        meta = jnp.where(lane == k, col, meta)
    meta_ref[...] = meta


ROUTE_E, ROUTE_RANK, ROUTE_W = 0, 2, 4


def router(x, g, mods, which_shift, rw, rb, n_lat, batch, tm, rows=None):
    t, d = x.shape
    spr = d // 2 // LANES_V7X
    return pl.pallas_call(
        _router_kernel,
        grid=((t if rows is None else rows) // tm,),
        in_specs=[pl.BlockSpec((tm, d), lambda i: (i, 0)),
                  pl.BlockSpec((1, d), lambda i: (0, 0)),
                  _mod_spec(which_shift, tm, n_lat, batch, d),
                  _mod_spec(which_shift + 1, tm, n_lat, batch, d),
                  pl.BlockSpec((d, LANES_V7X), lambda i: (0, 0)),
                  pl.BlockSpec((1, LANES_V7X), lambda i: (0, 0))],
        out_specs=[pl.BlockSpec((tm * spr, LANES_V7X), lambda i: (i, 0)),
                   pl.BlockSpec((tm, LANES_V7X), lambda i: (i, 0)),
                   pl.BlockSpec((1, LANES_V7X), lambda i: (0, 0))],
        out_shape=[jax.ShapeDtypeStruct((t * spr, LANES_V7X), jnp.uint32),
                   jax.ShapeDtypeStruct((t, LANES_V7X), F32),
                   jax.ShapeDtypeStruct((1, LANES_V7X), F32)],
        compiler_params=_cparams(1), name="router",
    )(x, g.reshape(1, d), mods, mods, rw, rb)


def dispatch_plan(meta, counts, n_exp, tile, n_tiles):
    cnt = counts[0, :n_exp].astype(jnp.int32)
    tiles_e = (cnt + tile - 1) // tile
    tile_end = jnp.cumsum(tiles_e)
    row_start = (tile_end - tiles_e) * tile
    experts = meta[:, ROUTE_E:ROUTE_E + 2].astype(jnp.int32)
    ranks = meta[:, ROUTE_RANK:ROUTE_RANK + 2].astype(jnp.int32)
    dest = (row_start[experts] + ranks).reshape(-1)
    tile_expert = jnp.minimum(jnp.sum(jnp.arange(n_tiles)[:, None] >= tile_end[None, :], axis=1),
                              n_exp - 1).astype(jnp.int32)
    zero_start = jnp.minimum(row_start + cnt, (n_tiles - 1) * tile).astype(jnp.int32)
    return dest, tile_expert, tile_end[-1:].astype(jnp.int32), zero_start


def _dispatch_kernel(dest_ref, zero_ref, h_ref, xs_ref, zbuf, sem, zsem, *, spr):
    tm = h_ref.shape[0] // spr
    base = pl.program_id(0) * (2 * tm)
    token = lambda ref, t: ref.at[pl.ds(pl.multiple_of(t * spr, spr), spr)]

    @pl.when(pl.program_id(0) == 0)
    def _():
        zbuf[...] = jnp.zeros_like(zbuf)

        def clear(e, carry):
            start = pl.multiple_of(zero_ref[e] * spr, spr)
            copy = pltpu.make_async_copy(zbuf, xs_ref.at[pl.ds(start, zbuf.shape[0])], zsem)
            copy.start()
            copy.wait()
            return carry

        lax.fori_loop(0, zero_ref.shape[0], clear, 0)

    def issue(r, carry):
        for k in range(2):
            pltpu.make_async_copy(token(h_ref, r), token(xs_ref, dest_ref[base + 2 * r + k]), sem).start()
        return carry

    lax.fori_loop(0, tm, issue, 0, unroll=8)
    for k in range(2):
        pltpu.make_async_copy(h_ref, xs_ref.at[pl.ds(0, tm * spr)], sem).wait()


def moe_dispatch(h, dest, zero_start, n_rows, tile, tm, spr):
    assert spr == SUBLANES_V7X
    t = dest.shape[0] // 2
    return pl.pallas_call(
        functools.partial(_dispatch_kernel, spr=spr),
        grid_spec=pltpu.PrefetchScalarGridSpec(
            num_scalar_prefetch=2, grid=(t // tm,),
            in_specs=[pl.BlockSpec((tm * spr, LANES_V7X), lambda i, dest, zs: (i, 0))],
            out_specs=pl.BlockSpec(memory_space=pl.ANY),
            scratch_shapes=[pltpu.VMEM((tile * spr, LANES_V7X), h.dtype),
                            pltpu.SemaphoreType.DMA, pltpu.SemaphoreType.DMA]),
        out_shape=jax.ShapeDtypeStruct((n_rows * spr, LANES_V7X), h.dtype),
        compiler_params=_cparams(1), name="moe_dispatch",
    )(dest, zero_start, h)


def _experts_kernel(te_ref, nu_ref, xs_ref, wg_ref, wu_ref, wd_ref, ys_ref, wgb, wub, wdb, *, spr):
    j = pl.program_id(0)

    @pl.when(j < nu_ref[0])
    def _():
        @pl.when((j == 0) | (te_ref[j] != te_ref[jnp.maximum(j - 1, 0)]))
        def _():
            wgb[...] = wg_ref[0, 0].astype(BF16)
            wub[...] = wu_ref[0, 0].astype(BF16)
            wdb[...] = wd_ref[0, 0].astype(BF16)

        tile = xs_ref.shape[0] // spr
        x_hi, x_lo = (v.astype(BF16) for v in _unpack_bf16_pairs(_load_tile_rows(xs_ref, 0, tile, spr)))
        half = x_hi.shape[1]

        def proj(w):
            return (jnp.dot(x_hi, w[:half, :], preferred_element_type=F32)
                    + jnp.dot(x_lo, w[half:, :], preferred_element_type=F32))

        gt = proj(wgb)
        act = (gt * _sigmoid(gt)) * proj(wub)
        _store_tile_rows(ys_ref, 0, _pack_bf16_pairs(
            jnp.dot(act.astype(BF16), wdb[...], preferred_element_type=F32)))


def moe_experts(tile_expert, n_used, xs, wg, wu, wd, layer, tile):
    d, de = wg.shape[2], wg.shape[3]
    spr = d // 2 // LANES_V7X
    n_tiles = xs.shape[0] // (tile * spr)
    used = lambda j, nu: jnp.minimum(j, nu[0] - 1)
    w_spec = lambda shape: pl.BlockSpec((1, 1) + shape, lambda j, te, nu: (layer, te[used(j, nu)], 0, 0))
    return pl.pallas_call(
        functools.partial(_experts_kernel, spr=spr),
        grid_spec=pltpu.PrefetchScalarGridSpec(
            num_scalar_prefetch=2, grid=(n_tiles,),
            in_specs=[pl.BlockSpec((tile * spr, LANES_V7X), lambda j, te, nu: (used(j, nu), 0)),
                      w_spec((d, de)), w_spec((d, de)), w_spec((de, d))],
            out_specs=pl.BlockSpec((tile * spr, LANES_V7X), lambda j, te, nu: (used(j, nu), 0)),
            scratch_shapes=[pltpu.VMEM((d, de), BF16), pltpu.VMEM((d, de), BF16),
                            pltpu.VMEM((de, d), BF16)]),
        out_shape=jax.ShapeDtypeStruct(xs.shape, xs.dtype),
        compiler_params=_cparams(1), name="moe_experts",
    )(tile_expert, n_used, xs, wg, wu, wd)


def _combine_kernel(dest_ref, meta_ref, x_ref, gate_ref, ys_ref, o_ref, buf, sem, *, spr):
    i = pl.program_id(0)
    tm = x_ref.shape[0]
    part = lambda slot, k: pl.multiple_of((slot * 2 + k) * (tm * spr), spr)

    def gather(tile, slot):
        base = tile * (2 * tm)

        def issue(r, carry):
            for k in range(2):
                row = dest_ref[base + 2 * r + k]
                pltpu.make_async_copy(ys_ref.at[pl.ds(pl.multiple_of(row * spr, spr), spr)],
                                      buf.at[pl.ds(part(slot, k) + r * spr, spr)], sem.at[slot]).start()
            return carry

        lax.fori_loop(0, tm, issue, 0, unroll=8)

    @pl.when(i == 0)
    def _():
        gather(0, 0)

    @pl.when(i + 1 < pl.num_programs(0))
    def _():
        gather(i + 1, (i + 1) % 2)

    slot = i % 2
    for k in range(2):
        pltpu.make_async_copy(ys_ref.at[pl.ds(0, tm * spr)], buf.at[pl.ds(part(slot, k), tm * spr)],
                              sem.at[slot]).wait()
    meta = meta_ref[...]
    w1 = meta[:, ROUTE_W:ROUTE_W + 1]
    w2 = meta[:, ROUTE_W + 1:ROUTE_W + 2]
    y1_hi, y1_lo = _unpack_bf16_pairs(_load_tile_rows(buf, part(slot, 0), tm, spr))
    y2_hi, y2_lo = _unpack_bf16_pairs(_load_tile_rows(buf, part(slot, 1), tm, spr))
    half = y1_hi.shape[1]
    gate = gate_ref[0, 0]
    o_ref[:, :half] = x_ref[:, :half] + gate[:, :half] * (w1 * y1_hi + w2 * y2_hi)
    o_ref[:, half:] = x_ref[:, half:] + gate[:, half:] * (w1 * y1_lo + w2 * y2_lo)


def moe_combine(dest, meta, x, mods, which_gate, ys, n_lat, batch, tm):
    t, d = x.shape
    spr = d // 2 // LANES_V7X
    return pl.pallas_call(
        functools.partial(_combine_kernel, spr=spr),
        grid_spec=pltpu.PrefetchScalarGridSpec(
            num_scalar_prefetch=1, grid=(dest.shape[0] // 2 // tm,),
            in_specs=[pl.BlockSpec((tm, LANES_V7X), lambda i, dest: (i, 0)),
                      pl.BlockSpec((tm, d), lambda i, dest: (i, 0)),
                      _mod_spec(which_gate, tm, n_lat, batch, d),
                      pl.BlockSpec(memory_space=pl.ANY)],
            out_specs=pl.BlockSpec((tm, d), lambda i, dest: (i, 0)),
            scratch_shapes=[pltpu.VMEM((2 * 2 * tm * spr, LANES_V7X), ys.dtype),
                            pltpu.SemaphoreType.DMA((2,))]),
        out_shape=jax.ShapeDtypeStruct((t, d), F32),
        compiler_params=_cparams(1), name="moe_combine",
    )(dest, meta, x, mods, ys)


def _rope_table(n_tokens):
    rows = n_tokens // GRID_W
    row = jnp.repeat(jnp.arange(rows), GRID_W).astype(F32)
    col = jnp.tile(jnp.arange(GRID_W), rows).astype(F32)
    pairs = HEAD_DIM // 4
    inv = ROPE_THETA ** (-jnp.arange(pairs, dtype=F32) / pairs)
    ang = jnp.concatenate([row[:, None] * inv, col[:, None] * inv], axis=-1)
    cos, sin = jnp.cos(ang), jnp.sin(ang)
    return jnp.concatenate([cos, cos, -sin, sin], axis=-1)


def _trig_kernel(ca_ref, sa_ref, cr_ref, sr_ref, o_ref, *, sign, inverse):
    tm = cr_ref.shape[0]
    n_lanes = ca_ref.shape[1] * LANES_V7X
    cr = cr_ref[...]
    sr = sr_ref[...]
    row = pl.program_id(0) * tm + lax.broadcasted_iota(jnp.int32, (tm, 1), 0)
    alt_row = sign * (1 - 2 * (row & 1)).astype(F32)
    for a in range(n_lanes // LANES_V7X):
        blk = slice(a * LANES_V7X, (a + 1) * LANES_V7X)
        ca = ca_ref[:, a:a + 1]
        sa = sa_ref[:, a:a + 1]
        cos_blk = ca * cr - sa * sr
        nsin_blk = -(sa * cr + ca * sr)
        col = a * LANES_V7X + lax.broadcasted_iota(jnp.int32, (1, LANES_V7X), 1)
        if inverse:
            o_ref[:, blk] = cos_blk.astype(o_ref.dtype)
            o_ref[:, n_lanes + a * LANES_V7X:n_lanes + (a + 1) * LANES_V7X] = (
                jnp.where(col == 0, alt_row, nsin_blk).astype(o_ref.dtype))
        else:
            alt_col = sign * (1 - 2 * (col & 1)).astype(F32)
            o_ref[0, :, blk] = cos_blk.astype(o_ref.dtype)
            o_ref[1, :, blk] = jnp.where(row == 0, alt_col, nsin_blk).astype(o_ref.dtype)


def _dft_tables(length):
    half = length // 2
    n_a = half // LANES_V7X
    tm = _pick(half, 256)
    idx = jnp.arange(half, dtype=jnp.int32)[:, None]
    a_hi = jnp.arange(n_a, dtype=jnp.int32)[None, :] * LANES_V7X
    r_lo = jnp.arange(LANES_V7X, dtype=jnp.int32)[None, :]
    ang = lambda prod: (prod % (2 * length)).astype(F32) * (math.pi / length)

    def table(row_val, lane_hi, lane_lo, sign, inverse):
        out_shape = (half, 2 * half) if inverse else (2, half, half)
        out_block = (tm, 2 * half) if inverse else (2, tm, half)
        out_index = (lambda i: (i, 0)) if inverse else (lambda i: (0, i, 0))
        small = [f(ang(row_val * v)) for v in (lane_hi, lane_lo) for f in (jnp.cos, jnp.sin)]
        return pl.pallas_call(
            functools.partial(_trig_kernel, sign=sign, inverse=inverse),
            grid=(half // tm,),
            in_specs=[pl.BlockSpec((tm, n_a), lambda i: (i, 0)), pl.BlockSpec((tm, n_a), lambda i: (i, 0)),
                      pl.BlockSpec((tm, LANES_V7X), lambda i: (i, 0)),
                      pl.BlockSpec((tm, LANES_V7X), lambda i: (i, 0))],
            out_specs=pl.BlockSpec(out_block, out_index),
            out_shape=jax.ShapeDtypeStruct(out_shape, BF16),
            compiler_params=_cparams(1), name="dft_tables",
        )(*small)

    tables = []
    for par, sign in ((0, 1.0), (1, -1.0)):
        tables.append(table(idx, 2 * a_hi, 2 * r_lo + par, sign, False))
        tables.append(table(2 * idx + par, a_hi, r_lo, sign, True))
    fe, ie, fo, io = tables
    return fe, fo, ie, io


def _hyena_features(length, k_pad):
    t = jnp.linspace(0.0, 1.0, length, dtype=F32)[:, None]
    w = 2.0 * math.pi * jnp.arange(length, dtype=F32)[:, None] / length
    f = jnp.linspace(1e-4, HYENA_BANDS - 1, HYENA_BANDS, dtype=F32)[None, :]
    z = jnp.concatenate([t, jnp.cos(f * w), -jnp.sin(f * w)], axis=-1)
    return jnp.pad(z, ((0, 0), (0, k_pad - z.shape[1])))


def _blockdiag_dense(w):
    nb, bs = w.shape[-3], w.shape[-2]
    eye = jnp.eye(nb, dtype=w.dtype)
    dense = jnp.einsum('...nde,nm->...ndme', w, eye)
    return dense.reshape(w.shape[:-3] + (nb * bs, nb * bs))


def kernel(x, c, ctx, c_ctx, w_ada, b_ada, g_mix, g_ffn, w_in, attn_sink, hy_short_w, hy_short_b, hy_w1, hy_b1, hy_freq, hy_w2, hy_b2, hy_w3, hy_skip, lru_conv_w, lru_conv_b, lru_wa, lru_ba, lru_wx, lru_bx, lru_lambda, w_br_attn, w_br_hy, w_br_lru, w_out, router_w, router_b, exp_w_gate, exp_w_up, exp_w_down, final_g):
    batch, n_lat, d = x.shape
    n_ctx = ctx.shape[1]
    depth = w_ada.shape[0]
    t_lat, t_ctx = batch * n_lat, batch * n_ctx
    hw = hy_skip.shape[2]
    lw = lru_lambda.shape[2]
    fh = hy_w2.shape[1]
    aw = N_Q_HEADS * HEAD_DIM
    kvw = N_KV_HEADS * HEAD_DIM
    q_off = 2 * kvw + lw
    hy_off = q_off + aw
    lru_y_off = hy_off + 3 * hw
    gate_off = lru_y_off + lw
    assert batch + 1 <= MOD_ROWS and t_lat % n_ctx == 0 and n_lat % ATTN_BLOCK == 0
    tile_base = math.gcd(n_lat, t_ctx)
    tm = _pick(tile_base, 512)
    tm_big = _pick(tile_base, 1024)

    xu = jnp.concatenate([x.reshape(t_lat, d), ctx.reshape(t_ctx, d)], axis=0)
    cc = jnp.concatenate([c, c_ctx[None, :], jnp.zeros((MOD_ROWS - batch - 1, d), F32)], axis=0)
    mods_all = ada_tables(cc, w_ada, b_ada).reshape(depth, MOD_ROWS, N_MOD, 1, d)

    rope_tab = _rope_table(n_lat)
    band_bias = _band_bias(n_lat)
    deltas = jnp.abs(jnp.linspace(math.log(HYENA_DECAY_TARGET) / HYENA_FAST_DECAY,
                                  math.log(HYENA_DECAY_TARGET) / HYENA_SLOW_DECAY, hw, dtype=F32))[None, :]
    seqs = []
    for length, row0 in ((n_lat, 0), (n_ctx, t_lat)):
        seqs.append((length, row0, _dft_tables(length), _hyena_features(length, fh)))
    hy_w1p = jnp.pad(hy_w1, ((0, 0), (0, fh - hy_w1.shape[1]), (0, 0)))

    sp = softplus_neg(lru_lambda)
    lru_w = jnp.concatenate([_blockdiag_dense(lru_wa), _blockdiag_dense(lru_wx)], axis=-1).astype(BF16)
    lru_b = jnp.concatenate([lru_ba, lru_bx], axis=-1)[:, :, None, :]
    wa_b, wh_b, wl_b, wo_b = (w.astype(BF16) for w in (w_br_attn, w_br_hy, w_br_lru, w_out))
    rw = jnp.pad(router_w, ((0, 0), (0, LANES_V7X - router_w.shape[1])))
    rb = jnp.pad(router_b, (0, LANES_V7X - router_b.shape[0]))[None, :]
    h_zero = jnp.zeros((batch, 2, lw), F32)
    t_all = t_lat + t_ctx
    n_exp = exp_w_gate.shape[1]
    exp_tile = _pick(2 * t_all, 512)
    n_exp_tiles = (2 * t_all) // exp_tile + n_exp
    tm_cmb = _pick(tile_base, 256)

    for l in range(depth):
        last = l == depth - 1
        rows = t_lat if last else t_all
        mods = mods_all[l]
        h = normmod(xu, g_mix[l], mods, 0, n_lat, batch, tm)
        p = in_proj(h, w_in, l, tm_big, 0, gate_off, BF16)
        gates = in_proj(h, w_in, l, tm_big, gate_off, w_in.shape[2] - gate_off, BF16, rows=rows)

        o_att = latent_attention(p, attn_sink[l], rope_tab, band_bias, batch, n_lat, n_ctx, q_off)
        if not last:
            o_att = context_attention(p, attn_sink[l], o_att, batch, n_lat, n_ctx, q_off)

        xc_c = dwconv(p, lru_conv_w[l], lru_conv_b[l], batch, n_ctx, t_lat, 2 * kvw, lw)
        hf, hb, h_end = rg_lru(xc_c, lru_w, l, lru_b[l], sp[l], h_zero, batch, n_ctx, t_all, t_lat)
        xc_l = dwconv(p, lru_conv_w[l], lru_conv_b[l], batch, n_lat, 0, 2 * kvw, lw)
        hf, hb, _ = rg_lru(xc_l, lru_w, l, lru_b[l], sp[l], h_end, batch, n_lat, t_all, 0, prev=(hf, hb))

        o_hy = None
        for length, row0, (fe, fo, ie, io), feats in seqs[:1] if last else seqs:
            u = dwconv(p, hy_short_w[l], hy_short_b[l], batch, length, row0, hy_off, 3 * hw, parity_order=True)
            hid = hyena_hidden(feats, hy_w1p[l], hy_b1[l], hy_freq[l], hy_w2[l], hy_b2[l])
            taps, kh = hyena_taps(hid, hy_w3[l], deltas)
            kspec = hyena_spectrum(fe, fo, taps)
            pq = conv_forward(fe, fo, u, 0, kspec, kh, 0, batch, length, hw)
            z1 = conv_inverse(ie, io, pq, u, 0, u, hw, hy_skip[l], 0, batch, length, BF16)
            pq = conv_forward(fe, fo, z1, 0, kspec, kh, 1, batch, length, hw)
            o_hy = conv_inverse(ie, io, pq, z1, 0, u, 2 * hw, hy_skip[l], 1, batch, length, BF16,
                                token_order=True, total_rows=t_all, row0=row0, prev=o_hy)

        m = merge_branches(o_att, o_hy, p, gates, hf, hb, wa_b, wh_b, wl_b, l, lru_y_off, tm, rows=rows)
        xu = out_proj_residual(m, wo_b, l, xu, mods, 2, n_lat, batch, tm, rows=rows)

        fl, route, counts = router(xu, g_ffn[l], mods, 3, rw, rb, n_lat, batch, tm, rows=rows)
        dest, tile_expert, n_used, zero_start = dispatch_plan(route[:rows], counts, n_exp, exp_tile, n_exp_tiles)
        xs = moe_dispatch(fl, dest, zero_start, n_exp_tiles * exp_tile, exp_tile, tm, d // 2 // LANES_V7X)
        ys = moe_experts(tile_expert, n_used, xs, exp_w_gate, exp_w_up, exp_w_down, l, exp_tile)
        xu = moe_combine(dest, route, xu, mods, 5, ys, n_lat, batch, tm_cmb)

    return final_norm(xu, final_g, t_lat, tm).reshape(batch, n_lat, d)
```

```python
import functools
import math

import jax
import jax.numpy as jnp
import numpy as np
from jax import lax
from jax.experimental import pallas as pl
from jax.experimental.pallas import tpu as pltpu

F32 = jnp.float32
BF16 = jnp.bfloat16
HIGHEST = lax.Precision.HIGHEST

LANES_V7X = 128
VMEM_LIMIT_V7X = 56 * 1024 * 1024

EPS = 1e-6
GRID_W = 64
HEAD_DIM = 128
N_Q_HEADS = 8
N_KV_HEADS = 2
Q_PER_KV = N_Q_HEADS // N_KV_HEADS
WINDOW = 128
ATTN_BLOCK = 128
ROPE_THETA = 10000.0
HYENA_BANDS = 16
HYENA_DECAY_TARGET = 1e-2
HYENA_FAST_DECAY = 0.3
HYENA_SLOW_DECAY = 1.5
LRU_C = 8.0
N_EXPERTS = 16
N_GROUPS = 4
EXPERTS_PER_GROUP = N_EXPERTS // N_GROUPS
LOG2_E = math.log2(math.e)
N_MOD = 6
MOD_ROWS = 8


def _cparams(n_axes):
    return pltpu.CompilerParams(dimension_semantics=("arbitrary",) * n_axes,
                                vmem_limit_bytes=VMEM_LIMIT_V7X)


def _sigmoid(x):
    return 0.5 * (1.0 + jnp.tanh(0.5 * x))


def _pack_bf16_pairs(x):
    half = x.shape[1] // 2
    hi = lax.bitcast_convert_type(x[:, :half].astype(BF16).astype(F32), jnp.uint32)
    lo = lax.bitcast_convert_type(x[:, half:].astype(BF16).astype(F32), jnp.uint32)
    return hi | (lo >> 16)


def _unpack_bf16_pairs(w):
    hi = lax.bitcast_convert_type(w & jnp.uint32(0xFFFF0000), F32)
    lo = lax.bitcast_convert_type(w << 16, F32)
    return hi, lo


SUBLANES_V7X = 8


def _store_tile_rows(ref, row0, x):
    rows, width = x.shape
    s_per_row = width // LANES_V7X
    for s in range(s_per_row):
        ref[pl.ds(row0 + s, rows, stride=s_per_row), :] = x[:, s * LANES_V7X:(s + 1) * LANES_V7X]


def _load_tile_rows(ref, row0, rows, s_per_row):
    return jnp.concatenate([ref[pl.ds(row0 + s, rows, stride=s_per_row), :] for s in range(s_per_row)], axis=1)


def _pick(n, cap):
    t = cap
    while n % t:
        t //= 2
    return t


def _ada_kernel(c_ref, w_ref, b_ref, o_ref):
    c = c_ref[...]
    s = (c * jax.nn.sigmoid(c)).astype(BF16)
    o_ref[0] = jnp.dot(s, w_ref[0].astype(BF16), preferred_element_type=F32) + b_ref[0]


def ada_tables(cc, w_ada, b_ada):
    depth, d, n6 = w_ada.shape
    tn = _pick(n6, 1024)
    return pl.pallas_call(
        _ada_kernel,
        grid=(depth, n6 // tn),
        in_specs=[pl.BlockSpec((MOD_ROWS, d), lambda l, j: (0, 0)),
                  pl.BlockSpec((1, d, tn), lambda l, j: (l, 0, j)),
                  pl.BlockSpec((1, 1, tn), lambda l, j: (l, 0, j))],
        out_specs=pl.BlockSpec((1, MOD_ROWS, tn), lambda l, j: (l, 0, j)),
        out_shape=jax.ShapeDtypeStruct((depth, MOD_ROWS, n6), F32),
        compiler_params=_cparams(2), name="ada_tables",
    )(cc, w_ada, b_ada.reshape(depth, 1, n6))


def _mod_spec(which, tm, n_lat, batch, d):
    return pl.BlockSpec((1, 1, 1, d),
                        lambda i, *_: (jnp.minimum((i * tm) // n_lat, batch), which, 0, 0))


def _normmod(x, g, shift, scale):
    y = x * lax.rsqrt(jnp.mean(x * x, axis=-1, keepdims=True) + EPS) * g
    return y * (1.0 + scale) + shift


def _normmod_kernel(x_ref, g_ref, sh_ref, sc_ref, o_ref):
    o_ref[...] = _normmod(x_ref[...], g_ref[...], sh_ref[0, 0], sc_ref[0, 0]).astype(o_ref.dtype)


def normmod(x, g, mods, which_shift, n_lat, batch, tm):
    t, d = x.shape
    return pl.pallas_call(
        _normmod_kernel,
        grid=(t // tm,),
        in_specs=[pl.BlockSpec((tm, d), lambda i: (i, 0)),
                  pl.BlockSpec((1, d), lambda i: (0, 0)),
                  _mod_spec(which_shift, tm, n_lat, batch, d),
                  _mod_spec(which_shift + 1, tm, n_lat, batch, d)],
        out_specs=pl.BlockSpec((tm, d), lambda i: (i, 0)),
        out_shape=jax.ShapeDtypeStruct((t, d), BF16),
        compiler_params=_cparams(1), name="normmod",
    )(x, g.reshape(1, d), mods, mods)


def _final_norm_kernel(x_ref, g_ref, o_ref):
    x = x_ref[...]
    o_ref[...] = x * lax.rsqrt(jnp.mean(x * x, axis=-1, keepdims=True) + EPS) * g_ref[...]


def final_norm(x, g, rows, tm):
    d = x.shape[1]
    return pl.pallas_call(
        _final_norm_kernel,
        grid=(rows // tm,),
        in_specs=[pl.BlockSpec((tm, d), lambda i: (i, 0)), pl.BlockSpec((1, d), lambda i: (0, 0))],
        out_specs=pl.BlockSpec((tm, d), lambda i: (i, 0)),
        out_shape=jax.ShapeDtypeStruct((rows, d), F32),
        compiler_params=_cparams(1), name="final_norm",
    )(x, g.reshape(1, d))


def _proj_kernel(a_ref, w_ref, o_ref, wb_ref):
    @pl.when(pl.program_id(1) == 0)
    def _():
        wb_ref[...] = w_ref[0].astype(BF16)

    o_ref[...] = jnp.dot(a_ref[...], wb_ref[...], preferred_element_type=F32).astype(o_ref.dtype)


def in_proj(a, w, layer, tm, col0, n, out_dtype, rows=None):
    t, k = a.shape
    tn = _pick(math.gcd(n, col0) if col0 else n, 1024)
    return pl.pallas_call(
        _proj_kernel,
        grid=(n // tn, (t if rows is None else rows) // tm),
        in_specs=[pl.BlockSpec((tm, k), lambda j, i: (i, 0)),
                  pl.BlockSpec((1, k, tn), lambda j, i: (layer, 0, col0 // tn + j))],
        out_specs=pl.BlockSpec((tm, tn), lambda j, i: (i, j)),
        out_shape=jax.ShapeDtypeStruct((t, n), out_dtype),
        scratch_shapes=[pltpu.VMEM((k, tn), BF16)],
        compiler_params=_cparams(2), name="in_proj",
    )(a, w)


def _rope(x, tab):
    return x * tab[:, :HEAD_DIM] + pltpu.roll(x, HEAD_DIM // 2, axis=1) * tab[:, HEAD_DIM:]


def _sink_column(sink_ref, h, rows):
    r = lax.broadcasted_iota(jnp.int32, (Q_PER_KV * rows, 1), 0)
    col = jnp.full((Q_PER_KV * rows, 1), sink_ref[h * Q_PER_KV], F32)
    for g in range(1, Q_PER_KV):
        col = jnp.where(r >= g * rows, sink_ref[h * Q_PER_KV + g], col)
    return col


def _nt(a, b):
    return lax.dot_general(a, b, (((1,), (1,)), ((), ())), preferred_element_type=F32)


def _band_bias(n_lat):
    blk = ATTN_BLOCK
    single = n_lat == blk
    qi = (jnp.arange(Q_PER_KV * blk) % blk)[:, None]
    kj = jnp.arange(3 * blk)[None, :]
    band = jnp.abs(qi + blk - kj) <= WINDOW
    kinds = []
    for no_prev, no_next in ((True, single), (False, False), (single, True)):
        ok = band & ((kj >= blk) | (not no_prev)) & ((kj < 2 * blk) | (not no_next))
        kinds.append(jnp.where(ok, 0.0, -jnp.inf))
    return jnp.stack(kinds).astype(F32)


def _lat_attn_kernel(sink_ref, q_ref, kvm_ref, kv0_ref, kvp_ref, kvc_ref, tm_ref, t0_ref, tp_ref, bias_ref,
                     o_ref):
    blk = ATTN_BLOCK
    kv_w = N_KV_HEADS * HEAD_DIM
    scale = HEAD_DIM ** -0.5
    tabs = (tm_ref[...], t0_ref[...], tp_ref[...])
    kvs = (kvm_ref, kv0_ref, kvp_ref)
    for h in range(N_KV_HEADS):
        ks = slice(h * HEAD_DIM, (h + 1) * HEAD_DIM)
        vs = slice(kv_w + h * HEAD_DIM, kv_w + (h + 1) * HEAD_DIM)
        k_loc = jnp.concatenate([_rope(kvs[n][:, ks].astype(F32), tabs[n]) for n in range(3)],
                                axis=0).astype(BF16)
        v_loc = jnp.concatenate([kvs[n][:, vs] for n in range(3)], axis=0).astype(BF16)
        k_ctx = kvc_ref[:, ks].astype(BF16)
        v_ctx = kvc_ref[:, vs].astype(BF16)
        q4 = jnp.concatenate(
            [_rope(q_ref[:, (h * Q_PER_KV + g) * HEAD_DIM:(h * Q_PER_KV + g + 1) * HEAD_DIM].astype(F32), tabs[1])
             * (scale * LOG2_E) for g in range(Q_PER_KV)], axis=0).astype(BF16)
        s_loc = _nt(q4, k_loc) + bias_ref[0]
        s_ctx = _nt(q4, k_ctx)
        sink = _sink_column(sink_ref, h, blk) * LOG2_E
        m = jnp.maximum(jnp.maximum(jnp.max(s_loc, axis=-1, keepdims=True),
                                    jnp.max(s_ctx, axis=-1, keepdims=True)), sink)
        p_loc = jnp.exp2(s_loc - m)
        p_ctx = jnp.exp2(s_ctx - m)
        den = (jnp.sum(p_loc, axis=-1, keepdims=True) + jnp.sum(p_ctx, axis=-1, keepdims=True)
               + jnp.exp2(sink - m))
        o = (jnp.dot(p_loc.astype(BF16), v_loc, preferred_element_type=F32)
             + jnp.dot(p_ctx.astype(BF16), v_ctx, preferred_element_type=F32)) / den
        for g in range(Q_PER_KV):
            hq = h * Q_PER_KV + g
            o_ref[:, hq * HEAD_DIM:(hq + 1) * HEAD_DIM] = o[g * blk:(g + 1) * blk].astype(o_ref.dtype)


def latent_attention(p, sink, rope_tab, band_bias, batch, n_lat, n_ctx, q_off):
    blk = ATTN_BLOCK
    nb = n_lat // blk
    aw = N_Q_HEADS * HEAD_DIM
    kvw = 2 * N_KV_HEADS * HEAD_DIM
    ctx_blk0 = (batch * n_lat) // n_ctx
    kv_spec = lambda off: pl.BlockSpec(
        (blk, kvw), lambda b, i: (b * nb + jnp.clip(i + off, 0, nb - 1), 0))
    tab_spec = lambda off: pl.BlockSpec(
        (blk, 2 * HEAD_DIM), lambda b, i: (jnp.clip(i + off, 0, nb - 1), 0))
    bias_kind = lambda b, i: (jnp.where(i == 0, 0, jnp.where(i == nb - 1, 2, 1)), 0, 0)
    return pl.pallas_call(
        _lat_attn_kernel,
        grid=(batch, nb),
        in_specs=[pl.BlockSpec(memory_space=pltpu.SMEM),
                  pl.BlockSpec((blk, aw), lambda b, i: (b * nb + i, q_off // aw)),
                  kv_spec(-1), kv_spec(0), kv_spec(1),
                  pl.BlockSpec((n_ctx, kvw), lambda b, i: (ctx_blk0 + b, 0)),
                  tab_spec(-1), tab_spec(0), tab_spec(1),
                  pl.BlockSpec((1, Q_PER_KV * blk, 3 * blk), bias_kind)],
        out_specs=pl.BlockSpec((blk, aw), lambda b, i: (b * nb + i, 0)),
        out_shape=jax.ShapeDtypeStruct((p.shape[0], aw), BF16),
        compiler_params=_cparams(2), name="latent_attention",
    )(sink, p, p, p, p, p, rope_tab, rope_tab, rope_tab, band_bias)


def _ctx_attn_kernel(sink_ref, q_ref, kv_ref, o_lat_ref, o_ref):
    del o_lat_ref
    rows = q_ref.shape[0]
    kv_w = N_KV_HEADS * HEAD_DIM
    scale = HEAD_DIM ** -0.5
    for h in range(N_KV_HEADS):
        k = kv_ref[:, h * HEAD_DIM:(h + 1) * HEAD_DIM].astype(BF16)
        v = kv_ref[:, kv_w + h * HEAD_DIM:kv_w + (h + 1) * HEAD_DIM].astype(BF16)
        q4 = jnp.concatenate(
            [q_ref[:, (h * Q_PER_KV + g) * HEAD_DIM:(h * Q_PER_KV + g + 1) * HEAD_DIM]
             for g in range(Q_PER_KV)], axis=0).astype(BF16)
        s = _nt(q4, k) * scale
        sink = _sink_column(sink_ref, h, rows)
        m = jnp.maximum(jnp.max(s, axis=-1, keepdims=True), sink)
        p = jnp.exp(s - m)
        den = jnp.sum(p, axis=-1, keepdims=True) + jnp.exp(sink - m)
        o = jnp.dot(p.astype(BF16), v, preferred_element_type=F32) / den
        for g in range(Q_PER_KV):
            hq = h * Q_PER_KV + g
            o_ref[:, hq * HEAD_DIM:(hq + 1) * HEAD_DIM] = o[g * rows:(g + 1) * rows].astype(o_ref.dtype)


def context_attention(p, sink, o_lat, batch, n_lat, n_ctx, q_off):
    aw = N_Q_HEADS * HEAD_DIM
    kvw = 2 * N_KV_HEADS * HEAD_DIM
    ctx_blk0 = (batch * n_lat) // n_ctx
    return pl.pallas_call(
        _ctx_attn_kernel,
        grid=(batch,),
        in_specs=[pl.BlockSpec(memory_space=pltpu.SMEM),
                  pl.BlockSpec((n_ctx, aw), lambda b: (ctx_blk0 + b, q_off // aw)),
                  pl.BlockSpec((n_ctx, kvw), lambda b: (ctx_blk0 + b, 0)),
                  pl.BlockSpec(memory_space=pl.ANY)],
        out_specs=pl.BlockSpec((n_ctx, aw), lambda b: (ctx_blk0 + b, 0)),
        out_shape=jax.ShapeDtypeStruct(o_lat.shape, o_lat.dtype),
        input_output_aliases={3: 0},
        compiler_params=_cparams(1), name="context_attention",
    )(sink, p, p, o_lat)


def _dwconv_kernel(u_ref, w_ref, b_ref, o_ref, *scratch, width):
    x = u_ref[...].astype(F32)
    length = x.shape[0]
    left = width // 2
    t = lax.broadcasted_iota(jnp.int32, (length, 1), 0)
    acc = jnp.broadcast_to(b_ref[...], x.shape)
    for j in range(width):
        s = j - left
        if s == 0:
            xs = x
        else:
            xs = pltpu.roll(x, (-s) % length, axis=0)
            xs = jnp.where((t + s >= 0) & (t + s < length), xs, 0.0)
        acc = acc + xs * w_ref[j:j + 1, :]
    if not scratch:
        o_ref[...] = acc
    else:
        half = length // 2
        scratch[0][...] = acc
        o_ref[:half, :] = scratch[0][pl.ds(0, half, stride=2), :].astype(o_ref.dtype)
        o_ref[half:, :] = scratch[0][pl.ds(1, half, stride=2), :].astype(o_ref.dtype)


def dwconv(p, w, b, n_seq, length, row0, col0, n_ch, parity_order=False):
    width = w.shape[0]
    cb = LANES_V7X if parity_order else _pick(n_ch, 256)
    return pl.pallas_call(
        functools.partial(_dwconv_kernel, width=width),
        grid=(n_seq, n_ch // cb),
        in_specs=[pl.BlockSpec((length, cb), lambda s, c: (row0 // length + s, col0 // cb + c)),
                  pl.BlockSpec((width, cb), lambda s, c: (0, c)),
                  pl.BlockSpec((1, cb), lambda s, c: (0, c))],
        out_specs=pl.BlockSpec((length, cb), lambda s, c: (s, c)),
        out_shape=jax.ShapeDtypeStruct((n_seq * length, n_ch), BF16 if parity_order else F32),
        scratch_shapes=[pltpu.VMEM((length, cb), F32)] if parity_order else [],
        compiler_params=_cparams(2), name="dwconv",
    )(p, w, b.reshape(1, n_ch))


def _lru_kernel(xf_ref, xb_ref, w_ref, bias_ref, sp_ref, h0_ref, *rest):
    hf_ref, hb_ref, hl_ref, af, bf, ab, bb, carry = rest[-8:]
    j = pl.program_id(1)
    width = xf_ref.shape[1]

    @pl.when(j == 0)
    def _():
        carry[...] = h0_ref[0]

    def gates(x, d):
        g = jnp.dot(x.astype(BF16), w_ref[0, d], preferred_element_type=F32) + bias_ref[d]
        a = jnp.exp2((-0.5 * LRU_C * LOG2_E * sp_ref[d]) * (1.0 + jnp.tanh(0.5 * g[:, :width])))
        gx = (1.0 + jnp.tanh(0.5 * g[:, width:])) * (0.5 * x)
        return a, jnp.sqrt(1.0 - a * a) * gx

    af[...], bf[...] = gates(xf_ref[...], 0)
    ab[...], bb[...] = gates(xb_ref[...], 1)
    tl = af.shape[0]

    def body(t, hs):
        hf, hb = hs
        hf = af[pl.ds(t, 1), :] * hf + bf[pl.ds(t, 1), :]
        hf_ref[pl.ds(t, 1), :] = hf
        tb = tl - 1 - t
        hb = ab[pl.ds(tb, 1), :] * hb + bb[pl.ds(tb, 1), :]
        hb_ref[pl.ds(tb, 1), :] = hb
        return hf, hb

    hf, hb = lax.fori_loop(0, tl, body, (carry[0:1, :], carry[1:2, :]), unroll=8)
    carry[0:1, :] = hf
    carry[1:2, :] = hb
    hl_ref[0] = carry[...]


def rg_lru(xc, w_gates, layer, b_gates, softplus_neg_lam, h0, n_seq, length, total_rows, row0, prev=None):
    width = xc.shape[1]
    tl = _pick(length, 512)
    nc = length // tl
    blk0 = row0 // tl
    in_specs = [pl.BlockSpec((tl, width), lambda s, j: (s * nc + j, 0)),
                pl.BlockSpec((tl, width), lambda s, j: (s * nc + nc - 1 - j, 0)),
                pl.BlockSpec((1, 2, width, 2 * width), lambda s, j: (layer, 0, 0, 0)),
                pl.BlockSpec((2, 1, 2 * width), lambda s, j: (0, 0, 0)),
                pl.BlockSpec((2, 1, width), lambda s, j: (0, 0, 0)),
                pl.BlockSpec((1, 2, width), lambda s, j: (s, 0, 0))]
    args = [xc, xc, w_gates, b_gates, softplus_neg_lam, h0]
    aliases = {}
    if prev is not None:
        in_specs += [pl.BlockSpec(memory_space=pl.ANY)] * 2
        aliases = {len(args): 0, len(args) + 1: 1}
        args += list(prev)
    return pl.pallas_call(
        _lru_kernel,
        grid=(n_seq, nc),
        in_specs=in_specs,
        out_specs=[pl.BlockSpec((tl, width), lambda s, j: (blk0 + s * nc + j, 0)),
                   pl.BlockSpec((tl, width), lambda s, j: (blk0 + s * nc + nc - 1 - j, 0)),
                   pl.BlockSpec((1, 2, width), lambda s, j: (s, 0, 0))],
        out_shape=[jax.ShapeDtypeStruct((total_rows, width), F32),
                   jax.ShapeDtypeStruct((total_rows, width), F32),
                   jax.ShapeDtypeStruct((n_seq, 2, width), F32)],
        scratch_shapes=[pltpu.VMEM((tl, width), F32)] * 4 + [pltpu.VMEM((2, width), F32)],
        input_output_aliases=aliases,
        compiler_params=_cparams(2), name="rg_lru",
    )(*args)


def _softplus_kernel(x_ref, o_ref):
    x = -x_ref[...]
    o_ref[...] = jnp.maximum(x, 0.0) + jnp.log(1.0 + jnp.exp(-jnp.abs(x)))


def softplus_neg(lam):
    depth, two, width = lam.shape
    x = lam.reshape(depth * two, width)
    out = pl.pallas_call(
        _softplus_kernel,
        out_shape=jax.ShapeDtypeStruct(x.shape, F32), name="softplus_neg",
    )(x)
    return out.reshape(depth, two, 1, width)


def _hy_hidden_kernel(z_ref, w1_ref, b1_ref, fr_ref, w2_ref, b2_ref, o_ref):
    fr = fr_ref[...]
    h = jnp.sin(fr * (jnp.dot(z_ref[...], w1_ref[...], precision=HIGHEST,
                              preferred_element_type=F32) + b1_ref[...]))
    o_ref[...] = jnp.sin(fr * (jnp.dot(h, w2_ref[...], precision=HIGHEST,
                                       preferred_element_type=F32) + b2_ref[...]))


def hyena_hidden(z, w1, b1, freq, w2, b2):
    length = z.shape[0]
    fh = w2.shape[0]
    return pl.pallas_call(
        _hy_hidden_kernel,
        out_shape=jax.ShapeDtypeStruct((length, fh), F32), name="hyena_hidden",
    )(z, w1, b1.reshape(1, fh), freq.reshape(1, fh), w2, b2.reshape(1, fh))


def _hy_taps_kernel(hid_ref, wf_ref, wb_ref, delta_ref, u_ref, kh_ref, tmp):
    hid = hid_ref[...]
    length = hid.shape[0]
    hf = jnp.dot(hid, wf_ref[...], precision=HIGHEST, preferred_element_type=F32)
    hb = jnp.dot(hid, wb_ref[...], precision=HIGHEST, preferred_element_type=F32)
    ti = lax.broadcasted_iota(jnp.int32, (length, 1), 0)
    win = jnp.exp(-(ti.astype(F32) * (1.0 / (length - 1))) * delta_ref[...])
    f = hf * win
    b = jnp.where(ti >= 1, hb * win, 0.0)
    sc = lax.rsqrt(jnp.sum(f * f + b * b, axis=0, keepdims=True) + EPS)
    u1 = (f + b) * sc
    u2 = (f - b) * sc
    quarter = jnp.where((ti & 1) == 0, (1 - (ti & 2)).astype(F32), 0.0)
    kh_ref[...] = jnp.sum(u1 * quarter, axis=0, keepdims=True)
    half = length // 2
    for k, u in enumerate((u1, u2)):
        tmp[...] = u
        u_ref[k, :half, :] = tmp[pl.ds(0, half, stride=2), :].astype(u_ref.dtype)
        u_ref[k, half:, :] = tmp[pl.ds(1, half, stride=2), :].astype(u_ref.dtype)


def hyena_taps(hid, w3, deltas):
    length, fh = hid.shape
    hw = deltas.shape[1]
    n_ord = w3.shape[1] // (2 * hw)
    cw = _pick(hw, 128)
    nc = hw // cw
    return pl.pallas_call(
        _hy_taps_kernel,
        grid=(n_ord, nc),
        in_specs=[pl.BlockSpec((length, fh), lambda o, c: (0, 0)),
                  pl.BlockSpec((fh, cw), lambda o, c: (0, o * 2 * nc + c)),
                  pl.BlockSpec((fh, cw), lambda o, c: (0, o * 2 * nc + nc + c)),
                  pl.BlockSpec((1, cw), lambda o, c: (0, c))],
        out_specs=[pl.BlockSpec((2, length, cw), lambda o, c: (0, 0, o * nc + c)),
                   pl.BlockSpec((1, cw), lambda o, c: (0, o * nc + c))],
        out_shape=[jax.ShapeDtypeStruct((2, length, n_ord * hw), BF16),
                   jax.ShapeDtypeStruct((1, n_ord * hw), F32)],
        scratch_shapes=[pltpu.VMEM((length, cw), F32)],
        compiler_params=_cparams(2), name="hyena_taps",
    )(hid, w3, w3, deltas)


def _dft_halves(fe_ref, fo_ref, xe_r, xo_r, xe_i, xo_i, first):
    e_r = jnp.dot(fe_ref[0], xe_r, preferred_element_type=F32)
    e_i = jnp.dot(fe_ref[1], xe_i, preferred_element_type=F32)
    o_r = jnp.dot(fo_ref[0], xo_r, preferred_element_type=F32)
    o_i = jnp.dot(fo_ref[1], xo_i, preferred_element_type=F32)
    return e_r + o_r, jnp.where(first, e_i, e_i + o_i), e_r - o_r, jnp.where(first, o_i, o_i - e_i)


def _first_row(rows, tile_index):
    return (lax.broadcasted_iota(jnp.int32, (rows, 1), 0) == 0) & (tile_index == 0)


def _spectrum_kernel(fe_ref, fo_ref, u_ref, o_ref):
    half = u_ref.shape[1] // 2
    first = _first_row(o_ref.shape[1], pl.program_id(1))
    planes = _dft_halves(fe_ref, fo_ref, u_ref[0, :half, :], u_ref[0, half:, :],
                         u_ref[1, :half, :], u_ref[1, half:, :], first)
    for k, plane in enumerate(planes):
        o_ref[k] = plane


def hyena_spectrum(fe, fo, u):
    _, length, n = u.shape
    half = length // 2
    tm = _pick(half, 256)
    tn = _pick(n, 512)
    return pl.pallas_call(
        _spectrum_kernel,
        grid=(n // tn, half // tm),
        in_specs=[pl.BlockSpec((2, tm, half), lambda c, i: (0, i, 0)),
                  pl.BlockSpec((2, tm, half), lambda c, i: (0, i, 0)),
                  pl.BlockSpec((2, length, tn), lambda c, i: (0, 0, c))],
        out_specs=pl.BlockSpec((4, tm, tn), lambda c, i: (0, i, c)),
        out_shape=jax.ShapeDtypeStruct((4, half, n), F32),
        compiler_params=_cparams(2), name="hyena_spectrum",
    )(fe, fo, u)


def _conv_fwd_kernel(fe_ref, fo_ref, z_ref, k_ref, kh_ref, y_ref):
    half = z_ref.shape[0] // 2
    first = _first_row(y_ref.shape[2], pl.program_id(1))
    ze = z_ref[:half, :]
    zo = z_ref[half:, :]
    a_r, a_i, b_r, b_i = _dft_halves(fe_ref, fo_ref, ze, zo, ze, zo, first)
    ka_r, ka_i, kb_r, kb_i = k_ref[0], k_ref[1], k_ref[2], k_ref[3]
    kh_r = kh_ref[...]
    ya_r = jnp.where(first, 0.5 * a_r * ka_r, a_r * ka_r - a_i * ka_i)
    yb_r = jnp.where(first, 0.5 * b_r * kb_r, b_r * kb_r - b_i * kb_i)
    ya_i = jnp.where(first, a_i * kh_r - b_i * kb_i, a_r * ka_i + a_i * ka_r)
    yb_i = jnp.where(first, a_i * kb_i + b_i * kh_r, b_r * kb_i + b_i * kb_r)
    y_ref[0, 0] = (ya_r + yb_r).astype(y_ref.dtype)
    y_ref[0, 1] = jnp.where(first, ya_i, ya_i - yb_i).astype(y_ref.dtype)
    y_ref[0, 2] = (ya_r - yb_r).astype(y_ref.dtype)
    y_ref[0, 3] = jnp.where(first, yb_i, ya_i + yb_i).astype(y_ref.dtype)


def conv_forward(fe, fo, z, col0, kspec, kh, order, n_seq, length, hw):
    half = length // 2
    tm = _pick(half, 512)
    return pl.pallas_call(
        _conv_fwd_kernel,
        grid=(n_seq, half // tm),
        in_specs=[pl.BlockSpec((2, tm, half), lambda s, i: (0, i, 0)),
                  pl.BlockSpec((2, tm, half), lambda s, i: (0, i, 0)),
                  pl.BlockSpec((length, hw), lambda s, i: (s, col0 // hw)),
                  pl.BlockSpec((4, tm, hw), lambda s, i: (0, i, order)),
                  pl.BlockSpec((1, hw), lambda s, i: (0, order))],
        out_specs=pl.BlockSpec((1, 4, tm, hw), lambda s, i: (s, 0, i, 0)),
        out_shape=jax.ShapeDtypeStruct((n_seq, 4, half, hw), BF16),
        compiler_params=_cparams(2), name="hyena_conv_fwd",
    )(fe, fo, z, kspec, kh)


def _conv_inv_kernel(ie_ref, io_ref, y_ref, z_ref, gate_ref, skip_ref, *rest, inv_len, token_order):
    skip = skip_ref[0]
    halves = []
    for par, inv_ref in enumerate((ie_ref, io_ref)):
        y = jnp.dot(inv_ref[...], y_ref[0, par], preferred_element_type=F32) * inv_len
        halves.append(gate_ref[0, par].astype(F32) * (y + skip * z_ref[0, par].astype(F32)))
    if not token_order:
        o_ref = rest[-1]
        for par in range(2):
            o_ref[0, par] = halves[par].astype(o_ref.dtype)
    else:
        o_ref, mix = rest[-2:]
        tm = halves[0].shape[0]
        for c0 in range(0, o_ref.shape[1], LANES_V7X):
            cols = slice(c0, c0 + LANES_V7X)
            for par in range(2):
                mix[pl.ds(par, tm, stride=2), :] = halves[par][:, cols]
            o_ref[:, cols] = mix[...].astype(o_ref.dtype)


def conv_inverse(ie, io, pq, z, z_col0, gates, gate_col0, skip, order, n_seq, length, out_dtype,
                 token_order=False, total_rows=None, row0=0, prev=None):
    hw = pq.shape[3]
    half = length // 2
    tm = _pick(half, 512)
    nt = half // tm
    par = lambda a: a.reshape(n_seq, 2, half, a.shape[1])
    in_specs = [pl.BlockSpec((tm, length), lambda s, i: (i, 0)),
                pl.BlockSpec((tm, length), lambda s, i: (i, 0)),
                pl.BlockSpec((1, 2, length, hw), lambda s, i: (s, 0, 0, 0)),
                pl.BlockSpec((1, 2, tm, hw), lambda s, i: (s, 0, i, z_col0 // hw)),
                pl.BlockSpec((1, 2, tm, hw), lambda s, i: (s, 0, i, gate_col0 // hw)),
                pl.BlockSpec((1, 1, hw), lambda s, i: (order, 0, 0))]
    args = [ie, io, pq.reshape(n_seq, 2, length, hw), par(z), par(gates), skip.reshape(skip.shape[0], 1, hw)]
    aliases, scratch = {}, []
    if token_order:
        blk0 = row0 // (2 * tm)
        out_spec = pl.BlockSpec((2 * tm, hw), lambda s, i: (blk0 + s * nt + i, 0))
        out_shape = jax.ShapeDtypeStruct((total_rows, hw), out_dtype)
        scratch = [pltpu.VMEM((2 * tm, LANES_V7X), F32)]
        if prev is not None:
            in_specs.append(pl.BlockSpec(memory_space=pl.ANY))
            aliases = {len(args): 0}
            args.append(prev)
    else:
        out_spec = pl.BlockSpec((1, 2, tm, hw), lambda s, i: (s, 0, i, 0))
        out_shape = jax.ShapeDtypeStruct((n_seq, 2, half, hw), out_dtype)
    out = pl.pallas_call(
        functools.partial(_conv_inv_kernel, inv_len=1.0 / length, token_order=token_order),
        grid=(n_seq, nt),
        in_specs=in_specs,
        out_specs=out_spec,
        out_shape=out_shape,
        scratch_shapes=scratch,
        input_output_aliases=aliases,
        compiler_params=_cparams(2), name="hyena_conv_inv",
    )(*args)
    return out if token_order else out.reshape(n_seq * length, hw)


def _merge_kernel(oa_ref, oh_ref, y_ref, hf_ref, hb_ref, g_ref, wa_ref, wh_ref, wl_ref, o_ref, *, tn):
    d = o_ref.shape[1]
    ol = (jax.nn.gelu(y_ref[...].astype(F32), approximate=True) * (hf_ref[...] + hb_ref[...])).astype(BF16)
    oa = oa_ref[...]
    oh = oh_ref[...]
    for c0 in range(0, d, tn):
        cols = slice(c0, c0 + tn)
        gate = lambda k: _sigmoid(g_ref[:, k * d + c0:k * d + c0 + tn].astype(F32))
        m = (gate(0) * jnp.dot(oa, wa_ref[0, :, cols], preferred_element_type=F32)
             + gate(1) * jnp.dot(oh, wh_ref[0, :, cols], preferred_element_type=F32)
             + gate(2) * jnp.dot(ol, wl_ref[0, :, cols], preferred_element_type=F32))
        o_ref[:, cols] = m.astype(o_ref.dtype)


def merge_branches(o_att, o_hy, p, gates, hf, hb, wa, wh, wl, layer, lru_y_off, tm, rows=None):
    t, aw = o_att.shape
    rows = t if rows is None else rows
    hw = o_hy.shape[1]
    lw = hf.shape[1]
    d = wa.shape[2]
    w_spec = lambda rows: pl.BlockSpec((1, rows, d), lambda i: (layer, 0, 0))
    return pl.pallas_call(
        functools.partial(_merge_kernel, tn=_pick(d, 1024)),
        grid=(rows // tm,),
        in_specs=[pl.BlockSpec((tm, aw), lambda i: (i, 0)),
                  pl.BlockSpec((tm, hw), lambda i: (i, 0)),
                  pl.BlockSpec((tm, lw), lambda i: (i, lru_y_off // lw)),
                  pl.BlockSpec((tm, lw), lambda i: (i, 0)),
                  pl.BlockSpec((tm, lw), lambda i: (i, 0)),
                  pl.BlockSpec((tm, 3 * d), lambda i: (i, 0)),
                  w_spec(aw), w_spec(hw), w_spec(lw)],
        out_specs=pl.BlockSpec((tm, d), lambda i: (i, 0)),
        out_shape=jax.ShapeDtypeStruct((t, d), BF16),
        compiler_params=_cparams(1), name="merge_branches",
    )(o_att, o_hy, p, hf, hb, gates, wa, wh, wl)


def _out_proj_kernel(m_ref, w_ref, x_ref, gate_ref, o_ref):
    o_ref[...] = x_ref[...] + gate_ref[0, 0] * jnp.dot(m_ref[...], w_ref[0],
                                                       preferred_element_type=F32)


def out_proj_residual(m, w, layer, x, mods, which_gate, n_lat, batch, tm, rows=None):
    t, d = x.shape
    return pl.pallas_call(
        _out_proj_kernel,
        grid=((t if rows is None else rows) // tm,),
        in_specs=[pl.BlockSpec((tm, d), lambda i: (i, 0)),
                  pl.BlockSpec((1, d, d), lambda i: (layer, 0, 0)),
                  pl.BlockSpec((tm, d), lambda i: (i, 0)),
                  _mod_spec(which_gate, tm, n_lat, batch, d)],
        out_specs=pl.BlockSpec((tm, d), lambda i: (i, 0)),
        out_shape=jax.ShapeDtypeStruct((t, d), F32),
        compiler_params=_cparams(1), name="out_proj_residual",
    )(m, w, x, mods)


def _router_kernel(x_ref, g_ref, sh_ref, sc_ref, rw_ref, rb_ref, h_ref, meta_ref, cnt_ref):
    h = _normmod(x_ref[...], g_ref[...], sh_ref[0, 0], sc_ref[0, 0])
    _store_tile_rows(h_ref, 0, _pack_bf16_pairs(h))
    w = rw_ref[...]
    h_hi, w_hi = h.astype(BF16), w.astype(BF16)
    h_lo = (h - h_hi.astype(F32)).astype(BF16)
    w_lo = (w - w_hi.astype(F32)).astype(BF16)
    logits = (jnp.dot(h_hi, w_hi, preferred_element_type=F32) + jnp.dot(h_lo, w_hi, preferred_element_type=F32)
              + jnp.dot(h_hi, w_lo, preferred_element_type=F32)) + rb_ref[...]
    lane = lax.broadcasted_iota(jnp.int32, logits.shape, 1)
    valid = lane < N_EXPERTS
    logits = jnp.where(valid, logits, -jnp.inf)
    ex = jnp.exp(logits - jnp.max(logits, axis=-1, keepdims=True))
    scores = ex / jnp.sum(ex, axis=-1, keepdims=True)
    grp = lax.shift_right_logical(lane, int(math.log2(EXPERTS_PER_GROUP)))
    best = jnp.zeros((logits.shape[0], 1), jnp.int32)
    best_max = jnp.max(jnp.where((grp == 0) & valid, scores, -1.0), axis=-1, keepdims=True)
    for g in range(1, N_GROUPS):
        gm = jnp.max(jnp.where((grp == g) & valid, scores, -1.0), axis=-1, keepdims=True)
        better = gm > best_max
        best = jnp.where(better, g, best)
        best_max = jnp.where(better, gm, best_max)
    s1 = jnp.where((grp == best) & valid, scores, -1.0)
    m1 = jnp.max(s1, axis=-1, keepdims=True)
    i1 = jnp.min(jnp.where(s1 == m1, lane, LANES_V7X), axis=-1, keepdims=True)
    s2 = jnp.where(lane == i1, -1.0, s1)
    m2 = jnp.max(s2, axis=-1, keepdims=True)
    i2 = jnp.min(jnp.where(s2 == m2, lane, LANES_V7X), axis=-1, keepdims=True)
    tot = m1 + m2
    @pl.when(pl.program_id(0) == 0)
    def _():
        cnt_ref[...] = jnp.zeros_like(cnt_ref)

    tm = logits.shape[0]
    onehot = ((lane == i1) | (lane == i2)).astype(BF16)
    lower = (lax.broadcasted_iota(jnp.int32, (tm, tm), 0)
             > lax.broadcasted_iota(jnp.int32, (tm, tm), 1)).astype(BF16)
    before = jnp.dot(lower, onehot, preferred_element_type=F32) + cnt_ref[...]
    r1 = jnp.sum(jnp.where(lane == i1, before, 0.0), axis=-1, keepdims=True)
    r2 = jnp.sum(jnp.where(lane == i2, before, 0.0), axis=-1, keepdims=True)
    cnt_ref[...] += jnp.sum(onehot.astype(F32), axis=0, keepdims=True)
    cols = (i1.astype(F32), i2.astype(F32), r1, r2, m1 / tot, m2 / tot)
    meta = jnp.zeros(logits.shape, F32)
    for k, col in enumerate(cols):
        meta = jnp.where(lane == k, col, meta)
    meta_ref[...] = meta


ROUTE_E, ROUTE_RANK, ROUTE_W = 0, 2, 4


def router(x, g, mods, which_shift, rw, rb, n_lat, batch, tm, rows=None):
    t, d = x.shape
    spr = d // 2 // LANES_V7X
    return pl.pallas_call(
        _router_kernel,
        grid=((t if rows is None else rows) // tm,),
        in_specs=[pl.BlockSpec((tm, d), lambda i: (i, 0)),
                  pl.BlockSpec((1, d), lambda i: (0, 0)),
                  _mod_spec(which_shift, tm, n_lat, batch, d),
                  _mod_spec(which_shift + 1, tm, n_lat, batch, d),
                  pl.BlockSpec((d, LANES_V7X), lambda i: (0, 0)),
                  pl.BlockSpec((1, LANES_V7X), lambda i: (0, 0))],
        out_specs=[pl.BlockSpec((tm * spr, LANES_V7X), lambda i: (i, 0)),
                   pl.BlockSpec((tm, LANES_V7X), lambda i: (i, 0)),
                   pl.BlockSpec((1, LANES_V7X), lambda i: (0, 0))],
        out_shape=[jax.ShapeDtypeStruct((t * spr, LANES_V7X), jnp.uint32),
                   jax.ShapeDtypeStruct((t, LANES_V7X), F32),
                   jax.ShapeDtypeStruct((1, LANES_V7X), F32)],
        compiler_params=_cparams(1), name="router",
    )(x, g.reshape(1, d), mods, mods, rw, rb)


def dispatch_plan(meta, counts, n_exp, tile, n_tiles):
    cnt = counts[0, :n_exp].astype(jnp.int32)
    tiles_e = (cnt + tile - 1) // tile
    tile_end = jnp.cumsum(tiles_e)
    row_start = (tile_end - tiles_e) * tile
    experts = meta[:, ROUTE_E:ROUTE_E + 2].astype(jnp.int32)
    ranks = meta[:, ROUTE_RANK:ROUTE_RANK + 2].astype(jnp.int32)
    dest = (row_start[experts] + ranks).reshape(-1)
    tile_expert = jnp.minimum(jnp.sum(jnp.arange(n_tiles)[:, None] >= tile_end[None, :], axis=1),
                              n_exp - 1).astype(jnp.int32)
    zero_start = jnp.minimum(row_start + cnt, (n_tiles - 1) * tile).astype(jnp.int32)
    return dest, tile_expert, tile_end[-1:].astype(jnp.int32), zero_start


def _dispatch_kernel(dest_ref, zero_ref, h_ref, xs_ref, zbuf, sem, zsem, *, spr):
    tm = h_ref.shape[0] // spr
    base = pl.program_id(0) * (2 * tm)
    token = lambda ref, t: ref.at[pl.ds(pl.multiple_of(t * spr, spr), spr)]

    @pl.when(pl.program_id(0) == 0)
    def _():
        zbuf[...] = jnp.zeros_like(zbuf)

        def clear(e, carry):
            start = pl.multiple_of(zero_ref[e] * spr, spr)
            copy = pltpu.make_async_copy(zbuf, xs_ref.at[pl.ds(start, zbuf.shape[0])], zsem)
            copy.start()
            copy.wait()
            return carry

        lax.fori_loop(0, zero_ref.shape[0], clear, 0)

    def issue(r, carry):
        for k in range(2):
            pltpu.make_async_copy(token(h_ref, r), token(xs_ref, dest_ref[base + 2 * r + k]), sem).start()
        return carry

    lax.fori_loop(0, tm, issue, 0, unroll=8)
    for k in range(2):
        pltpu.make_async_copy(h_ref, xs_ref.at[pl.ds(0, tm * spr)], sem).wait()


def moe_dispatch(h, dest, zero_start, n_rows, tile, tm, spr):
    assert spr == SUBLANES_V7X
    t = dest.shape[0] // 2
    return pl.pallas_call(
        functools.partial(_dispatch_kernel, spr=spr),
        grid_spec=pltpu.PrefetchScalarGridSpec(
            num_scalar_prefetch=2, grid=(t // tm,),
            in_specs=[pl.BlockSpec((tm * spr, LANES_V7X), lambda i, dest, zs: (i, 0))],
            out_specs=pl.BlockSpec(memory_space=pl.ANY),
            scratch_shapes=[pltpu.VMEM((tile * spr, LANES_V7X), h.dtype),
                            pltpu.SemaphoreType.DMA, pltpu.SemaphoreType.DMA]),
        out_shape=jax.ShapeDtypeStruct((n_rows * spr, LANES_V7X), h.dtype),
        compiler_params=_cparams(1), name="moe_dispatch",
    )(dest, zero_start, h)


def _experts_kernel(te_ref, nu_ref, xs_ref, wg_ref, wu_ref, wd_ref, ys_ref, wgb, wub, wdb, *, spr):
    j = pl.program_id(0)

    @pl.when(j < nu_ref[0])
    def _():
        @pl.when((j == 0) | (te_ref[j] != te_ref[jnp.maximum(j - 1, 0)]))
        def _():
            wgb[...] = wg_ref[0, 0].astype(BF16)
            wub[...] = wu_ref[0, 0].astype(BF16)
            wdb[...] = wd_ref[0, 0].astype(BF16)

        tile = xs_ref.shape[0] // spr
        x_hi, x_lo = (v.astype(BF16) for v in _unpack_bf16_pairs(_load_tile_rows(xs_ref, 0, tile, spr)))
        half = x_hi.shape[1]

        def proj(w):
            return (jnp.dot(x_hi, w[:half, :], preferred_element_type=F32)
                    + jnp.dot(x_lo, w[half:, :], preferred_element_type=F32))

        gt = proj(wgb)
        act = (gt * _sigmoid(gt)) * proj(wub)
        _store_tile_rows(ys_ref, 0, _pack_bf16_pairs(
            jnp.dot(act.astype(BF16), wdb[...], preferred_element_type=F32)))


def moe_experts(tile_expert, n_used, xs, wg, wu, wd, layer, tile):
    d, de = wg.shape[2], wg.shape[3]
    spr = d // 2 // LANES_V7X
    n_tiles = xs.shape[0] // (tile * spr)
    used = lambda j, nu: jnp.minimum(j, nu[0] - 1)
    w_spec = lambda shape: pl.BlockSpec((1, 1) + shape, lambda j, te, nu: (layer, te[used(j, nu)], 0, 0))
    return pl.pallas_call(
        functools.partial(_experts_kernel, spr=spr),
        grid_spec=pltpu.PrefetchScalarGridSpec(
            num_scalar_prefetch=2, grid=(n_tiles,),
            in_specs=[pl.BlockSpec((tile * spr, LANES_V7X), lambda j, te, nu: (used(j, nu), 0)),
                      w_spec((d, de)), w_spec((d, de)), w_spec((de, d))],
            out_specs=pl.BlockSpec((tile * spr, LANES_V7X), lambda j, te, nu: (used(j, nu), 0)),
            scratch_shapes=[pltpu.VMEM((d, de), BF16), pltpu.VMEM((d, de), BF16),
                            pltpu.VMEM((de, d), BF16)]),
        out_shape=jax.ShapeDtypeStruct(xs.shape, xs.dtype),
        compiler_params=_cparams(1), name="moe_experts",
    )(tile_expert, n_used, xs, wg, wu, wd)


def _combine_kernel(dest_ref, meta_ref, x_ref, gate_ref, ys_ref, o_ref, buf, sem, *, spr):
    i = pl.program_id(0)
    tm = x_ref.shape[0]
    part = lambda slot, k: pl.multiple_of((slot * 2 + k) * (tm * spr), spr)

    def gather(tile, slot):
        base = tile * (2 * tm)

        def issue(r, carry):
            for k in range(2):
                row = dest_ref[base + 2 * r + k]
                pltpu.make_async_copy(ys_ref.at[pl.ds(pl.multiple_of(row * spr, spr), spr)],
                                      buf.at[pl.ds(part(slot, k) + r * spr, spr)], sem.at[slot]).start()
            return carry

        lax.fori_loop(0, tm, issue, 0, unroll=8)

    @pl.when(i == 0)
    def _():
        gather(0, 0)

    @pl.when(i + 1 < pl.num_programs(0))
    def _():
        gather(i + 1, (i + 1) % 2)

    slot = i % 2
    for k in range(2):
        pltpu.make_async_copy(ys_ref.at[pl.ds(0, tm * spr)], buf.at[pl.ds(part(slot, k), tm * spr)],
                              sem.at[slot]).wait()
    meta = meta_ref[...]
    w1 = meta[:, ROUTE_W:ROUTE_W + 1]
    w2 = meta[:, ROUTE_W + 1:ROUTE_W + 2]
    y1_hi, y1_lo = _unpack_bf16_pairs(_load_tile_rows(buf, part(slot, 0), tm, spr))
    y2_hi, y2_lo = _unpack_bf16_pairs(_load_tile_rows(buf, part(slot, 1), tm, spr))
    half = y1_hi.shape[1]
    gate = gate_ref[0, 0]
    o_ref[:, :half] = x_ref[:, :half] + gate[:, :half] * (w1 * y1_hi + w2 * y2_hi)
    o_ref[:, half:] = x_ref[:, half:] + gate[:, half:] * (w1 * y1_lo + w2 * y2_lo)


def moe_combine(dest, meta, x, mods, which_gate, ys, n_lat, batch, tm):
    t, d = x.shape
    spr = d // 2 // LANES_V7X
    return pl.pallas_call(
        functools.partial(_combine_kernel, spr=spr),
        grid_spec=pltpu.PrefetchScalarGridSpec(
            num_scalar_prefetch=1, grid=(dest.shape[0] // 2 // tm,),
            in_specs=[pl.BlockSpec((tm, LANES_V7X), lambda i, dest: (i, 0)),
                      pl.BlockSpec((tm, d), lambda i, dest: (i, 0)),
                      _mod_spec(which_gate, tm, n_lat, batch, d),
                      pl.BlockSpec(memory_space=pl.ANY)],
            out_specs=pl.BlockSpec((tm, d), lambda i, dest: (i, 0)),
            scratch_shapes=[pltpu.VMEM((2 * 2 * tm * spr, LANES_V7X), ys.dtype),
                            pltpu.SemaphoreType.DMA((2,))]),
        out_shape=jax.ShapeDtypeStruct((t, d), F32),
        compiler_params=_cparams(1), name="moe_combine",
    )(dest, meta, x, mods, ys)


def _rope_table(n_tokens):
    rows = n_tokens // GRID_W
    row = jnp.repeat(jnp.arange(rows), GRID_W).astype(F32)
    col = jnp.tile(jnp.arange(GRID_W), rows).astype(F32)
    pairs = HEAD_DIM // 4
    inv = ROPE_THETA ** (-jnp.arange(pairs, dtype=F32) / pairs)
    ang = jnp.concatenate([row[:, None] * inv, col[:, None] * inv], axis=-1)
    cos, sin = jnp.cos(ang), jnp.sin(ang)
    return jnp.concatenate([cos, cos, -sin, sin], axis=-1)


def _trig_kernel(ca_ref, sa_ref, cr_ref, sr_ref, o_ref, *, sign, inverse):
    tm = cr_ref.shape[0]
    n_lanes = ca_ref.shape[1] * LANES_V7X
    cr = cr_ref[...]
    sr = sr_ref[...]
    row = pl.program_id(0) * tm + lax.broadcasted_iota(jnp.int32, (tm, 1), 0)
    alt_row = sign * (1 - 2 * (row & 1)).astype(F32)
    for a in range(n_lanes // LANES_V7X):
        blk = slice(a * LANES_V7X, (a + 1) * LANES_V7X)
        ca = ca_ref[:, a:a + 1]
        sa = sa_ref[:, a:a + 1]
        cos_blk = ca * cr - sa * sr
        nsin_blk = -(sa * cr + ca * sr)
        col = a * LANES_V7X + lax.broadcasted_iota(jnp.int32, (1, LANES_V7X), 1)
        if inverse:
            o_ref[:, blk] = cos_blk.astype(o_ref.dtype)
            o_ref[:, n_lanes + a * LANES_V7X:n_lanes + (a + 1) * LANES_V7X] = (
                jnp.where(col == 0, alt_row, nsin_blk).astype(o_ref.dtype))
        else:
            alt_col = sign * (1 - 2 * (col & 1)).astype(F32)
            o_ref[0, :, blk] = cos_blk.astype(o_ref.dtype)
            o_ref[1, :, blk] = jnp.where(row == 0, alt_col, nsin_blk).astype(o_ref.dtype)


def _dft_tables(length):
    half = length // 2
    n_a = half // LANES_V7X
    tm = _pick(half, 256)
    idx = jnp.arange(half, dtype=jnp.int32)[:, None]
    a_hi = jnp.arange(n_a, dtype=jnp.int32)[None, :] * LANES_V7X
    r_lo = jnp.arange(LANES_V7X, dtype=jnp.int32)[None, :]
    ang = lambda prod: (prod % (2 * length)).astype(F32) * (math.pi / length)

    def table(row_val, lane_hi, lane_lo, sign, inverse):
        out_shape = (half, 2 * half) if inverse else (2, half, half)
        out_block = (tm, 2 * half) if inverse else (2, tm, half)
        out_index = (lambda i: (i, 0)) if inverse else (lambda i: (0, i, 0))
        small = [f(ang(row_val * v)) for v in (lane_hi, lane_lo) for f in (jnp.cos, jnp.sin)]
        return pl.pallas_call(
            functools.partial(_trig_kernel, sign=sign, inverse=inverse),
            grid=(half // tm,),
            in_specs=[pl.BlockSpec((tm, n_a), lambda i: (i, 0)), pl.BlockSpec((tm, n_a), lambda i: (i, 0)),
                      pl.BlockSpec((tm, LANES_V7X), lambda i: (i, 0)),
                      pl.BlockSpec((tm, LANES_V7X), lambda i: (i, 0))],
            out_specs=pl.BlockSpec(out_block, out_index),
            out_shape=jax.ShapeDtypeStruct(out_shape, BF16),
            compiler_params=_cparams(1), name="dft_tables",
        )(*small)

    tables = []
    for par, sign in ((0, 1.0), (1, -1.0)):
        tables.append(table(idx, 2 * a_hi, 2 * r_lo + par, sign, False))
        tables.append(table(2 * idx + par, a_hi, r_lo, sign, True))
    fe, ie, fo, io = tables
    return fe, fo, ie, io


def _hyena_features(length, k_pad):
    t = jnp.linspace(0.0, 1.0, length, dtype=F32)[:, None]
    w = 2.0 * math.pi * jnp.arange(length, dtype=F32)[:, None] / length
    f = jnp.linspace(1e-4, HYENA_BANDS - 1, HYENA_BANDS, dtype=F32)[None, :]
    z = jnp.concatenate([t, jnp.cos(f * w), -jnp.sin(f * w)], axis=-1)
    return jnp.pad(z, ((0, 0), (0, k_pad - z.shape[1])))


def _blockdiag_dense(w):
    nb, bs = w.shape[-3], w.shape[-2]
    eye = jnp.eye(nb, dtype=w.dtype)
    dense = jnp.einsum('...nde,nm->...ndme', w, eye)
    return dense.reshape(w.shape[:-3] + (nb * bs, nb * bs))


def kernel(x, c, ctx, c_ctx, w_ada, b_ada, g_mix, g_ffn, w_in, attn_sink, hy_short_w, hy_short_b, hy_w1, hy_b1, hy_freq, hy_w2, hy_b2, hy_w3, hy_skip, lru_conv_w, lru_conv_b, lru_wa, lru_ba, lru_wx, lru_bx, lru_lambda, w_br_attn, w_br_hy, w_br_lru, w_out, router_w, router_b, exp_w_gate, exp_w_up, exp_w_down, final_g):
    batch, n_lat, d = x.shape
    n_ctx = ctx.shape[1]
    depth = w_ada.shape[0]
    t_lat, t_ctx = batch * n_lat, batch * n_ctx
    hw = hy_skip.shape[2]
    lw = lru_lambda.shape[2]
    fh = hy_w2.shape[1]
    aw = N_Q_HEADS * HEAD_DIM
    kvw = N_KV_HEADS * HEAD_DIM
    q_off = 2 * kvw + lw
    hy_off = q_off + aw
    lru_y_off = hy_off + 3 * hw
    gate_off = lru_y_off + lw
    assert batch + 1 <= MOD_ROWS and t_lat % n_ctx == 0 and n_lat % ATTN_BLOCK == 0
    tile_base = math.gcd(n_lat, t_ctx)
    tm = _pick(tile_base, 512)
    tm_big = _pick(tile_base, 1024)

    xu = jnp.concatenate([x.reshape(t_lat, d), ctx.reshape(t_ctx, d)], axis=0)
    cc = jnp.concatenate([c, c_ctx[None, :], jnp.zeros((MOD_ROWS - batch - 1, d), F32)], axis=0)
    mods_all = ada_tables(cc, w_ada, b_ada).reshape(depth, MOD_ROWS, N_MOD, 1, d)

    rope_tab = _rope_table(n_lat)
    band_bias = _band_bias(n_lat)
    deltas = jnp.abs(jnp.linspace(math.log(HYENA_DECAY_TARGET) / HYENA_FAST_DECAY,
                                  math.log(HYENA_DECAY_TARGET) / HYENA_SLOW_DECAY, hw, dtype=F32))[None, :]
    seqs = []
    for length, row0 in ((n_lat, 0), (n_ctx, t_lat)):
        seqs.append((length, row0, _dft_tables(length), _hyena_features(length, fh)))
    hy_w1p = jnp.pad(hy_w1, ((0, 0), (0, fh - hy_w1.shape[1]), (0, 0)))

    sp = softplus_neg(lru_lambda)
    lru_w = jnp.concatenate([_blockdiag_dense(lru_wa), _blockdiag_dense(lru_wx)], axis=-1).astype(BF16)
    lru_b = jnp.concatenate([lru_ba, lru_bx], axis=-1)[:, :, None, :]
    wa_b, wh_b, wl_b, wo_b = (w.astype(BF16) for w in (w_br_attn, w_br_hy, w_br_lru, w_out))
    rw = jnp.pad(router_w, ((0, 0), (0, LANES_V7X - router_w.shape[1])))
    rb = jnp.pad(router_b, (0, LANES_V7X - router_b.shape[0]))[None, :]
    h_zero = jnp.zeros((batch, 2, lw), F32)
    t_all = t_lat + t_ctx
    n_exp = exp_w_gate.shape[1]
    exp_tile = _pick(2 * t_all, 512)
    n_exp_tiles = (2 * t_all) // exp_tile + n_exp
    tm_cmb = _pick(tile_base, 512)

    for l in range(depth):
        last = l == depth - 1
        rows = t_lat if last else t_all
        mods = mods_all[l]
        h = normmod(xu, g_mix[l], mods, 0, n_lat, batch, tm)
        p = in_proj(h, w_in, l, tm_big, 0, gate_off, BF16)
        gates = in_proj(h, w_in, l, tm_big, gate_off, w_in.shape[2] - gate_off, BF16, rows=rows)

        o_att = latent_attention(p, attn_sink[l], rope_tab, band_bias, batch, n_lat, n_ctx, q_off)
        if not last:
            o_att = context_attention(p, attn_sink[l], o_att, batch, n_lat, n_ctx, q_off)

        xc_c = dwconv(p, lru_conv_w[l], lru_conv_b[l], batch, n_ctx, t_lat, 2 * kvw, lw)
        hf, hb, h_end = rg_lru(xc_c, lru_w, l, lru_b[l], sp[l], h_zero, batch, n_ctx, t_all, t_lat)
        xc_l = dwconv(p, lru_conv_w[l], lru_conv_b[l], batch, n_lat, 0, 2 * kvw, lw)
        hf, hb, _ = rg_lru(xc_l, lru_w, l, lru_b[l], sp[l], h_end, batch, n_lat, t_all, 0, prev=(hf, hb))

        o_hy = None
        for length, row0, (fe, fo, ie, io), feats in seqs[:1] if last else seqs:
            u = dwconv(p, hy_short_w[l], hy_short_b[l], batch, length, row0, hy_off, 3 * hw, parity_order=True)
            hid = hyena_hidden(feats, hy_w1p[l], hy_b1[l], hy_freq[l], hy_w2[l], hy_b2[l])
            taps, kh = hyena_taps(hid, hy_w3[l], deltas)
            kspec = hyena_spectrum(fe, fo, taps)
            pq = conv_forward(fe, fo, u, 0, kspec, kh, 0, batch, length, hw)
            z1 = conv_inverse(ie, io, pq, u, 0, u, hw, hy_skip[l], 0, batch, length, BF16)
            pq = conv_forward(fe, fo, z1, 0, kspec, kh, 1, batch, length, hw)
            o_hy = conv_inverse(ie, io, pq, z1, 0, u, 2 * hw, hy_skip[l], 1, batch, length, BF16,
                                token_order=True, total_rows=t_all, row0=row0, prev=o_hy)

        m = merge_branches(o_att, o_hy, p, gates, hf, hb, wa_b, wh_b, wl_b, l, lru_y_off, tm, rows=rows)
        xu = out_proj_residual(m, wo_b, l, xu, mods, 2, n_lat, batch, tm, rows=rows)

        fl, route, counts = router(xu, g_ffn[l], mods, 3, rw, rb, n_lat, batch, tm, rows=rows)
        dest, tile_expert, n_used, zero_start = dispatch_plan(route[:rows], counts, n_exp, exp_tile, n_exp_tiles)
        xs = moe_dispatch(fl, dest, zero_start, n_exp_tiles * exp_tile, exp_tile, tm, d // 2 // LANES_V7X)
        ys = moe_experts(tile_expert, n_used, xs, exp_w_gate, exp_w_up, exp_w_down, l, exp_tile)
        xu = moe_combine(dest, route, xu, mods, 5, ys, n_lat, batch, tm_cmb)

    return final_norm(xu, final_g, t_lat, tm).reshape(batch, n_lat, d)
```

```python
import functools
import math

import jax
import jax.numpy as jnp
import numpy as np
from jax import lax
from jax.experimental import pallas as pl
from jax.experimental.pallas import tpu as pltpu

F32 = jnp.float32
BF16 = jnp.bfloat16
HIGHEST = lax.Precision.HIGHEST

LANES_V7X = 128
VMEM_LIMIT_V7X = 56 * 1024 * 1024

EPS = 1e-6
GRID_W = 64
HEAD_DIM = 128
N_Q_HEADS = 8
N_KV_HEADS = 2
Q_PER_KV = N_Q_HEADS // N_KV_HEADS
WINDOW = 128
ATTN_BLOCK = 128
ROPE_THETA = 10000.0
HYENA_BANDS = 16
HYENA_DECAY_TARGET = 1e-2
HYENA_FAST_DECAY = 0.3
HYENA_SLOW_DECAY = 1.5
LRU_C = 8.0
N_EXPERTS = 16
N_GROUPS = 4
EXPERTS_PER_GROUP = N_EXPERTS // N_GROUPS
LOG2_E = math.log2(math.e)
N_MOD = 6
MOD_ROWS = 8


def _cparams(n_axes):
    return pltpu.CompilerParams(dimension_semantics=("arbitrary",) * n_axes,
                                vmem_limit_bytes=VMEM_LIMIT_V7X)


def _sigmoid(x):
    return 0.5 * (1.0 + jnp.tanh(0.5 * x))


def _pack_bf16_pairs(x):
    half = x.shape[1] // 2
    hi = lax.bitcast_convert_type(x[:, :half].astype(BF16).astype(F32), jnp.uint32)
    lo = lax.bitcast_convert_type(x[:, half:].astype(BF16).astype(F32), jnp.uint32)
    return hi | (lo >> 16)


def _unpack_bf16_pairs(w):
    hi = lax.bitcast_convert_type(w & jnp.uint32(0xFFFF0000), F32)
    lo = lax.bitcast_convert_type(w << 16, F32)
    return hi, lo


SUBLANES_V7X = 8


def _store_tile_rows(ref, row0, x):
    rows, width = x.shape
    s_per_row = width // LANES_V7X
    for s in range(s_per_row):
        ref[pl.ds(row0 + s, rows, stride=s_per_row), :] = x[:, s * LANES_V7X:(s + 1) * LANES_V7X]


def _load_tile_rows(ref, row0, rows, s_per_row):
    return jnp.concatenate([ref[pl.ds(row0 + s, rows, stride=s_per_row), :] for s in range(s_per_row)], axis=1)


def _pick(n, cap):
    t = cap
    while n % t:
        t //= 2
    return t


def _ada_kernel(c_ref, w_ref, b_ref, o_ref):
    c = c_ref[...]
    s = (c * jax.nn.sigmoid(c)).astype(BF16)
    o_ref[0] = jnp.dot(s, w_ref[0].astype(BF16), preferred_element_type=F32) + b_ref[0]


def ada_tables(cc, w_ada, b_ada):
    depth, d, n6 = w_ada.shape
    tn = _pick(n6, 1024)
    return pl.pallas_call(
        _ada_kernel,
        grid=(depth, n6 // tn),
        in_specs=[pl.BlockSpec((MOD_ROWS, d), lambda l, j: (0, 0)),
                  pl.BlockSpec((1, d, tn), lambda l, j: (l, 0, j)),
                  pl.BlockSpec((1, 1, tn), lambda l, j: (l, 0, j))],
        out_specs=pl.BlockSpec((1, MOD_ROWS, tn), lambda l, j: (l, 0, j)),
        out_shape=jax.ShapeDtypeStruct((depth, MOD_ROWS, n6), F32),
        compiler_params=_cparams(2), name="ada_tables",
    )(cc, w_ada, b_ada.reshape(depth, 1, n6))


def _mod_spec(which, tm, n_lat, batch, d):
    return pl.BlockSpec((1, 1, 1, d),
                        lambda i, *_: (jnp.minimum((i * tm) // n_lat, batch), which, 0, 0))


def _normmod(x, g, shift, scale):
    y = x * lax.rsqrt(jnp.mean(x * x, axis=-1, keepdims=True) + EPS) * g
    return y * (1.0 + scale) + shift


def _normmod_kernel(x_ref, g_ref, sh_ref, sc_ref, o_ref):
    o_ref[...] = _normmod(x_ref[...], g_ref[...], sh_ref[0, 0], sc_ref[0, 0]).astype(o_ref.dtype)


def normmod(x, g, mods, which_shift, n_lat, batch, tm):
    t, d = x.shape
    return pl.pallas_call(
        _normmod_kernel,
        grid=(t // tm,),
        in_specs=[pl.BlockSpec((tm, d), lambda i: (i, 0)),
                  pl.BlockSpec((1, d), lambda i: (0, 0)),
                  _mod_spec(which_shift, tm, n_lat, batch, d),
                  _mod_spec(which_shift + 1, tm, n_lat, batch, d)],
        out_specs=pl.BlockSpec((tm, d), lambda i: (i, 0)),
        out_shape=jax.ShapeDtypeStruct((t, d), BF16),
        compiler_params=_cparams(1), name="normmod",
    )(x, g.reshape(1, d), mods, mods)


def _final_norm_kernel(x_ref, g_ref, o_ref):
    x = x_ref[...]
    o_ref[...] = x * lax.rsqrt(jnp.mean(x * x, axis=-1, keepdims=True) + EPS) * g_ref[...]


def final_norm(x, g, rows, tm):
    d = x.shape[1]
    return pl.pallas_call(
        _final_norm_kernel,
        grid=(rows // tm,),
        in_specs=[pl.BlockSpec((tm, d), lambda i: (i, 0)), pl.BlockSpec((1, d), lambda i: (0, 0))],
        out_specs=pl.BlockSpec((tm, d), lambda i: (i, 0)),
        out_shape=jax.ShapeDtypeStruct((rows, d), F32),
        compiler_params=_cparams(1), name="final_norm",
    )(x, g.reshape(1, d))


def _proj_kernel(a_ref, w_ref, o_ref, wb_ref):
    @pl.when(pl.program_id(1) == 0)
    def _():
        wb_ref[...] = w_ref[0].astype(BF16)

    o_ref[...] = jnp.dot(a_ref[...], wb_ref[...], preferred_element_type=F32).astype(o_ref.dtype)


def in_proj(a, w, layer, tm, col0, n, out_dtype, rows=None):
    t, k = a.shape
    tn = _pick(math.gcd(n, col0) if col0 else n, 1024)
    return pl.pallas_call(
        _proj_kernel,
        grid=(n // tn, (t if rows is None else rows) // tm),
        in_specs=[pl.BlockSpec((tm, k), lambda j, i: (i, 0)),
                  pl.BlockSpec((1, k, tn), lambda j, i: (layer, 0, col0 // tn + j))],
        out_specs=pl.BlockSpec((tm, tn), lambda j, i: (i, j)),
        out_shape=jax.ShapeDtypeStruct((t, n), out_dtype),
        scratch_shapes=[pltpu.VMEM((k, tn), BF16)],
        compiler_params=_cparams(2), name="in_proj",
    )(a, w)


def _rope(x, tab):
    return x * tab[:, :HEAD_DIM] + pltpu.roll(x, HEAD_DIM // 2, axis=1) * tab[:, HEAD_DIM:]


def _sink_column(sink_ref, h, rows):
    r = lax.broadcasted_iota(jnp.int32, (Q_PER_KV * rows, 1), 0)
    col = jnp.full((Q_PER_KV * rows, 1), sink_ref[h * Q_PER_KV], F32)
    for g in range(1, Q_PER_KV):
        col = jnp.where(r >= g * rows, sink_ref[h * Q_PER_KV + g], col)
    return col


def _nt(a, b):
    return lax.dot_general(a, b, (((1,), (1,)), ((), ())), preferred_element_type=F32)


def _band_bias(n_lat):
    blk = ATTN_BLOCK
    single = n_lat == blk
    qi = (jnp.arange(Q_PER_KV * blk) % blk)[:, None]
    kj = jnp.arange(3 * blk)[None, :]
    band = jnp.abs(qi + blk - kj) <= WINDOW
    kinds = []
    for no_prev, no_next in ((True, single), (False, False), (single, True)):
        ok = band & ((kj >= blk) | (not no_prev)) & ((kj < 2 * blk) | (not no_next))
        kinds.append(jnp.where(ok, 0.0, -jnp.inf))
    return jnp.stack(kinds).astype(F32)


def _lat_attn_kernel(sink_ref, q_ref, kvm_ref, kv0_ref, kvp_ref, kvc_ref, tm_ref, t0_ref, tp_ref, bias_ref,
                     o_ref):
    blk = ATTN_BLOCK
    kv_w = N_KV_HEADS * HEAD_DIM
    scale = HEAD_DIM ** -0.5
    tabs = (tm_ref[...], t0_ref[...], tp_ref[...])
    kvs = (kvm_ref, kv0_ref, kvp_ref)
    for h in range(N_KV_HEADS):
        ks = slice(h * HEAD_DIM, (h + 1) * HEAD_DIM)
        vs = slice(kv_w + h * HEAD_DIM, kv_w + (h + 1) * HEAD_DIM)
        k_loc = jnp.concatenate([_rope(kvs[n][:, ks].astype(F32), tabs[n]) for n in range(3)],
                                axis=0).astype(BF16)
        v_loc = jnp.concatenate([kvs[n][:, vs] for n in range(3)], axis=0).astype(BF16)
        k_ctx = kvc_ref[:, ks].astype(BF16)
        v_ctx = kvc_ref[:, vs].astype(BF16)
        q4 = jnp.concatenate(
            [_rope(q_ref[:, (h * Q_PER_KV + g) * HEAD_DIM:(h * Q_PER_KV + g + 1) * HEAD_DIM].astype(F32), tabs[1])
             * (scale * LOG2_E) for g in range(Q_PER_KV)], axis=0).astype(BF16)
        s_loc = _nt(q4, k_loc) + bias_ref[0]
        s_ctx = _nt(q4, k_ctx)
        sink = _sink_column(sink_ref, h, blk) * LOG2_E
        m = jnp.maximum(jnp.maximum(jnp.max(s_loc, axis=-1, keepdims=True),
                                    jnp.max(s_ctx, axis=-1, keepdims=True)), sink)
        p_loc = jnp.exp2(s_loc - m)
        p_ctx = jnp.exp2(s_ctx - m)
        den = (jnp.sum(p_loc, axis=-1, keepdims=True) + jnp.sum(p_ctx, axis=-1, keepdims=True)
               + jnp.exp2(sink - m))
        o = (jnp.dot(p_loc.astype(BF16), v_loc, preferred_element_type=F32)
             + jnp.dot(p_ctx.astype(BF16), v_ctx, preferred_element_type=F32)) / den
        for g in range(Q_PER_KV):
            hq = h * Q_PER_KV + g
            o_ref[:, hq * HEAD_DIM:(hq + 1) * HEAD_DIM] = o[g * blk:(g + 1) * blk].astype(o_ref.dtype)


def latent_attention(p, sink, rope_tab, band_bias, batch, n_lat, n_ctx, q_off):
    blk = ATTN_BLOCK
    nb = n_lat // blk
    aw = N_Q_HEADS * HEAD_DIM
    kvw = 2 * N_KV_HEADS * HEAD_DIM
    ctx_blk0 = (batch * n_lat) // n_ctx
    kv_spec = lambda off: pl.BlockSpec(
        (blk, kvw), lambda b, i: (b * nb + jnp.clip(i + off, 0, nb - 1), 0))
    tab_spec = lambda off: pl.BlockSpec(
        (blk, 2 * HEAD_DIM), lambda b, i: (jnp.clip(i + off, 0, nb - 1), 0))
    bias_kind = lambda b, i: (jnp.where(i == 0, 0, jnp.where(i == nb - 1, 2, 1)), 0, 0)
    return pl.pallas_call(
        _lat_attn_kernel,
        grid=(batch, nb),
        in_specs=[pl.BlockSpec(memory_space=pltpu.SMEM),
                  pl.BlockSpec((blk, aw), lambda b, i: (b * nb + i, q_off // aw)),
                  kv_spec(-1), kv_spec(0), kv_spec(1),
                  pl.BlockSpec((n_ctx, kvw), lambda b, i: (ctx_blk0 + b, 0)),
                  tab_spec(-1), tab_spec(0), tab_spec(1),
                  pl.BlockSpec((1, Q_PER_KV * blk, 3 * blk), bias_kind)],
        out_specs=pl.BlockSpec((blk, aw), lambda b, i: (b * nb + i, 0)),
        out_shape=jax.ShapeDtypeStruct((p.shape[0], aw), BF16),
        compiler_params=_cparams(2), name="latent_attention",
    )(sink, p, p, p, p, p, rope_tab, rope_tab, rope_tab, band_bias)


def _ctx_attn_kernel(sink_ref, q_ref, kv_ref, o_lat_ref, o_ref):
    del o_lat_ref
    rows = q_ref.shape[0]
    kv_w = N_KV_HEADS * HEAD_DIM
    scale = HEAD_DIM ** -0.5
    for h in range(N_KV_HEADS):
        k = kv_ref[:, h * HEAD_DIM:(h + 1) * HEAD_DIM].astype(BF16)
        v = kv_ref[:, kv_w + h * HEAD_DIM:kv_w + (h + 1) * HEAD_DIM].astype(BF16)
        q4 = jnp.concatenate(
            [q_ref[:, (h * Q_PER_KV + g) * HEAD_DIM:(h * Q_PER_KV + g + 1) * HEAD_DIM]
             for g in range(Q_PER_KV)], axis=0).astype(BF16)
        s = _nt(q4, k) * scale
        sink = _sink_column(sink_ref, h, rows)
        m = jnp.maximum(jnp.max(s, axis=-1, keepdims=True), sink)
        p = jnp.exp(s - m)
        den = jnp.sum(p, axis=-1, keepdims=True) + jnp.exp(sink - m)
        o = jnp.dot(p.astype(BF16), v, preferred_element_type=F32) / den
        for g in range(Q_PER_KV):
            hq = h * Q_PER_KV + g
            o_ref[:, hq * HEAD_DIM:(hq + 1) * HEAD_DIM] = o[g * rows:(g + 1) * rows].astype(o_ref.dtype)


def context_attention(p, sink, o_lat, batch, n_lat, n_ctx, q_off):
    aw = N_Q_HEADS * HEAD_DIM
    kvw = 2 * N_KV_HEADS * HEAD_DIM
    ctx_blk0 = (batch * n_lat) // n_ctx
    return pl.pallas_call(
        _ctx_attn_kernel,
        grid=(batch,),
        in_specs=[pl.BlockSpec(memory_space=pltpu.SMEM),
                  pl.BlockSpec((n_ctx, aw), lambda b: (ctx_blk0 + b, q_off // aw)),
                  pl.BlockSpec((n_ctx, kvw), lambda b: (ctx_blk0 + b, 0)),
                  pl.BlockSpec(memory_space=pl.ANY)],
        out_specs=pl.BlockSpec((n_ctx, aw), lambda b: (ctx_blk0 + b, 0)),
        out_shape=jax.ShapeDtypeStruct(o_lat.shape, o_lat.dtype),
        input_output_aliases={3: 0},
        compiler_params=_cparams(1), name="context_attention",
    )(sink, p, p, o_lat)


def _dwconv_kernel(u_ref, w_ref, b_ref, o_ref, *scratch, width):
    x = u_ref[...].astype(F32)
    length = x.shape[0]
    left = width // 2
    t = lax.broadcasted_iota(jnp.int32, (length, 1), 0)
    acc = jnp.broadcast_to(b_ref[...], x.shape)
    for j in range(width):
        s = j - left
        if s == 0:
            xs = x
        else:
            xs = pltpu.roll(x, (-s) % length, axis=0)
            xs = jnp.where((t + s >= 0) & (t + s < length), xs, 0.0)
        acc = acc + xs * w_ref[j:j + 1, :]
    if not scratch:
        o_ref[...] = acc
    else:
        half = length // 2
        scratch[0][...] = acc
        o_ref[:half, :] = scratch[0][pl.ds(0, half, stride=2), :].astype(o_ref.dtype)
        o_ref[half:, :] = scratch[0][pl.ds(1, half, stride=2), :].astype(o_ref.dtype)


def dwconv(p, w, b, n_seq, length, row0, col0, n_ch, parity_order=False):
    width = w.shape[0]
    cb = LANES_V7X if parity_order else _pick(n_ch, 256)
    return pl.pallas_call(
        functools.partial(_dwconv_kernel, width=width),
        grid=(n_seq, n_ch // cb),
        in_specs=[pl.BlockSpec((length, cb), lambda s, c: (row0 // length + s, col0 // cb + c)),
                  pl.BlockSpec((width, cb), lambda s, c: (0, c)),
                  pl.BlockSpec((1, cb), lambda s, c: (0, c))],
        out_specs=pl.BlockSpec((length, cb), lambda s, c: (s, c)),
        out_shape=jax.ShapeDtypeStruct((n_seq * length, n_ch), BF16 if parity_order else F32),
        scratch_shapes=[pltpu.VMEM((length, cb), F32)] if parity_order else [],
        compiler_params=_cparams(2), name="dwconv",
    )(p, w, b.reshape(1, n_ch))


def _lru_kernel(xf_ref, xb_ref, w_ref, bias_ref, sp_ref, h0_ref, *rest):
    hf_ref, hb_ref, hl_ref, af, bf, ab, bb, carry = rest[-8:]
    j = pl.program_id(1)
    width = xf_ref.shape[1]

    @pl.when(j == 0)
    def _():
        carry[...] = h0_ref[0]

    def gates(x, d):
        g = jnp.dot(x.astype(BF16), w_ref[0, d], preferred_element_type=F32) + bias_ref[d]
        a = jnp.exp2((-0.5 * LRU_C * LOG2_E * sp_ref[d]) * (1.0 + jnp.tanh(0.5 * g[:, :width])))
        gx = (1.0 + jnp.tanh(0.5 * g[:, width:])) * (0.5 * x)
        return a, jnp.sqrt(1.0 - a * a) * gx

    af[...], bf[...] = gates(xf_ref[...], 0)
    ab[...], bb[...] = gates(xb_ref[...], 1)
    tl = af.shape[0]

    def body(t, hs):
        hf, hb = hs
        hf = af[pl.ds(t, 1), :] * hf + bf[pl.ds(t, 1), :]
        hf_ref[pl.ds(t, 1), :] = hf
        tb = tl - 1 - t
        hb = ab[pl.ds(tb, 1), :] * hb + bb[pl.ds(tb, 1), :]
        hb_ref[pl.ds(tb, 1), :] = hb
        return hf, hb

    hf, hb = lax.fori_loop(0, tl, body, (carry[0:1, :], carry[1:2, :]), unroll=8)
    carry[0:1, :] = hf
    carry[1:2, :] = hb
    hl_ref[0] = carry[...]


def rg_lru(xc, w_gates, layer, b_gates, softplus_neg_lam, h0, n_seq, length, total_rows, row0, prev=None):
    width = xc.shape[1]
    tl = _pick(length, 512)
    nc = length // tl
    blk0 = row0 // tl
    in_specs = [pl.BlockSpec((tl, width), lambda s, j: (s * nc + j, 0)),
                pl.BlockSpec((tl, width), lambda s, j: (s * nc + nc - 1 - j, 0)),
                pl.BlockSpec((1, 2, width, 2 * width), lambda s, j: (layer, 0, 0, 0)),
                pl.BlockSpec((2, 1, 2 * width), lambda s, j: (0, 0, 0)),
                pl.BlockSpec((2, 1, width), lambda s, j: (0, 0, 0)),
                pl.BlockSpec((1, 2, width), lambda s, j: (s, 0, 0))]
    args = [xc, xc, w_gates, b_gates, softplus_neg_lam, h0]
    aliases = {}
    if prev is not None:
        in_specs += [pl.BlockSpec(memory_space=pl.ANY)] * 2
        aliases = {len(args): 0, len(args) + 1: 1}
        args += list(prev)
    return pl.pallas_call(
        _lru_kernel,
        grid=(n_seq, nc),
        in_specs=in_specs,
        out_specs=[pl.BlockSpec((tl, width), lambda s, j: (blk0 + s * nc + j, 0)),
                   pl.BlockSpec((tl, width), lambda s, j: (blk0 + s * nc + nc - 1 - j, 0)),
                   pl.BlockSpec((1, 2, width), lambda s, j: (s, 0, 0))],
        out_shape=[jax.ShapeDtypeStruct((total_rows, width), F32),
                   jax.ShapeDtypeStruct((total_rows, width), F32),
                   jax.ShapeDtypeStruct((n_seq, 2, width), F32)],
        scratch_shapes=[pltpu.VMEM((tl, width), F32)] * 4 + [pltpu.VMEM((2, width), F32)],
        input_output_aliases=aliases,
        compiler_params=_cparams(2), name="rg_lru",
    )(*args)


def _softplus_kernel(x_ref, o_ref):
    x = -x_ref[...]
    o_ref[...] = jnp.maximum(x, 0.0) + jnp.log(1.0 + jnp.exp(-jnp.abs(x)))


def softplus_neg(lam):
    depth, two, width = lam.shape
    x = lam.reshape(depth * two, width)
    out = pl.pallas_call(
        _softplus_kernel,
        out_shape=jax.ShapeDtypeStruct(x.shape, F32), name="softplus_neg",
    )(x)
    return out.reshape(depth, two, 1, width)


def _hy_hidden_kernel(z_ref, w1_ref, b1_ref, fr_ref, w2_ref, b2_ref, o_ref):
    fr = fr_ref[...]
    h = jnp.sin(fr * (jnp.dot(z_ref[...], w1_ref[...], precision=HIGHEST,
                              preferred_element_type=F32) + b1_ref[...]))
    o_ref[...] = jnp.sin(fr * (jnp.dot(h, w2_ref[...], precision=HIGHEST,
                                       preferred_element_type=F32) + b2_ref[...]))


def hyena_hidden(z, w1, b1, freq, w2, b2):
    length = z.shape[0]
    fh = w2.shape[0]
    return pl.pallas_call(
        _hy_hidden_kernel,
        out_shape=jax.ShapeDtypeStruct((length, fh), F32), name="hyena_hidden",
    )(z, w1, b1.reshape(1, fh), freq.reshape(1, fh), w2, b2.reshape(1, fh))


def _hy_taps_kernel(hid_ref, wf_ref, wb_ref, delta_ref, u_ref, kh_ref, tmp):
    hid = hid_ref[...]
    length = hid.shape[0]
    hf = jnp.dot(hid, wf_ref[...], precision=HIGHEST, preferred_element_type=F32)
    hb = jnp.dot(hid, wb_ref[...], precision=HIGHEST, preferred_element_type=F32)
    ti = lax.broadcasted_iota(jnp.int32, (length, 1), 0)
    win = jnp.exp(-(ti.astype(F32) * (1.0 / (length - 1))) * delta_ref[...])
    f = hf * win
    b = jnp.where(ti >= 1, hb * win, 0.0)
    sc = lax.rsqrt(jnp.sum(f * f + b * b, axis=0, keepdims=True) + EPS)
    u1 = (f + b) * sc
    u2 = (f - b) * sc
    quarter = jnp.where((ti & 1) == 0, (1 - (ti & 2)).astype(F32), 0.0)
    kh_ref[...] = jnp.sum(u1 * quarter, axis=0, keepdims=True)
    half = length // 2
    for k, u in enumerate((u1, u2)):
        tmp[...] = u
        u_ref[k, :half, :] = tmp[pl.ds(0, half, stride=2), :].astype(u_ref.dtype)
        u_ref[k, half:, :] = tmp[pl.ds(1, half, stride=2), :].astype(u_ref.dtype)


def hyena_taps(hid, w3, deltas):
    length, fh = hid.shape
    hw = deltas.shape[1]
    n_ord = w3.shape[1] // (2 * hw)
    cw = _pick(hw, 128)
    nc = hw // cw
    return pl.pallas_call(
        _hy_taps_kernel,
        grid=(n_ord, nc),
        in_specs=[pl.BlockSpec((length, fh), lambda o, c: (0, 0)),
                  pl.BlockSpec((fh, cw), lambda o, c: (0, o * 2 * nc + c)),
                  pl.BlockSpec((fh, cw), lambda o, c: (0, o * 2 * nc + nc + c)),
                  pl.BlockSpec((1, cw), lambda o, c: (0, c))],
        out_specs=[pl.BlockSpec((2, length, cw), lambda o, c: (0, 0, o * nc + c)),
                   pl.BlockSpec((1, cw), lambda o, c: (0, o * nc + c))],
        out_shape=[jax.ShapeDtypeStruct((2, length, n_ord * hw), BF16),
                   jax.ShapeDtypeStruct((1, n_ord * hw), F32)],
        scratch_shapes=[pltpu.VMEM((length, cw), F32)],
        compiler_params=_cparams(2), name="hyena_taps",
    )(hid, w3, w3, deltas)


def _dft_halves(fe_ref, fo_ref, xe_r, xo_r, xe_i, xo_i, first):
    e_r = jnp.dot(fe_ref[0], xe_r, preferred_element_type=F32)
    e_i = jnp.dot(fe_ref[1], xe_i, preferred_element_type=F32)
    o_r = jnp.dot(fo_ref[0], xo_r, preferred_element_type=F32)
    o_i = jnp.dot(fo_ref[1], xo_i, preferred_element_type=F32)
    return e_r + o_r, jnp.where(first, e_i, e_i + o_i), e_r - o_r, jnp.where(first, o_i, o_i - e_i)


def _first_row(rows, tile_index):
    return (lax.broadcasted_iota(jnp.int32, (rows, 1), 0) == 0) & (tile_index == 0)


def _spectrum_kernel(fe_ref, fo_ref, u_ref, o_ref):
    half = u_ref.shape[1] // 2
    first = _first_row(o_ref.shape[1], pl.program_id(1))
    planes = _dft_halves(fe_ref, fo_ref, u_ref[0, :half, :], u_ref[0, half:, :],
                         u_ref[1, :half, :], u_ref[1, half:, :], first)
    for k, plane in enumerate(planes):
        o_ref[k] = plane


def hyena_spectrum(fe, fo, u):
    _, length, n = u.shape
    half = length // 2
    tm = _pick(half, 256)
    tn = _pick(n, 512)
    return pl.pallas_call(
        _spectrum_kernel,
        grid=(n // tn, half // tm),
        in_specs=[pl.BlockSpec((2, tm, half), lambda c, i: (0, i, 0)),
                  pl.BlockSpec((2, tm, half), lambda c, i: (0, i, 0)),
                  pl.BlockSpec((2, length, tn), lambda c, i: (0, 0, c))],
        out_specs=pl.BlockSpec((4, tm, tn), lambda c, i: (0, i, c)),
        out_shape=jax.ShapeDtypeStruct((4, half, n), F32),
        compiler_params=_cparams(2), name="hyena_spectrum",
    )(fe, fo, u)


def _conv_fwd_kernel(fe_ref, fo_ref, z_ref, k_ref, kh_ref, y_ref):
    half = z_ref.shape[0] // 2
    first = _first_row(y_ref.shape[2], pl.program_id(1))
    ze = z_ref[:half, :]
    zo = z_ref[half:, :]
    a_r, a_i, b_r, b_i = _dft_halves(fe_ref, fo_ref, ze, zo, ze, zo, first)
    ka_r, ka_i, kb_r, kb_i = k_ref[0], k_ref[1], k_ref[2], k_ref[3]
    kh_r = kh_ref[...]
    ya_r = jnp.where(first, 0.5 * a_r * ka_r, a_r * ka_r - a_i * ka_i)
    yb_r = jnp.where(first, 0.5 * b_r * kb_r, b_r * kb_r - b_i * kb_i)
    ya_i = jnp.where(first, a_i * kh_r - b_i * kb_i, a_r * ka_i + a_i * ka_r)
    yb_i = jnp.where(first, a_i * kb_i + b_i * kh_r, b_r * kb_i + b_i * kb_r)
    y_ref[0, 0] = (ya_r + yb_r).astype(y_ref.dtype)
    y_ref[0, 1] = jnp.where(first, ya_i, ya_i - yb_i).astype(y_ref.dtype)
    y_ref[0, 2] = (ya_r - yb_r).astype(y_ref.dtype)
    y_ref[0, 3] = jnp.where(first, yb_i, ya_i + yb_i).astype(y_ref.dtype)


def conv_forward(fe, fo, z, col0, kspec, kh, order, n_seq, length, hw):
    half = length // 2
    tm = _pick(half, 512)
    return pl.pallas_call(
        _conv_fwd_kernel,
        grid=(n_seq, half // tm),
        in_specs=[pl.BlockSpec((2, tm, half), lambda s, i: (0, i, 0)),
                  pl.BlockSpec((2, tm, half), lambda s, i: (0, i, 0)),
                  pl.BlockSpec((length, hw), lambda s, i: (s, col0 // hw)),
                  pl.BlockSpec((4, tm, hw), lambda s, i: (0, i, order)),
                  pl.BlockSpec((1, hw), lambda s, i: (0, order))],
        out_specs=pl.BlockSpec((1, 4, tm, hw), lambda s, i: (s, 0, i, 0)),
        out_shape=jax.ShapeDtypeStruct((n_seq, 4, half, hw), BF16),
        compiler_params=_cparams(2), name="hyena_conv_fwd",
    )(fe, fo, z, kspec, kh)


def _conv_inv_kernel(ie_ref, io_ref, y_ref, z_ref, gate_ref, skip_ref, *rest, inv_len, token_order):
    skip = skip_ref[0]
    halves = []
    for par, inv_ref in enumerate((ie_ref, io_ref)):
        y = jnp.dot(inv_ref[...], y_ref[0, par], preferred_element_type=F32) * inv_len
        halves.append(gate_ref[0, par].astype(F32) * (y + skip * z_ref[0, par].astype(F32)))
    if not token_order:
        o_ref = rest[-1]
        for par in range(2):
            o_ref[0, par] = halves[par].astype(o_ref.dtype)
    else:
        o_ref, mix = rest[-2:]
        tm = halves[0].shape[0]
        for c0 in range(0, o_ref.shape[1], LANES_V7X):
            cols = slice(c0, c0 + LANES_V7X)
            for par in range(2):
                mix[pl.ds(par, tm, stride=2), :] = halves[par][:, cols]
            o_ref[:, cols] = mix[...].astype(o_ref.dtype)


def conv_inverse(ie, io, pq, z, z_col0, gates, gate_col0, skip, order, n_seq, length, out_dtype,
                 token_order=False, total_rows=None, row0=0, prev=None):
    hw = pq.shape[3]
    half = length // 2
    tm = _pick(half, 512)
    nt = half // tm
    par = lambda a: a.reshape(n_seq, 2, half, a.shape[1])
    in_specs = [pl.BlockSpec((tm, length), lambda s, i: (i, 0)),
                pl.BlockSpec((tm, length), lambda s, i: (i, 0)),
                pl.BlockSpec((1, 2, length, hw), lambda s, i: (s, 0, 0, 0)),
                pl.BlockSpec((1, 2, tm, hw), lambda s, i: (s, 0, i, z_col0 // hw)),
                pl.BlockSpec((1, 2, tm, hw), lambda s, i: (s, 0, i, gate_col0 // hw)),
                pl.BlockSpec((1, 1, hw), lambda s, i: (order, 0, 0))]
    args = [ie, io, pq.reshape(n_seq, 2, length, hw), par(z), par(gates), skip.reshape(skip.shape[0], 1, hw)]
    aliases, scratch = {}, []
    if token_order:
        blk0 = row0 // (2 * tm)
        out_spec = pl.BlockSpec((2 * tm, hw), lambda s, i: (blk0 + s * nt + i, 0))
        out_shape = jax.ShapeDtypeStruct((total_rows, hw), out_dtype)
        scratch = [pltpu.VMEM((2 * tm, LANES_V7X), F32)]
        if prev is not None:
            in_specs.append(pl.BlockSpec(memory_space=pl.ANY))
            aliases = {len(args): 0}
            args.append(prev)
    else:
        out_spec = pl.BlockSpec((1, 2, tm, hw), lambda s, i: (s, 0, i, 0))
        out_shape = jax.ShapeDtypeStruct((n_seq, 2, half, hw), out_dtype)
    out = pl.pallas_call(
        functools.partial(_conv_inv_kernel, inv_len=1.0 / length, token_order=token_order),
        grid=(n_seq, nt),
        in_specs=in_specs,
        out_specs=out_spec,
        out_shape=out_shape,
        scratch_shapes=scratch,
        input_output_aliases=aliases,
        compiler_params=_cparams(2), name="hyena_conv_inv",
    )(*args)
    return out if token_order else out.reshape(n_seq * length, hw)


def _merge_kernel(oa_ref, oh_ref, y_ref, hf_ref, hb_ref, g_ref, wa_ref, wh_ref, wl_ref, o_ref, *, tn):
    d = o_ref.shape[1]
    ol = (jax.nn.gelu(y_ref[...].astype(F32), approximate=True) * (hf_ref[...] + hb_ref[...])).astype(BF16)
    oa = oa_ref[...]
    oh = oh_ref[...]
    for c0 in range(0, d, tn):
        cols = slice(c0, c0 + tn)
        gate = lambda k: _sigmoid(g_ref[:, k * d + c0:k * d + c0 + tn].astype(F32))
        m = (gate(0) * jnp.dot(oa, wa_ref[0, :, cols], preferred_element_type=F32)
             + gate(1) * jnp.dot(oh, wh_ref[0, :, cols], preferred_element_type=F32)
             + gate(2) * jnp.dot(ol, wl_ref[0, :, cols], preferred_element_type=F32))
        o_ref[:, cols] = m.astype(o_ref.dtype)


def merge_branches(o_att, o_hy, p, gates, hf, hb, wa, wh, wl, layer, lru_y_off, tm, rows=None):
    t, aw = o_att.shape
    rows = t if rows is None else rows
    hw = o_hy.shape[1]
    lw = hf.shape[1]
    d = wa.shape[2]
    w_spec = lambda rows: pl.BlockSpec((1, rows, d), lambda i: (layer, 0, 0))
    return pl.pallas_call(
        functools.partial(_merge_kernel, tn=_pick(d, 1024)),
        grid=(rows // tm,),
        in_specs=[pl.BlockSpec((tm, aw), lambda i: (i, 0)),
                  pl.BlockSpec((tm, hw), lambda i: (i, 0)),
                  pl.BlockSpec((tm, lw), lambda i: (i, lru_y_off // lw)),
                  pl.BlockSpec((tm, lw), lambda i: (i, 0)),
                  pl.BlockSpec((tm, lw), lambda i: (i, 0)),
                  pl.BlockSpec((tm, 3 * d), lambda i: (i, 0)),
                  w_spec(aw), w_spec(hw), w_spec(lw)],
        out_specs=pl.BlockSpec((tm, d), lambda i: (i, 0)),
        out_shape=jax.ShapeDtypeStruct((t, d), BF16),
        compiler_params=_cparams(1), name="merge_branches",
    )(o_att, o_hy, p, hf, hb, gates, wa, wh, wl)


def _out_proj_kernel(m_ref, w_ref, x_ref, gate_ref, o_ref):
    o_ref[...] = x_ref[...] + gate_ref[0, 0] * jnp.dot(m_ref[...], w_ref[0],
                                                       preferred_element_type=F32)


def out_proj_residual(m, w, layer, x, mods, which_gate, n_lat, batch, tm, rows=None):
    t, d = x.shape
    return pl.pallas_call(
        _out_proj_kernel,
        grid=((t if rows is None else rows) // tm,),
        in_specs=[pl.BlockSpec((tm, d), lambda i: (i, 0)),
                  pl.BlockSpec((1, d, d), lambda i: (layer, 0, 0)),
                  pl.BlockSpec((tm, d), lambda i: (i, 0)),
                  _mod_spec(which_gate, tm, n_lat, batch, d)],
        out_specs=pl.BlockSpec((tm, d), lambda i: (i, 0)),
        out_shape=jax.ShapeDtypeStruct((t, d), F32),
        compiler_params=_cparams(1), name="out_proj_residual",
    )(m, w, x, mods)


def _router_kernel(x_ref, g_ref, sh_ref, sc_ref, rw_ref, rb_ref, h_ref, meta_ref, cnt_ref):
    h = _normmod(x_ref[...], g_ref[...], sh_ref[0, 0], sc_ref[0, 0])
    _store_tile_rows(h_ref, 0, _pack_bf16_pairs(h))
    w = rw_ref[...]
    h_hi, w_hi = h.astype(BF16), w.astype(BF16)
    h_lo = (h - h_hi.astype(F32)).astype(BF16)
    w_lo = (w - w_hi.astype(F32)).astype(BF16)
    logits = (jnp.dot(h_hi, w_hi, preferred_element_type=F32) + jnp.dot(h_lo, w_hi, preferred_element_type=F32)
              + jnp.dot(h_hi, w_lo, preferred_element_type=F32)) + rb_ref[...]
    lane = lax.broadcasted_iota(jnp.int32, logits.shape, 1)
    valid = lane < N_EXPERTS
    logits = jnp.where(valid, logits, -jnp.inf)
    ex = jnp.exp(logits - jnp.max(logits, axis=-1, keepdims=True))
    scores = ex / jnp.sum(ex, axis=-1, keepdims=True)
    grp = lax.shift_right_logical(lane, int(math.log2(EXPERTS_PER_GROUP)))
    best = jnp.zeros((logits.shape[0], 1), jnp.int32)
    best_max = jnp.max(jnp.where((grp == 0) & valid, scores, -1.0), axis=-1, keepdims=True)
    for g in range(1, N_GROUPS):
        gm = jnp.max(jnp.where((grp == g) & valid, scores, -1.0), axis=-1, keepdims=True)
        better = gm > best_max
        best = jnp.where(better, g, best)
        best_max = jnp.where(better, gm, best_max)
    s1 = jnp.where((grp == best) & valid, scores, -1.0)
    m1 = jnp.max(s1, axis=-1, keepdims=True)
    i1 = jnp.min(jnp.where(s1 == m1, lane, LANES_V7X), axis=-1, keepdims=True)
    s2 = jnp.where(lane == i1, -1.0, s1)
    m2 = jnp.max(s2, axis=-1, keepdims=True)
    i2 = jnp.min(jnp.where(s2 == m2, lane, LANES_V7X), axis=-1, keepdims=True)
    tot = m1 + m2
    @pl.when(pl.program_id(0) == 0)
    def _():
        cnt_ref[...] = jnp.zeros_like(cnt_ref)

    tm = logits.shape[0]
    onehot = ((lane == i1) | (lane == i2)).astype(BF16)
    lower = (lax.broadcasted_iota(jnp.int32, (tm, tm), 0)
             > lax.broadcasted_iota(jnp.int32, (tm, tm), 1)).astype(BF16)
    before = jnp.dot(lower, onehot, preferred_element_type=F32) + cnt_ref[...]
    r1 = jnp.sum(jnp.where(lane == i1, before, 0.0), axis=-1, keepdims=True)
    r2 = jnp.sum(jnp.where(lane == i2, before, 0.0), axis=-1, keepdims=True)
    cnt_ref[...] += jnp.sum(onehot.astype(F32), axis=0, keepdims=True)
    cols = (i1.astype(F32), i2.astype(F32), r1, r2, m1 / tot, m2 / tot)
    meta = jnp.zeros(logits.shape, F32)
    for k, col in enumerate(cols):
        meta = jnp.where(lane == k, col, meta)
    meta_ref[...] = meta


ROUTE_E, ROUTE_RANK, ROUTE_W = 0, 2, 4


def router(x, g, mods, which_shift, rw, rb, n_lat, batch, tm, rows=None):
    t, d = x.shape
    spr = d // 2 // LANES_V7X
    return pl.pallas_call(
        _router_kernel,
        grid=((t if rows is None else rows) // tm,),
        in_specs=[pl.BlockSpec((tm, d), lambda i: (i, 0)),
                  pl.BlockSpec((1, d), lambda i: (0, 0)),
                  _mod_spec(which_shift, tm, n_lat, batch, d),
                  _mod_spec(which_shift + 1, tm, n_lat, batch, d),
                  pl.BlockSpec((d, LANES_V7X), lambda i: (0, 0)),
                  pl.BlockSpec((1, LANES_V7X), lambda i: (0, 0))],
        out_specs=[pl.BlockSpec((tm * spr, LANES_V7X), lambda i: (i, 0)),
                   pl.BlockSpec((tm, LANES_V7X), lambda i: (i, 0)),
                   pl.BlockSpec((1, LANES_V7X), lambda i: (0, 0))],
        out_shape=[jax.ShapeDtypeStruct((t * spr, LANES_V7X), jnp.uint32),
                   jax.ShapeDtypeStruct((t, LANES_V7X), F32),
                   jax.ShapeDtypeStruct((1, LANES_V7X), F32)],
        compiler_params=_cparams(1), name="router",
    )(x, g.reshape(1, d), mods, mods, rw, rb)


def dispatch_plan(meta, counts, n_exp, tile, n_tiles):
    cnt = counts[0, :n_exp].astype(jnp.int32)
    tiles_e = (cnt + tile - 1) // tile
    tile_end = jnp.cumsum(tiles_e)
    row_start = (tile_end - tiles_e) * tile
    experts = meta[:, ROUTE_E:ROUTE_E + 2].astype(jnp.int32)
    ranks = meta[:, ROUTE_RANK:ROUTE_RANK + 2].astype(jnp.int32)
    dest = (row_start[experts] + ranks).reshape(-1)
    tile_expert = jnp.minimum(jnp.sum(jnp.arange(n_tiles)[:, None] >= tile_end[None, :], axis=1),
                              n_exp - 1).astype(jnp.int32)
    zero_start = jnp.minimum(row_start + cnt, (n_tiles - 1) * tile).astype(jnp.int32)
    return dest, tile_expert, tile_end[-1:].astype(jnp.int32), zero_start


def _dispatch_kernel(dest_ref, zero_ref, h_ref, xs_ref, zbuf, sem, zsem, *, spr):
    tm = h_ref.shape[0] // spr
    base = pl.program_id(0) * (2 * tm)
    token = lambda ref, t: ref.at[pl.ds(pl.multiple_of(t * spr, spr), spr)]

    @pl.when(pl.program_id(0) == 0)
    def _():
        zbuf[...] = jnp.zeros_like(zbuf)

        def clear(e, carry):
            start = pl.multiple_of(zero_ref[e] * spr, spr)
            copy = pltpu.make_async_copy(zbuf, xs_ref.at[pl.ds(start, zbuf.shape[0])], zsem)
            copy.start()
            copy.wait()
            return carry

        lax.fori_loop(0, zero_ref.shape[0], clear, 0)

    def issue(r, carry):
        for k in range(2):
            pltpu.make_async_copy(token(h_ref, r), token(xs_ref, dest_ref[base + 2 * r + k]), sem).start()
        return carry

    lax.fori_loop(0, tm, issue, 0, unroll=8)
    for k in range(2):
        pltpu.make_async_copy(h_ref, xs_ref.at[pl.ds(0, tm * spr)], sem).wait()


def moe_dispatch(h, dest, zero_start, n_rows, tile, tm, spr):
    assert spr == SUBLANES_V7X
    t = dest.shape[0] // 2
    return pl.pallas_call(
        functools.partial(_dispatch_kernel, spr=spr),
        grid_spec=pltpu.PrefetchScalarGridSpec(
            num_scalar_prefetch=2, grid=(t // tm,),
            in_specs=[pl.BlockSpec((tm * spr, LANES_V7X), lambda i, dest, zs: (i, 0))],
            out_specs=pl.BlockSpec(memory_space=pl.ANY),
            scratch_shapes=[pltpu.VMEM((tile * spr, LANES_V7X), h.dtype),
                            pltpu.SemaphoreType.DMA, pltpu.SemaphoreType.DMA]),
        out_shape=jax.ShapeDtypeStruct((n_rows * spr, LANES_V7X), h.dtype),
        compiler_params=_cparams(1), name="moe_dispatch",
    )(dest, zero_start, h)


def _experts_kernel(te_ref, nu_ref, xs_ref, wg_ref, wu_ref, wd_ref, ys_ref, wgb, wub, wdb, *, spr):
    j = pl.program_id(0)

    @pl.when(j < nu_ref[0])
    def _():
        @pl.when((j == 0) | (te_ref[j] != te_ref[jnp.maximum(j - 1, 0)]))
        def _():
            wgb[...] = wg_ref[0, 0].astype(BF16)
            wub[...] = wu_ref[0, 0].astype(BF16)
            wdb[...] = wd_ref[0, 0].astype(BF16)

        tile = xs_ref.shape[0] // spr
        x_hi, x_lo = (v.astype(BF16) for v in _unpack_bf16_pairs(_load_tile_rows(xs_ref, 0, tile, spr)))
        half = x_hi.shape[1]

        def proj(w):
            return (jnp.dot(x_hi, w[:half, :], preferred_element_type=F32)
                    + jnp.dot(x_lo, w[half:, :], preferred_element_type=F32))

        gt = proj(wgb)
        act = (gt * _sigmoid(gt)) * proj(wub)
        _store_tile_rows(ys_ref, 0, _pack_bf16_pairs(
            jnp.dot(act.astype(BF16), wdb[...], preferred_element_type=F32)))


def moe_experts(tile_expert, n_used, xs, wg, wu, wd, layer, tile):
    d, de = wg.shape[2], wg.shape[3]
    spr = d // 2 // LANES_V7X
    n_tiles = xs.shape[0] // (tile * spr)
    used = lambda j, nu: jnp.minimum(j, nu[0] - 1)
    w_spec = lambda shape: pl.BlockSpec((1, 1) + shape, lambda j, te, nu: (layer, te[used(j, nu)], 0, 0))
    return pl.pallas_call(
        functools.partial(_experts_kernel, spr=spr),
        grid_spec=pltpu.PrefetchScalarGridSpec(
            num_scalar_prefetch=2, grid=(n_tiles,),
            in_specs=[pl.BlockSpec((tile * spr, LANES_V7X), lambda j, te, nu: (used(j, nu), 0)),
                      w_spec((d, de)), w_spec((d, de)), w_spec((de, d))],
            out_specs=pl.BlockSpec((tile * spr, LANES_V7X), lambda j, te, nu: (used(j, nu), 0)),
            scratch_shapes=[pltpu.VMEM((d, de), BF16), pltpu.VMEM((d, de), BF16),
                            pltpu.VMEM((de, d), BF16)]),
        out_shape=jax.ShapeDtypeStruct(xs.shape, xs.dtype),
        compiler_params=_cparams(1), name="moe_experts",
    )(tile_expert, n_used, xs, wg, wu, wd)


def _combine_kernel(dest_ref, meta_ref, x_ref, gate_ref, ys_ref, o_ref, buf, sem, *, spr):
    i = pl.program_id(0)
    tm = x_ref.shape[0]
    part = lambda slot, k: pl.multiple_of((slot * 2 + k) * (tm * spr), spr)

    def gather(tile, slot):
        base = tile * (2 * tm)

        def issue(r, carry):
            for k in range(2):
                row = dest_ref[base + 2 * r + k]
                pltpu.make_async_copy(ys_ref.at[pl.ds(pl.multiple_of(row * spr, spr), spr)],
                                      buf.at[pl.ds(part(slot, k) + r * spr, spr)], sem.at[slot]).start()
            return carry

        lax.fori_loop(0, tm, issue, 0, unroll=8)

    @pl.when(i == 0)
    def _():
        gather(0, 0)

    @pl.when(i + 1 < pl.num_programs(0))
    def _():
        gather(i + 1, (i + 1) % 2)

    slot = i % 2
    for k in range(2):
        pltpu.make_async_copy(ys_ref.at[pl.ds(0, tm * spr)], buf.at[pl.ds(part(slot, k), tm * spr)],
                              sem.at[slot]).wait()
    meta = meta_ref[...]
    w1 = meta[:, ROUTE_W:ROUTE_W + 1]
    w2 = meta[:, ROUTE_W + 1:ROUTE_W + 2]
    y1_hi, y1_lo = _unpack_bf16_pairs(_load_tile_rows(buf, part(slot, 0), tm, spr))
    y2_hi, y2_lo = _unpack_bf16_pairs(_load_tile_rows(buf, part(slot, 1), tm, spr))
    half = y1_hi.shape[1]
    gate = gate_ref[0, 0]
    o_ref[:, :half] = x_ref[:, :half] + gate[:, :half] * (w1 * y1_hi + w2 * y2_hi)
    o_ref[:, half:] = x_ref[:, half:] + gate[:, half:] * (w1 * y1_lo + w2 * y2_lo)


def moe_combine(dest, meta, x, mods, which_gate, ys, n_lat, batch, tm):
    t, d = x.shape
    spr = d // 2 // LANES_V7X
    return pl.pallas_call(
        functools.partial(_combine_kernel, spr=spr),
        grid_spec=pltpu.PrefetchScalarGridSpec(
            num_scalar_prefetch=1, grid=(dest.shape[0] // 2 // tm,),
            in_specs=[pl.BlockSpec((tm, LANES_V7X), lambda i, dest: (i, 0)),
                      pl.BlockSpec((tm, d), lambda i, dest: (i, 0)),
                      _mod_spec(which_gate, tm, n_lat, batch, d),
                      pl.BlockSpec(memory_space=pl.ANY)],
            out_specs=pl.BlockSpec((tm, d), lambda i, dest: (i, 0)),
            scratch_shapes=[pltpu.VMEM((2 * 2 * tm * spr, LANES_V7X), ys.dtype),
                            pltpu.SemaphoreType.DMA((2,))]),
        out_shape=jax.ShapeDtypeStruct((t, d), F32),
        compiler_params=_cparams(1), name="moe_combine",
    )(dest, meta, x, mods, ys)


def _rope_table(n_tokens):
    rows = n_tokens // GRID_W
    row = jnp.repeat(jnp.arange(rows), GRID_W).astype(F32)
    col = jnp.tile(jnp.arange(GRID_W), rows).astype(F32)
    pairs = HEAD_DIM // 4
    inv = ROPE_THETA ** (-jnp.arange(pairs, dtype=F32) / pairs)
    ang = jnp.concatenate([row[:, None] * inv, col[:, None] * inv], axis=-1)
    cos, sin = jnp.cos(ang), jnp.sin(ang)
    return jnp.concatenate([cos, cos, -sin, sin], axis=-1)


def _trig_kernel(ca_ref, sa_ref, cr_ref, sr_ref, o_ref, *, sign, inverse):
    tm = cr_ref.shape[0]
    n_lanes = ca_ref.shape[1] * LANES_V7X
    cr = cr_ref[...]
    sr = sr_ref[...]
    row = pl.program_id(0) * tm + lax.broadcasted_iota(jnp.int32, (tm, 1), 0)
    alt_row = sign * (1 - 2 * (row & 1)).astype(F32)
    for a in range(n_lanes // LANES_V7X):
        blk = slice(a * LANES_V7X, (a + 1) * LANES_V7X)
        ca = ca_ref[:, a:a + 1]
        sa = sa_ref[:, a:a + 1]
        cos_blk = ca * cr - sa * sr
        nsin_blk = -(sa * cr + ca * sr)
        col = a * LANES_V7X + lax.broadcasted_iota(jnp.int32, (1, LANES_V7X), 1)
        if inverse:
            o_ref[:, blk] = cos_blk.astype(o_ref.dtype)
            o_ref[:, n_lanes + a * LANES_V7X:n_lanes + (a + 1) * LANES_V7X] = (
                jnp.where(col == 0, alt_row, nsin_blk).astype(o_ref.dtype))
        else:
            alt_col = sign * (1 - 2 * (col & 1)).astype(F32)
            o_ref[0, :, blk] = cos_blk.astype(o_ref.dtype)
            o_ref[1, :, blk] = jnp.where(row == 0, alt_col, nsin_blk).astype(o_ref.dtype)


def _dft_tables(length):
    half = length // 2
    n_a = half // LANES_V7X
    tm = _pick(half, 256)
    idx = jnp.arange(half, dtype=jnp.int32)[:, None]
    a_hi = jnp.arange(n_a, dtype=jnp.int32)[None, :] * LANES_V7X
    r_lo = jnp.arange(LANES_V7X, dtype=jnp.int32)[None, :]
    ang = lambda prod: (prod % (2 * length)).astype(F32) * (math.pi / length)

    def table(row_val, lane_hi, lane_lo, sign, inverse):
        out_shape = (half, 2 * half) if inverse else (2, half, half)
        out_block = (tm, 2 * half) if inverse else (2, tm, half)
        out_index = (lambda i: (i, 0)) if inverse else (lambda i: (0, i, 0))
        small = [f(ang(row_val * v)) for v in (lane_hi, lane_lo) for f in (jnp.cos, jnp.sin)]
        return pl.pallas_call(
            functools.partial(_trig_kernel, sign=sign, inverse=inverse),
            grid=(half // tm,),
            in_specs=[pl.BlockSpec((tm, n_a), lambda i: (i, 0)), pl.BlockSpec((tm, n_a), lambda i: (i, 0)),
                      pl.BlockSpec((tm, LANES_V7X), lambda i: (i, 0)),
                      pl.BlockSpec((tm, LANES_V7X), lambda i: (i, 0))],
            out_specs=pl.BlockSpec(out_block, out_index),
            out_shape=jax.ShapeDtypeStruct(out_shape, BF16),
            compiler_params=_cparams(1), name="dft_tables",
        )(*small)

    tables = []
    for par, sign in ((0, 1.0), (1, -1.0)):
        tables.append(table(idx, 2 * a_hi, 2 * r_lo + par, sign, False))
        tables.append(table(2 * idx + par, a_hi, r_lo, sign, True))
    fe, ie, fo, io = tables
    return fe, fo, ie, io


def _hyena_features(length, k_pad):
    t = jnp.linspace(0.0, 1.0, length, dtype=F32)[:, None]
    w = 2.0 * math.pi * jnp.arange(length, dtype=F32)[:, None] / length
    f = jnp.linspace(1e-4, HYENA_BANDS - 1, HYENA_BANDS, dtype=F32)[None, :]
    z = jnp.concatenate([t, jnp.cos(f * w), -jnp.sin(f * w)], axis=-1)
    return jnp.pad(z, ((0, 0), (0, k_pad - z.shape[1])))


def _blockdiag_dense(w):
    nb, bs = w.shape[-3], w.shape[-2]
    eye = jnp.eye(nb, dtype=w.dtype)
    dense = jnp.einsum('...nde,nm->...ndme', w, eye)
    return dense.reshape(w.shape[:-3] + (nb * bs, nb * bs))


def kernel(x, c, ctx, c_ctx, w_ada, b_ada, g_mix, g_ffn, w_in, attn_sink, hy_short_w, hy_short_b, hy_w1, hy_b1, hy_freq, hy_w2, hy_b2, hy_w3, hy_skip, lru_conv_w, lru_conv_b, lru_wa, lru_ba, lru_wx, lru_bx, lru_lambda, w_br_attn, w_br_hy, w_br_lru, w_out, router_w, router_b, exp_w_gate, exp_w_up, exp_w_down, final_g):
    batch, n_lat, d = x.shape
    n_ctx = ctx.shape[1]
    depth = w_ada.shape[0]
    t_lat, t_ctx = batch * n_lat, batch * n_ctx
    hw = hy_skip.shape[2]
    lw = lru_lambda.shape[2]
    fh = hy_w2.shape[1]
    aw = N_Q_HEADS * HEAD_DIM
    kvw = N_KV_HEADS * HEAD_DIM
    q_off = 2 * kvw + lw
    hy_off = q_off + aw
    lru_y_off = hy_off + 3 * hw
    gate_off = lru_y_off + lw
    assert batch + 1 <= MOD_ROWS and t_lat % n_ctx == 0 and n_lat % ATTN_BLOCK == 0
    tile_base = math.gcd(n_lat, t_ctx)
    tm = _pick(tile_base, 512)
    tm_big = _pick(tile_base, 1024)

    xu = jnp.concatenate([x.reshape(t_lat, d), ctx.reshape(t_ctx, d)], axis=0)
    cc = jnp.concatenate([c, c_ctx[None, :], jnp.zeros((MOD_ROWS - batch - 1, d), F32)], axis=0)
    mods_all = ada_tables(cc, w_ada, b_ada).reshape(depth, MOD_ROWS, N_MOD, 1, d)

    rope_tab = _rope_table(n_lat)
    band_bias = _band_bias(n_lat)
    deltas = jnp.abs(jnp.linspace(math.log(HYENA_DECAY_TARGET) / HYENA_FAST_DECAY,
                                  math.log(HYENA_DECAY_TARGET) / HYENA_SLOW_DECAY, hw, dtype=F32))[None, :]
    seqs = []
    for length, row0 in ((n_lat, 0), (n_ctx, t_lat)):
        seqs.append((length, row0, _dft_tables(length), _hyena_features(length, fh)))
    hy_w1p = jnp.pad(hy_w1, ((0, 0), (0, fh - hy_w1.shape[1]), (0, 0)))

    sp = softplus_neg(lru_lambda)
    lru_w = jnp.concatenate([_blockdiag_dense(lru_wa), _blockdiag_dense(lru_wx)], axis=-1).astype(BF16)
    lru_b = jnp.concatenate([lru_ba, lru_bx], axis=-1)[:, :, None, :]
    wa_b, wh_b, wl_b, wo_b = (w.astype(BF16) for w in (w_br_attn, w_br_hy, w_br_lru, w_out))
    rw = jnp.pad(router_w, ((0, 0), (0, LANES_V7X - router_w.shape[1])))
    rb = jnp.pad(router_b, (0, LANES_V7X - router_b.shape[0]))[None, :]
    h_zero = jnp.zeros((batch, 2, lw), F32)
    t_all = t_lat + t_ctx
    n_exp = exp_w_gate.shape[1]
    exp_tile = _pick(2 * t_all, 512)
    n_exp_tiles = (2 * t_all) // exp_tile + n_exp
    tm_cmb = _pick(tile_base, 256)

    for l in range(depth):
        last = l == depth - 1
        rows = t_lat if last else t_all
        mods = mods_all[l]
        h = normmod(xu, g_mix[l], mods, 0, n_lat, batch, tm)
        p = in_proj(h, w_in, l, tm_big, 0, gate_off, BF16)
        gates = in_proj(h, w_in, l, tm_big, gate_off, w_in.shape[2] - gate_off, BF16, rows=rows)

        o_att = latent_attention(p, attn_sink[l], rope_tab, band_bias, batch, n_lat, n_ctx, q_off)
        if not last:
            o_att = context_attention(p, attn_sink[l], o_att, batch, n_lat, n_ctx, q_off)

        xc_c = dwconv(p, lru_conv_w[l], lru_conv_b[l], batch, n_ctx, t_lat, 2 * kvw, lw)
        hf, hb, h_end = rg_lru(xc_c, lru_w, l, lru_b[l], sp[l], h_zero, batch, n_ctx, t_all, t_lat)
        xc_l = dwconv(p, lru_conv_w[l], lru_conv_b[l], batch, n_lat, 0, 2 * kvw, lw)
        hf, hb, _ = rg_lru(xc_l, lru_w, l, lru_b[l], sp[l], h_end, batch, n_lat, t_all, 0, prev=(hf, hb))

        o_hy = None
        for length, row0, (fe, fo, ie, io), feats in seqs[:1] if last else seqs:
            u = dwconv(p, hy_short_w[l], hy_short_b[l], batch, length, row0, hy_off, 3 * hw, parity_order=True)
            hid = hyena_hidden(feats, hy_w1p[l], hy_b1[l], hy_freq[l], hy_w2[l], hy_b2[l])
            taps, kh = hyena_taps(hid, hy_w3[l], deltas)
            kspec = hyena_spectrum(fe, fo, taps)
            pq = conv_forward(fe, fo, u, 0, kspec, kh, 0, batch, length, hw)
            z1 = conv_inverse(ie, io, pq, u, 0, u, hw, hy_skip[l], 0, batch, length, BF16)
            pq = conv_forward(fe, fo, z1, 0, kspec, kh, 1, batch, length, hw)
            o_hy = conv_inverse(ie, io, pq, z1, 0, u, 2 * hw, hy_skip[l], 1, batch, length, BF16,
                                token_order=True, total_rows=t_all, row0=row0, prev=o_hy)

        m = merge_branches(o_att, o_hy, p, gates, hf, hb, wa_b, wh_b, wl_b, l, lru_y_off, tm, rows=rows)
        xu = out_proj_residual(m, wo_b, l, xu, mods, 2, n_lat, batch, tm, rows=rows)

        fl, route, counts = router(xu, g_ffn[l], mods, 3, rw, rb, n_lat, batch, tm, rows=rows)
        dest, tile_expert, n_used, zero_start = dispatch_plan(route[:rows], counts, n_exp, exp_tile, n_exp_tiles)
        xs = moe_dispatch(fl, dest, zero_start, n_exp_tiles * exp_tile, exp_tile, tm, d // 2 // LANES_V7X)
        ys = moe_experts(tile_expert, n_used, xs, exp_w_gate, exp_w_up, exp_w_down, l, exp_tile)
        xu = moe_combine(dest, route, xu, mods, 5, ys, n_lat, batch, tm_cmb)

    return final_norm(xu, final_g, t_lat, tm).reshape(batch, n_lat, d)
```

```python
import functools
import math

import jax
import jax.numpy as jnp
import numpy as np
from jax import lax
from jax.experimental import pallas as pl
from jax.experimental.pallas import tpu as pltpu

F32 = jnp.float32
BF16 = jnp.bfloat16
HIGHEST = lax.Precision.HIGHEST

LANES_V7X = 128
VMEM_LIMIT_V7X = 56 * 1024 * 1024

EPS = 1e-6
GRID_W = 64
HEAD_DIM = 128
N_Q_HEADS = 8
N_KV_HEADS = 2
Q_PER_KV = N_Q_HEADS // N_KV_HEADS
WINDOW = 128
ATTN_BLOCK = 128
ROPE_THETA = 10000.0
HYENA_BANDS = 16
HYENA_DECAY_TARGET = 1e-2
HYENA_FAST_DECAY = 0.3
HYENA_SLOW_DECAY = 1.5
LRU_C = 8.0
N_EXPERTS = 16
N_GROUPS = 4
EXPERTS_PER_GROUP = N_EXPERTS // N_GROUPS
LOG2_E = math.log2(math.e)
N_MOD = 6
MOD_ROWS = 8


def _cparams(n_axes):
    return pltpu.CompilerParams(dimension_semantics=("arbitrary",) * n_axes,
                                vmem_limit_bytes=VMEM_LIMIT_V7X)


def _sigmoid(x):
    return 0.5 * (1.0 + jnp.tanh(0.5 * x))


def _pack_bf16_pairs(x):
    half = x.shape[1] // 2
    hi = lax.bitcast_convert_type(x[:, :half].astype(BF16).astype(F32), jnp.uint32)
    lo = lax.bitcast_convert_type(x[:, half:].astype(BF16).astype(F32), jnp.uint32)
    return hi | (lo >> 16)


def _unpack_bf16_pairs(w):
    hi = lax.bitcast_convert_type(w & jnp.uint32(0xFFFF0000), F32)
    lo = lax.bitcast_convert_type(w << 16, F32)
    return hi, lo


SUBLANES_V7X = 8


def _store_tile_rows(ref, row0, x):
    rows, width = x.shape
    s_per_row = width // LANES_V7X
    for s in range(s_per_row):
        ref[pl.ds(row0 + s, rows, stride=s_per_row), :] = x[:, s * LANES_V7X:(s + 1) * LANES_V7X]


def _load_tile_rows(ref, row0, rows, s_per_row):
    return jnp.concatenate([ref[pl.ds(row0 + s, rows, stride=s_per_row), :] for s in range(s_per_row)], axis=1)


def _pick(n, cap):
    t = cap
    while n % t:
        t //= 2
    return t


def _ada_kernel(c_ref, w_ref, b_ref, o_ref):
    c = c_ref[...]
    s = (c * jax.nn.sigmoid(c)).astype(BF16)
    o_ref[0] = jnp.dot(s, w_ref[0].astype(BF16), preferred_element_type=F32) + b_ref[0]


def ada_tables(cc, w_ada, b_ada):
    depth, d, n6 = w_ada.shape
    tn = _pick(n6, 1024)
    return pl.pallas_call(
        _ada_kernel,
        grid=(depth, n6 // tn),
        in_specs=[pl.BlockSpec((MOD_ROWS, d), lambda l, j: (0, 0)),
                  pl.BlockSpec((1, d, tn), lambda l, j: (l, 0, j)),
                  pl.BlockSpec((1, 1, tn), lambda l, j: (l, 0, j))],
        out_specs=pl.BlockSpec((1, MOD_ROWS, tn), lambda l, j: (l, 0, j)),
        out_shape=jax.ShapeDtypeStruct((depth, MOD_ROWS, n6), F32),
        compiler_params=_cparams(2), name="ada_tables",
    )(cc, w_ada, b_ada.reshape(depth, 1, n6))


def _mod_spec(which, tm, n_lat, batch, d):
    return pl.BlockSpec((1, 1, 1, d),
                        lambda i, *_: (jnp.minimum((i * tm) // n_lat, batch), which, 0, 0))


def _normmod(x, g, shift, scale):
    y = x * lax.rsqrt(jnp.mean(x * x, axis=-1, keepdims=True) + EPS) * g
    return y * (1.0 + scale) + shift


def _normmod_kernel(x_ref, g_ref, sh_ref, sc_ref, o_ref):
    o_ref[...] = _normmod(x_ref[...], g_ref[...], sh_ref[0, 0], sc_ref[0, 0]).astype(o_ref.dtype)


def normmod(x, g, mods, which_shift, n_lat, batch, tm):
    t, d = x.shape
    return pl.pallas_call(
        _normmod_kernel,
        grid=(t // tm,),
        in_specs=[pl.BlockSpec((tm, d), lambda i: (i, 0)),
                  pl.BlockSpec((1, d), lambda i: (0, 0)),
                  _mod_spec(which_shift, tm, n_lat, batch, d),
                  _mod_spec(which_shift + 1, tm, n_lat, batch, d)],
        out_specs=pl.BlockSpec((tm, d), lambda i: (i, 0)),
        out_shape=jax.ShapeDtypeStruct((t, d), BF16),
        compiler_params=_cparams(1), name="normmod",
    )(x, g.reshape(1, d), mods, mods)


def _final_norm_kernel(x_ref, g_ref, o_ref):
    x = x_ref[...]
    o_ref[...] = x * lax.rsqrt(jnp.mean(x * x, axis=-1, keepdims=True) + EPS) * g_ref[...]


def final_norm(x, g, rows, tm):
    d = x.shape[1]
    return pl.pallas_call(
        _final_norm_kernel,
        grid=(rows // tm,),
        in_specs=[pl.BlockSpec((tm, d), lambda i: (i, 0)), pl.BlockSpec((1, d), lambda i: (0, 0))],
        out_specs=pl.BlockSpec((tm, d), lambda i: (i, 0)),
        out_shape=jax.ShapeDtypeStruct((rows, d), F32),
        compiler_params=_cparams(1), name="final_norm",
    )(x, g.reshape(1, d))


def _proj_kernel(a_ref, w_ref, o_ref, wb_ref):
    @pl.when(pl.program_id(1) == 0)
    def _():
        wb_ref[...] = w_ref[0].astype(BF16)

    o_ref[...] = jnp.dot(a_ref[...], wb_ref[...], preferred_element_type=F32).astype(o_ref.dtype)


def in_proj(a, w, layer, tm, col0, n, out_dtype, rows=None):
    t, k = a.shape
    tn = _pick(math.gcd(n, col0) if col0 else n, 1024)
    return pl.pallas_call(
        _proj_kernel,
        grid=(n // tn, (t if rows is None else rows) // tm),
        in_specs=[pl.BlockSpec((tm, k), lambda j, i: (i, 0)),
                  pl.BlockSpec((1, k, tn), lambda j, i: (layer, 0, col0 // tn + j))],
        out_specs=pl.BlockSpec((tm, tn), lambda j, i: (i, j)),
        out_shape=jax.ShapeDtypeStruct((t, n), out_dtype),
        scratch_shapes=[pltpu.VMEM((k, tn), BF16)],
        compiler_params=_cparams(2), name="in_proj",
    )(a, w)


def _rope(x, tab):
    return x * tab[:, :HEAD_DIM] + pltpu.roll(x, HEAD_DIM // 2, axis=1) * tab[:, HEAD_DIM:]


def _sink_column(sink_ref, h, rows):
    r = lax.broadcasted_iota(jnp.int32, (Q_PER_KV * rows, 1), 0)
    col = jnp.full((Q_PER_KV * rows, 1), sink_ref[h * Q_PER_KV], F32)
    for g in range(1, Q_PER_KV):
        col = jnp.where(r >= g * rows, sink_ref[h * Q_PER_KV + g], col)
    return col


def _nt(a, b):
    return lax.dot_general(a, b, (((1,), (1,)), ((), ())), preferred_element_type=F32)


def _band_bias(n_lat):
    blk = ATTN_BLOCK
    single = n_lat == blk
    qi = (jnp.arange(Q_PER_KV * blk) % blk)[:, None]
    kj = jnp.arange(3 * blk)[None, :]
    band = jnp.abs(qi + blk - kj) <= WINDOW
    kinds = []
    for no_prev, no_next in ((True, single), (False, False), (single, True)):
        ok = band & ((kj >= blk) | (not no_prev)) & ((kj < 2 * blk) | (not no_next))
        kinds.append(jnp.where(ok, 0.0, -jnp.inf))
    return jnp.stack(kinds).astype(F32)


def _lat_attn_kernel(sink_ref, q_ref, kvm_ref, kv0_ref, kvp_ref, kvc_ref, tm_ref, t0_ref, tp_ref, bias_ref,
                     o_ref):
    blk = ATTN_BLOCK
    kv_w = N_KV_HEADS * HEAD_DIM
    scale = HEAD_DIM ** -0.5
    tabs = (tm_ref[...], t0_ref[...], tp_ref[...])
    kvs = (kvm_ref, kv0_ref, kvp_ref)
    for h in range(N_KV_HEADS):
        ks = slice(h * HEAD_DIM, (h + 1) * HEAD_DIM)
        vs = slice(kv_w + h * HEAD_DIM, kv_w + (h + 1) * HEAD_DIM)
        k_loc = jnp.concatenate([_rope(kvs[n][:, ks].astype(F32), tabs[n]) for n in range(3)],
                                axis=0).astype(BF16)
        v_loc = jnp.concatenate([kvs[n][:, vs] for n in range(3)], axis=0).astype(BF16)
        k_ctx = kvc_ref[:, ks].astype(BF16)
        v_ctx = kvc_ref[:, vs].astype(BF16)
        q4 = jnp.concatenate(
            [_rope(q_ref[:, (h * Q_PER_KV + g) * HEAD_DIM:(h * Q_PER_KV + g + 1) * HEAD_DIM].astype(F32), tabs[1])
             * (scale * LOG2_E) for g in range(Q_PER_KV)], axis=0).astype(BF16)
        s_loc = _nt(q4, k_loc) + bias_ref[0]
        s_ctx = _nt(q4, k_ctx)
        sink = _sink_column(sink_ref, h, blk) * LOG2_E
        m = jnp.maximum(jnp.maximum(jnp.max(s_loc, axis=-1, keepdims=True),
                                    jnp.max(s_ctx, axis=-1, keepdims=True)), sink)
        p_loc = jnp.exp2(s_loc - m)
        p_ctx = jnp.exp2(s_ctx - m)
        den = (jnp.sum(p_loc, axis=-1, keepdims=True) + jnp.sum(p_ctx, axis=-1, keepdims=True)
               + jnp.exp2(sink - m))
        o = (jnp.dot(p_loc.astype(BF16), v_loc, preferred_element_type=F32)
             + jnp.dot(p_ctx.astype(BF16), v_ctx, preferred_element_type=F32)) / den
        for g in range(Q_PER_KV):
            hq = h * Q_PER_KV + g
            o_ref[:, hq * HEAD_DIM:(hq + 1) * HEAD_DIM] = o[g * blk:(g + 1) * blk].astype(o_ref.dtype)


def latent_attention(p, sink, rope_tab, band_bias, batch, n_lat, n_ctx, q_off):
    blk = ATTN_BLOCK
    nb = n_lat // blk
    aw = N_Q_HEADS * HEAD_DIM
    kvw = 2 * N_KV_HEADS * HEAD_DIM
    ctx_blk0 = (batch * n_lat) // n_ctx
    kv_spec = lambda off: pl.BlockSpec(
        (blk, kvw), lambda b, i: (b * nb + jnp.clip(i + off, 0, nb - 1), 0))
    tab_spec = lambda off: pl.BlockSpec(
        (blk, 2 * HEAD_DIM), lambda b, i: (jnp.clip(i + off, 0, nb - 1), 0))
    bias_kind = lambda b, i: (jnp.where(i == 0, 0, jnp.where(i == nb - 1, 2, 1)), 0, 0)
    return pl.pallas_call(
        _lat_attn_kernel,
        grid=(batch, nb),
        in_specs=[pl.BlockSpec(memory_space=pltpu.SMEM),
                  pl.BlockSpec((blk, aw), lambda b, i: (b * nb + i, q_off // aw)),
                  kv_spec(-1), kv_spec(0), kv_spec(1),
                  pl.BlockSpec((n_ctx, kvw), lambda b, i: (ctx_blk0 + b, 0)),
                  tab_spec(-1), tab_spec(0), tab_spec(1),
                  pl.BlockSpec((1, Q_PER_KV * blk, 3 * blk), bias_kind)],
        out_specs=pl.BlockSpec((blk, aw), lambda b, i: (b * nb + i, 0)),
        out_shape=jax.ShapeDtypeStruct((p.shape[0], aw), BF16),
        compiler_params=_cparams(2), name="latent_attention",
    )(sink, p, p, p, p, p, rope_tab, rope_tab, rope_tab, band_bias)


def _ctx_attn_kernel(sink_ref, q_ref, kv_ref, o_lat_ref, o_ref):
    del o_lat_ref
    rows = q_ref.shape[0]
    kv_w = N_KV_HEADS * HEAD_DIM
    scale = HEAD_DIM ** -0.5
    for h in range(N_KV_HEADS):
        k = kv_ref[:, h * HEAD_DIM:(h + 1) * HEAD_DIM].astype(BF16)
        v = kv_ref[:, kv_w + h * HEAD_DIM:kv_w + (h + 1) * HEAD_DIM].astype(BF16)
        q4 = jnp.concatenate(
            [q_ref[:, (h * Q_PER_KV + g) * HEAD_DIM:(h * Q_PER_KV + g + 1) * HEAD_DIM]
             for g in range(Q_PER_KV)], axis=0).astype(BF16)
        s = _nt(q4, k) * scale
        sink = _sink_column(sink_ref, h, rows)
        m = jnp.maximum(jnp.max(s, axis=-1, keepdims=True), sink)
        p = jnp.exp(s - m)
        den = jnp.sum(p, axis=-1, keepdims=True) + jnp.exp(sink - m)
        o = jnp.dot(p.astype(BF16), v, preferred_element_type=F32) / den
        for g in range(Q_PER_KV):
            hq = h * Q_PER_KV + g
            o_ref[:, hq * HEAD_DIM:(hq + 1) * HEAD_DIM] = o[g * rows:(g + 1) * rows].astype(o_ref.dtype)


def context_attention(p, sink, o_lat, batch, n_lat, n_ctx, q_off):
    aw = N_Q_HEADS * HEAD_DIM
    kvw = 2 * N_KV_HEADS * HEAD_DIM
    ctx_blk0 = (batch * n_lat) // n_ctx
    return pl.pallas_call(
        _ctx_attn_kernel,
        grid=(batch,),
        in_specs=[pl.BlockSpec(memory_space=pltpu.SMEM),
                  pl.BlockSpec((n_ctx, aw), lambda b: (ctx_blk0 + b, q_off // aw)),
                  pl.BlockSpec((n_ctx, kvw), lambda b: (ctx_blk0 + b, 0)),
                  pl.BlockSpec(memory_space=pl.ANY)],
        out_specs=pl.BlockSpec((n_ctx, aw), lambda b: (ctx_blk0 + b, 0)),
        out_shape=jax.ShapeDtypeStruct(o_lat.shape, o_lat.dtype),
        input_output_aliases={3: 0},
        compiler_params=_cparams(1), name="context_attention",
    )(sink, p, p, o_lat)


def _dwconv_kernel(u_ref, w_ref, b_ref, o_ref, *scratch, width):
    x = u_ref[...].astype(F32)
    length = x.shape[0]
    left = width // 2
    t = lax.broadcasted_iota(jnp.int32, (length, 1), 0)
    acc = jnp.broadcast_to(b_ref[...], x.shape)
    for j in range(width):
        s = j - left
        if s == 0:
            xs = x
        else:
            xs = pltpu.roll(x, (-s) % length, axis=0)
            xs = jnp.where((t + s >= 0) & (t + s < length), xs, 0.0)
        acc = acc + xs * w_ref[j:j + 1, :]
    if not scratch:
        o_ref[...] = acc
    else:
        half = length // 2
        scratch[0][...] = acc
        o_ref[:half, :] = scratch[0][pl.ds(0, half, stride=2), :].astype(o_ref.dtype)
        o_ref[half:, :] = scratch[0][pl.ds(1, half, stride=2), :].astype(o_ref.dtype)


def dwconv(p, w, b, n_seq, length, row0, col0, n_ch, parity_order=False):
    width = w.shape[0]
    cb = LANES_V7X if parity_order else _pick(n_ch, 256)
    return pl.pallas_call(
        functools.partial(_dwconv_kernel, width=width),
        grid=(n_seq, n_ch // cb),
        in_specs=[pl.BlockSpec((length, cb), lambda s, c: (row0 // length + s, col0 // cb + c)),
                  pl.BlockSpec((width, cb), lambda s, c: (0, c)),
                  pl.BlockSpec((1, cb), lambda s, c: (0, c))],
        out_specs=pl.BlockSpec((length, cb), lambda s, c: (s, c)),
        out_shape=jax.ShapeDtypeStruct((n_seq * length, n_ch), BF16 if parity_order else F32),
        scratch_shapes=[pltpu.VMEM((length, cb), F32)] if parity_order else [],
        compiler_params=_cparams(2), name="dwconv",
    )(p, w, b.reshape(1, n_ch))


def _lru_kernel(xf_ref, xb_ref, w_ref, bias_ref, sp_ref, h0_ref, *rest):
    hf_ref, hb_ref, hl_ref, af, bf, ab, bb, carry = rest[-8:]
    j = pl.program_id(1)
    width = xf_ref.shape[1]

    @pl.when(j == 0)
    def _():
        carry[...] = h0_ref[0]

    def gates(x, d):
        g = jnp.dot(x.astype(BF16), w_ref[0, d], preferred_element_type=F32) + bias_ref[d]
        a = jnp.exp2((-0.5 * LRU_C * LOG2_E * sp_ref[d]) * (1.0 + jnp.tanh(0.5 * g[:, :width])))
        gx = (1.0 + jnp.tanh(0.5 * g[:, width:])) * (0.5 * x)
        return a, jnp.sqrt(1.0 - a * a) * gx

    af[...], bf[...] = gates(xf_ref[...], 0)
    ab[...], bb[...] = gates(xb_ref[...], 1)
    tl = af.shape[0]

    def body(t, hs):
        hf, hb = hs
        hf = af[pl.ds(t, 1), :] * hf + bf[pl.ds(t, 1), :]
        hf_ref[pl.ds(t, 1), :] = hf
        tb = tl - 1 - t
        hb = ab[pl.ds(tb, 1), :] * hb + bb[pl.ds(tb, 1), :]
        hb_ref[pl.ds(tb, 1), :] = hb
        return hf, hb

    hf, hb = lax.fori_loop(0, tl, body, (carry[0:1, :], carry[1:2, :]), unroll=8)
    carry[0:1, :] = hf
    carry[1:2, :] = hb
    hl_ref[0] = carry[...]


def rg_lru(xc, w_gates, layer, b_gates, softplus_neg_lam, h0, n_seq, length, total_rows, row0, prev=None):
    width = xc.shape[1]
    tl = _pick(length, 512)
    nc = length // tl
    blk0 = row0 // tl
    in_specs = [pl.BlockSpec((tl, width), lambda s, j: (s * nc + j, 0)),
                pl.BlockSpec((tl, width), lambda s, j: (s * nc + nc - 1 - j, 0)),
                pl.BlockSpec((1, 2, width, 2 * width), lambda s, j: (layer, 0, 0, 0)),
                pl.BlockSpec((2, 1, 2 * width), lambda s, j: (0, 0, 0)),
                pl.BlockSpec((2, 1, width), lambda s, j: (0, 0, 0)),
                pl.BlockSpec((1, 2, width), lambda s, j: (s, 0, 0))]
    args = [xc, xc, w_gates, b_gates, softplus_neg_lam, h0]
    aliases = {}
    if prev is not None:
        in_specs += [pl.BlockSpec(memory_space=pl.ANY)] * 2
        aliases = {len(args): 0, len(args) + 1: 1}
        args += list(prev)
    return pl.pallas_call(
        _lru_kernel,
        grid=(n_seq, nc),
        in_specs=in_specs,
        out_specs=[pl.BlockSpec((tl, width), lambda s, j: (blk0 + s * nc + j, 0)),
                   pl.BlockSpec((tl, width), lambda s, j: (blk0 + s * nc + nc - 1 - j, 0)),
                   pl.BlockSpec((1, 2, width), lambda s, j: (s, 0, 0))],
        out_shape=[jax.ShapeDtypeStruct((total_rows, width), F32),
                   jax.ShapeDtypeStruct((total_rows, width), F32),
                   jax.ShapeDtypeStruct((n_seq, 2, width), F32)],
        scratch_shapes=[pltpu.VMEM((tl, width), F32)] * 4 + [pltpu.VMEM((2, width), F32)],
        input_output_aliases=aliases,
        compiler_params=_cparams(2), name="rg_lru",
    )(*args)


def _softplus_kernel(x_ref, o_ref):
    x = -x_ref[...]
    o_ref[...] = jnp.maximum(x, 0.0) + jnp.log(1.0 + jnp.exp(-jnp.abs(x)))


def softplus_neg(lam):
    depth, two, width = lam.shape
    x = lam.reshape(depth * two, width)
    out = pl.pallas_call(
        _softplus_kernel,
        out_shape=jax.ShapeDtypeStruct(x.shape, F32), name="softplus_neg",
    )(x)
    return out.reshape(depth, two, 1, width)


def _hy_hidden_kernel(z_ref, w1_ref, b1_ref, fr_ref, w2_ref, b2_ref, o_ref):
    fr = fr_ref[...]
    h = jnp.sin(fr * (jnp.dot(z_ref[...], w1_ref[...], precision=HIGHEST,
                              preferred_element_type=F32) + b1_ref[...]))
    o_ref[...] = jnp.sin(fr * (jnp.dot(h, w2_ref[...], precision=HIGHEST,
                                       preferred_element_type=F32) + b2_ref[...]))


def hyena_hidden(z, w1, b1, freq, w2, b2):
    length = z.shape[0]
    fh = w2.shape[0]
    return pl.pallas_call(
        _hy_hidden_kernel,
        out_shape=jax.ShapeDtypeStruct((length, fh), F32), name="hyena_hidden",
    )(z, w1, b1.reshape(1, fh), freq.reshape(1, fh), w2, b2.reshape(1, fh))


def _hy_taps_kernel(hid_ref, wf_ref, wb_ref, delta_ref, u_ref, kh_ref, tmp):
    hid = hid_ref[...]
    length = hid.shape[0]
    hf = jnp.dot(hid, wf_ref[...], precision=HIGHEST, preferred_element_type=F32)
    hb = jnp.dot(hid, wb_ref[...], precision=HIGHEST, preferred_element_type=F32)
    ti = lax.broadcasted_iota(jnp.int32, (length, 1), 0)
    win = jnp.exp(-(ti.astype(F32) * (1.0 / (length - 1))) * delta_ref[...])
    f = hf * win
    b = jnp.where(ti >= 1, hb * win, 0.0)
    sc = lax.rsqrt(jnp.sum(f * f + b * b, axis=0, keepdims=True) + EPS)
    u1 = (f + b) * sc
    u2 = (f - b) * sc
    quarter = jnp.where((ti & 1) == 0, (1 - (ti & 2)).astype(F32), 0.0)
    kh_ref[...] = jnp.sum(u1 * quarter, axis=0, keepdims=True)
    half = length // 2
    for k, u in enumerate((u1, u2)):
        tmp[...] = u
        u_ref[k, :half, :] = tmp[pl.ds(0, half, stride=2), :].astype(u_ref.dtype)
        u_ref[k, half:, :] = tmp[pl.ds(1, half, stride=2), :].astype(u_ref.dtype)


def hyena_taps(hid, w3, deltas):
    length, fh = hid.shape
    hw = deltas.shape[1]
    n_ord = w3.shape[1] // (2 * hw)
    cw = _pick(hw, 128)
    nc = hw // cw
    return pl.pallas_call(
        _hy_taps_kernel,
        grid=(n_ord, nc),
        in_specs=[pl.BlockSpec((length, fh), lambda o, c: (0, 0)),
                  pl.BlockSpec((fh, cw), lambda o, c: (0, o * 2 * nc + c)),
                  pl.BlockSpec((fh, cw), lambda o, c: (0, o * 2 * nc + nc + c)),
                  pl.BlockSpec((1, cw), lambda o, c: (0, c))],
        out_specs=[pl.BlockSpec((2, length, cw), lambda o, c: (0, 0, o * nc + c)),
                   pl.BlockSpec((1, cw), lambda o, c: (0, o * nc + c))],
        out_shape=[jax.ShapeDtypeStruct((2, length, n_ord * hw), BF16),
                   jax.ShapeDtypeStruct((1, n_ord * hw), F32)],
        scratch_shapes=[pltpu.VMEM((length, cw), F32)],
        compiler_params=_cparams(2), name="hyena_taps",
    )(hid, w3, w3, deltas)


def _dft_halves(fe_ref, fo_ref, xe_r, xo_r, xe_i, xo_i, first):
    e_r = jnp.dot(fe_ref[0], xe_r, preferred_element_type=F32)
    e_i = jnp.dot(fe_ref[1], xe_i, preferred_element_type=F32)
    o_r = jnp.dot(fo_ref[0], xo_r, preferred_element_type=F32)
    o_i = jnp.dot(fo_ref[1], xo_i, preferred_element_type=F32)
    return e_r + o_r, jnp.where(first, e_i, e_i + o_i), e_r - o_r, jnp.where(first, o_i, o_i - e_i)


def _first_row(rows, tile_index):
    return (lax.broadcasted_iota(jnp.int32, (rows, 1), 0) == 0) & (tile_index == 0)


def _spectrum_kernel(fe_ref, fo_ref, u_ref, o_ref):
    half = u_ref.shape[1] // 2
    first = _first_row(o_ref.shape[1], pl.program_id(1))
    planes = _dft_halves(fe_ref, fo_ref, u_ref[0, :half, :], u_ref[0, half:, :],
                         u_ref[1, :half, :], u_ref[1, half:, :], first)
    for k, plane in enumerate(planes):
        o_ref[k] = plane


def hyena_spectrum(fe, fo, u):
    _, length, n = u.shape
    half = length // 2
    tm = _pick(half, 256)
    tn = _pick(n, 512)
    return pl.pallas_call(
        _spectrum_kernel,
        grid=(n // tn, half // tm),
        in_specs=[pl.BlockSpec((2, tm, half), lambda c, i: (0, i, 0)),
                  pl.BlockSpec((2, tm, half), lambda c, i: (0, i, 0)),
                  pl.BlockSpec((2, length, tn), lambda c, i: (0, 0, c))],
        out_specs=pl.BlockSpec((4, tm, tn), lambda c, i: (0, i, c)),
        out_shape=jax.ShapeDtypeStruct((4, half, n), F32),
        compiler_params=_cparams(2), name="hyena_spectrum",
    )(fe, fo, u)


def _conv_fwd_kernel(fe_ref, fo_ref, z_ref, k_ref, kh_ref, y_ref):
    half = z_ref.shape[0] // 2
    first = _first_row(y_ref.shape[2], pl.program_id(1))
    ze = z_ref[:half, :]
    zo = z_ref[half:, :]
    a_r, a_i, b_r, b_i = _dft_halves(fe_ref, fo_ref, ze, zo, ze, zo, first)
    ka_r, ka_i, kb_r, kb_i = k_ref[0], k_ref[1], k_ref[2], k_ref[3]
    kh_r = kh_ref[...]
    ya_r = jnp.where(first, 0.5 * a_r * ka_r, a_r * ka_r - a_i * ka_i)
    yb_r = jnp.where(first, 0.5 * b_r * kb_r, b_r * kb_r - b_i * kb_i)
    ya_i = jnp.where(first, a_i * kh_r - b_i * kb_i, a_r * ka_i + a_i * ka_r)
    yb_i = jnp.where(first, a_i * kb_i + b_i * kh_r, b_r * kb_i + b_i * kb_r)
    y_ref[0, 0] = (ya_r + yb_r).astype(y_ref.dtype)
    y_ref[0, 1] = jnp.where(first, ya_i, ya_i - yb_i).astype(y_ref.dtype)
    y_ref[0, 2] = (ya_r - yb_r).astype(y_ref.dtype)
    y_ref[0, 3] = jnp.where(first, yb_i, ya_i + yb_i).astype(y_ref.dtype)


def conv_forward(fe, fo, z, col0, kspec, kh, order, n_seq, length, hw):
    half = length // 2
    tm = _pick(half, 512)
    return pl.pallas_call(
        _conv_fwd_kernel,
        grid=(n_seq, half // tm),
        in_specs=[pl.BlockSpec((2, tm, half), lambda s, i: (0, i, 0)),
                  pl.BlockSpec((2, tm, half), lambda s, i: (0, i, 0)),
                  pl.BlockSpec((length, hw), lambda s, i: (s, col0 // hw)),
                  pl.BlockSpec((4, tm, hw), lambda s, i: (0, i, order)),
                  pl.BlockSpec((1, hw), lambda s, i: (0, order))],
        out_specs=pl.BlockSpec((1, 4, tm, hw), lambda s, i: (s, 0, i, 0)),
        out_shape=jax.ShapeDtypeStruct((n_seq, 4, half, hw), BF16),
        compiler_params=_cparams(2), name="hyena_conv_fwd",
    )(fe, fo, z, kspec, kh)


def _conv_inv_kernel(ie_ref, io_ref, y_ref, z_ref, gate_ref, skip_ref, *rest, inv_len, token_order):
    skip = skip_ref[0]
    halves = []
    for par, inv_ref in enumerate((ie_ref, io_ref)):
        y = jnp.dot(inv_ref[...], y_ref[0, par], preferred_element_type=F32) * inv_len
        halves.append(gate_ref[0, par].astype(F32) * (y + skip * z_ref[0, par].astype(F32)))
    if not token_order:
        o_ref = rest[-1]
        for par in range(2):
            o_ref[0, par] = halves[par].astype(o_ref.dtype)
    else:
        o_ref, mix = rest[-2:]
        tm = halves[0].shape[0]
        for c0 in range(0, o_ref.shape[1], LANES_V7X):
            cols = slice(c0, c0 + LANES_V7X)
            for par in range(2):
                mix[pl.ds(par, tm, stride=2), :] = halves[par][:, cols]
            o_ref[:, cols] = mix[...].astype(o_ref.dtype)


def conv_inverse(ie, io, pq, z, z_col0, gates, gate_col0, skip, order, n_seq, length, out_dtype,
                 token_order=False, total_rows=None, row0=0, prev=None):
    hw = pq.shape[3]
    half = length // 2
    tm = _pick(half, 512)
    nt = half // tm
    par = lambda a: a.reshape(n_seq, 2, half, a.shape[1])
    in_specs = [pl.BlockSpec((tm, length), lambda s, i: (i, 0)),
                pl.BlockSpec((tm, length), lambda s, i: (i, 0)),
                pl.BlockSpec((1, 2, length, hw), lambda s, i: (s, 0, 0, 0)),
                pl.BlockSpec((1, 2, tm, hw), lambda s, i: (s, 0, i, z_col0 // hw)),
                pl.BlockSpec((1, 2, tm, hw), lambda s, i: (s, 0, i, gate_col0 // hw)),
                pl.BlockSpec((1, 1, hw), lambda s, i: (order, 0, 0))]
    args = [ie, io, pq.reshape(n_seq, 2, length, hw), par(z), par(gates), skip.reshape(skip.shape[0], 1, hw)]
    aliases, scratch = {}, []
    if token_order:
        blk0 = row0 // (2 * tm)
        out_spec = pl.BlockSpec((2 * tm, hw), lambda s, i: (blk0 + s * nt + i, 0))
        out_shape = jax.ShapeDtypeStruct((total_rows, hw), out_dtype)
        scratch = [pltpu.VMEM((2 * tm, LANES_V7X), F32)]
        if prev is not None:
            in_specs.append(pl.BlockSpec(memory_space=pl.ANY))
            aliases = {len(args): 0}
            args.append(prev)
    else:
        out_spec = pl.BlockSpec((1, 2, tm, hw), lambda s, i: (s, 0, i, 0))
        out_shape = jax.ShapeDtypeStruct((n_seq, 2, half, hw), out_dtype)
    out = pl.pallas_call(
        functools.partial(_conv_inv_kernel, inv_len=1.0 / length, token_order=token_order),
        grid=(n_seq, nt),
        in_specs=in_specs,
        out_specs=out_spec,
        out_shape=out_shape,
        scratch_shapes=scratch,
        input_output_aliases=aliases,
        compiler_params=_cparams(2), name="hyena_conv_inv",
    )(*args)
    return out if token_order else out.reshape(n_seq * length, hw)


def _merge_kernel(oa_ref, oh_ref, y_ref, hf_ref, hb_ref, g_ref, wa_ref, wh_ref, wl_ref, o_ref, *, tn):
    d = o_ref.shape[1]
    ol = (jax.nn.gelu(y_ref[...].astype(F32), approximate=True) * (hf_ref[...] + hb_ref[...])).astype(BF16)
    oa = oa_ref[...]
    oh = oh_ref[...]
    for c0 in range(0, d, tn):
        cols = slice(c0, c0 + tn)
        gate = lambda k: _sigmoid(g_ref[:, k * d + c0:k * d + c0 + tn].astype(F32))
        m = (gate(0) * jnp.dot(oa, wa_ref[0, :, cols], preferred_element_type=F32)
             + gate(1) * jnp.dot(oh, wh_ref[0, :, cols], preferred_element_type=F32)
             + gate(2) * jnp.dot(ol, wl_ref[0, :, cols], preferred_element_type=F32))
        o_ref[:, cols] = m.astype(o_ref.dtype)


def merge_branches(o_att, o_hy, p, gates, hf, hb, wa, wh, wl, layer, lru_y_off, tm, rows=None):
    t, aw = o_att.shape
    rows = t if rows is None else rows
    hw = o_hy.shape[1]
    lw = hf.shape[1]
    d = wa.shape[2]
    w_spec = lambda rows: pl.BlockSpec((1, rows, d), lambda i: (layer, 0, 0))
    return pl.pallas_call(
        functools.partial(_merge_kernel, tn=_pick(d, 1024)),
        grid=(rows // tm,),
        in_specs=[pl.BlockSpec((tm, aw), lambda i: (i, 0)),
                  pl.BlockSpec((tm, hw), lambda i: (i, 0)),
                  pl.BlockSpec((tm, lw), lambda i: (i, lru_y_off // lw)),
                  pl.BlockSpec((tm, lw), lambda i: (i, 0)),
                  pl.BlockSpec((tm, lw), lambda i: (i, 0)),
                  pl.BlockSpec((tm, 3 * d), lambda i: (i, 0)),
                  w_spec(aw), w_spec(hw), w_spec(lw)],
        out_specs=pl.BlockSpec((tm, d), lambda i: (i, 0)),
        out_shape=jax.ShapeDtypeStruct((t, d), BF16),
        compiler_params=_cparams(1), name="merge_branches",
    )(o_att, o_hy, p, hf, hb, gates, wa, wh, wl)


def _out_proj_kernel(m_ref, w_ref, x_ref, gate_ref, o_ref):
    o_ref[...] = x_ref[...] + gate_ref[0, 0] * jnp.dot(m_ref[...], w_ref[0],
                                                       preferred_element_type=F32)


def out_proj_residual(m, w, layer, x, mods, which_gate, n_lat, batch, tm, rows=None):
    t, d = x.shape
    return pl.pallas_call(
        _out_proj_kernel,
        grid=((t if rows is None else rows) // tm,),
        in_specs=[pl.BlockSpec((tm, d), lambda i: (i, 0)),
                  pl.BlockSpec((1, d, d), lambda i: (layer, 0, 0)),
                  pl.BlockSpec((tm, d), lambda i: (i, 0)),
                  _mod_spec(which_gate, tm, n_lat, batch, d)],
        out_specs=pl.BlockSpec((tm, d), lambda i: (i, 0)),
        out_shape=jax.ShapeDtypeStruct((t, d), F32),
        compiler_params=_cparams(1), name="out_proj_residual",
    )(m, w, x, mods)


def _router_kernel(x_ref, g_ref, sh_ref, sc_ref, rw_ref, rb_ref, h_ref, meta_ref, cnt_ref):
    h = _normmod(x_ref[...], g_ref[...], sh_ref[0, 0], sc_ref[0, 0])
    _store_tile_rows(h_ref, 0, _pack_bf16_pairs(h))
    w = rw_ref[...]
    h_hi, w_hi = h.astype(BF16), w.astype(BF16)
    h_lo = (h - h_hi.astype(F32)).astype(BF16)
    w_lo = (w - w_hi.astype(F32)).astype(BF16)
    logits = (jnp.dot(h_hi, w_hi, preferred_element_type=F32) + jnp.dot(h_lo, w_hi, preferred_element_type=F32)
              + jnp.dot(h_hi, w_lo, preferred_element_type=F32)) + rb_ref[...]
    lane = lax.broadcasted_iota(jnp.int32, logits.shape, 1)
    valid = lane < N_EXPERTS
    logits = jnp.where(valid, logits, -jnp.inf)
    ex = jnp.exp(logits - jnp.max(logits, axis=-1, keepdims=True))
    scores = ex / jnp.sum(ex, axis=-1, keepdims=True)
    grp = lax.shift_right_logical(lane, int(math.log2(EXPERTS_PER_GROUP)))
    best = jnp.zeros((logits.shape[0], 1), jnp.int32)
    best_max = jnp.max(jnp.where((grp == 0) & valid, scores, -1.0), axis=-1, keepdims=True)
    for g in range(1, N_GROUPS):
        gm = jnp.max(jnp.where((grp == g) & valid, scores, -1.0), axis=-1, keepdims=True)
        better = gm > best_max
        best = jnp.where(better, g, best)
        best_max = jnp.where(better, gm, best_max)
    s1 = jnp.where((grp == best) & valid, scores, -1.0)
    m1 = jnp.max(s1, axis=-1, keepdims=True)
    i1 = jnp.min(jnp.where(s1 == m1, lane, LANES_V7X), axis=-1, keepdims=True)
    s2 = jnp.where(lane == i1, -1.0, s1)
    m2 = jnp.max(s2, axis=-1, keepdims=True)
    i2 = jnp.min(jnp.where(s2 == m2, lane, LANES_V7X), axis=-1, keepdims=True)
    tot = m1 + m2
    @pl.when(pl.program_id(0) == 0)
    def _():
        cnt_ref[...] = jnp.zeros_like(cnt_ref)

    tm = logits.shape[0]
    onehot = ((lane == i1) | (lane == i2)).astype(BF16)
    lower = (lax.broadcasted_iota(jnp.int32, (tm, tm), 0)
             > lax.broadcasted_iota(jnp.int32, (tm, tm), 1)).astype(BF16)
    before = jnp.dot(lower, onehot, preferred_element_type=F32) + cnt_ref[...]
    r1 = jnp.sum(jnp.where(lane == i1, before, 0.0), axis=-1, keepdims=True)
    r2 = jnp.sum(jnp.where(lane == i2, before, 0.0), axis=-1, keepdims=True)
    cnt_ref[...] += jnp.sum(onehot.astype(F32), axis=0, keepdims=True)
    cols = (i1.astype(F32), i2.astype(F32), r1, r2, m1 / tot, m2 / tot)
    meta = jnp.zeros(logits.shape, F32)
    for k, col in enumerate(cols):
        meta = jnp.where(lane == k, col, meta)
    meta_ref[...] = meta


ROUTE_E, ROUTE_RANK, ROUTE_W = 0, 2, 4


def router(x, g, mods, which_shift, rw, rb, n_lat, batch, tm, rows=None):
    t, d = x.shape
    spr = d // 2 // LANES_V7X
    return pl.pallas_call(
        _router_kernel,
        grid=((t if rows is None else rows) // tm,),
        in_specs=[pl.BlockSpec((tm, d), lambda i: (i, 0)),
                  pl.BlockSpec((1, d), lambda i: (0, 0)),
                  _mod_spec(which_shift, tm, n_lat, batch, d),
                  _mod_spec(which_shift + 1, tm, n_lat, batch, d),
                  pl.BlockSpec((d, LANES_V7X), lambda i: (0, 0)),
                  pl.BlockSpec((1, LANES_V7X), lambda i: (0, 0))],
        out_specs=[pl.BlockSpec((tm * spr, LANES_V7X), lambda i: (i, 0)),
                   pl.BlockSpec((tm, LANES_V7X), lambda i: (i, 0)),
                   pl.BlockSpec((1, LANES_V7X), lambda i: (0, 0))],
        out_shape=[jax.ShapeDtypeStruct((t * spr, LANES_V7X), jnp.uint32),
                   jax.ShapeDtypeStruct((t, LANES_V7X), F32),
                   jax.ShapeDtypeStruct((1, LANES_V7X), F32)],
        compiler_params=_cparams(1), name="router",
    )(x, g.reshape(1, d), mods, mods, rw, rb)


def dispatch_plan(meta, counts, n_exp, tile, n_tiles):
    cnt = counts[0, :n_exp].astype(jnp.int32)
    tiles_e = (cnt + tile - 1) // tile
    tile_end = jnp.cumsum(tiles_e)
    row_start = (tile_end - tiles_e) * tile
    experts = meta[:, ROUTE_E:ROUTE_E + 2].astype(jnp.int32)
    ranks = meta[:, ROUTE_RANK:ROUTE_RANK + 2].astype(jnp.int32)
    dest = (row_start[experts] + ranks).reshape(-1)
    tile_expert = jnp.minimum(jnp.sum(jnp.arange(n_tiles)[:, None] >= tile_end[None, :], axis=1),
                              n_exp - 1).astype(jnp.int32)
    zero_start = jnp.minimum(row_start + cnt, (n_tiles - 1) * tile).astype(jnp.int32)
    return dest, tile_expert, tile_end[-1:].astype(jnp.int32), zero_start


def _dispatch_kernel(dest_ref, zero_ref, h_ref, xs_ref, zbuf, sem, zsem, *, spr):
    tm = h_ref.shape[0] // spr
    base = pl.program_id(0) * (2 * tm)
    token = lambda ref, t: ref.at[pl.ds(pl.multiple_of(t * spr, spr), spr)]

    @pl.when(pl.program_id(0) == 0)
    def _():
        zbuf[...] = jnp.zeros_like(zbuf)

        def clear(e, carry):
            start = pl.multiple_of(zero_ref[e] * spr, spr)
            copy = pltpu.make_async_copy(zbuf, xs_ref.at[pl.ds(start, zbuf.shape[0])], zsem)
            copy.start()
            copy.wait()
            return carry

        lax.fori_loop(0, zero_ref.shape[0], clear, 0)

    def issue(r, carry):
        for k in range(2):
            pltpu.make_async_copy(token(h_ref, r), token(xs_ref, dest_ref[base + 2 * r + k]), sem).start(priority=k)
        return carry

    lax.fori_loop(0, tm, issue, 0, unroll=8)
    for k in range(2):
        pltpu.make_async_copy(h_ref, xs_ref.at[pl.ds(0, tm * spr)], sem).wait()


def moe_dispatch(h, dest, zero_start, n_rows, tile, tm, spr):
    assert spr == SUBLANES_V7X
    t = dest.shape[0] // 2
    return pl.pallas_call(
        functools.partial(_dispatch_kernel, spr=spr),
        grid_spec=pltpu.PrefetchScalarGridSpec(
            num_scalar_prefetch=2, grid=(t // tm,),
            in_specs=[pl.BlockSpec((tm * spr, LANES_V7X), lambda i, dest, zs: (i, 0))],
            out_specs=pl.BlockSpec(memory_space=pl.ANY),
            scratch_shapes=[pltpu.VMEM((tile * spr, LANES_V7X), h.dtype),
                            pltpu.SemaphoreType.DMA, pltpu.SemaphoreType.DMA]),
        out_shape=jax.ShapeDtypeStruct((n_rows * spr, LANES_V7X), h.dtype),
        compiler_params=_cparams(1), name="moe_dispatch",
    )(dest, zero_start, h)


def _experts_kernel(te_ref, nu_ref, xs_ref, wg_ref, wu_ref, wd_ref, ys_ref, wgb, wub, wdb, *, spr):
    j = pl.program_id(0)

    @pl.when(j < nu_ref[0])
    def _():
        @pl.when((j == 0) | (te_ref[j] != te_ref[jnp.maximum(j - 1, 0)]))
        def _():
            wgb[...] = wg_ref[0, 0].astype(BF16)
            wub[...] = wu_ref[0, 0].astype(BF16)
            wdb[...] = wd_ref[0, 0].astype(BF16)

        tile = xs_ref.shape[0] // spr
        x_hi, x_lo = (v.astype(BF16) for v in _unpack_bf16_pairs(_load_tile_rows(xs_ref, 0, tile, spr)))
        half = x_hi.shape[1]

        def proj(w):
            return (jnp.dot(x_hi, w[:half, :], preferred_element_type=F32)
                    + jnp.dot(x_lo, w[half:, :], preferred_element_type=F32))

        gt = proj(wgb)
        act = (gt * _sigmoid(gt)) * proj(wub)
        _store_tile_rows(ys_ref, 0, _pack_bf16_pairs(
            jnp.dot(act.astype(BF16), wdb[...], preferred_element_type=F32)))


def moe_experts(tile_expert, n_used, xs, wg, wu, wd, layer, tile):
    d, de = wg.shape[2], wg.shape[3]
    spr = d // 2 // LANES_V7X
    n_tiles = xs.shape[0] // (tile * spr)
    used = lambda j, nu: jnp.minimum(j, nu[0] - 1)
    w_spec = lambda shape: pl.BlockSpec((1, 1) + shape, lambda j, te, nu: (layer, te[used(j, nu)], 0, 0))
    return pl.pallas_call(
        functools.partial(_experts_kernel, spr=spr),
        grid_spec=pltpu.PrefetchScalarGridSpec(
            num_scalar_prefetch=2, grid=(n_tiles,),
            in_specs=[pl.BlockSpec((tile * spr, LANES_V7X), lambda j, te, nu: (used(j, nu), 0)),
                      w_spec((d, de)), w_spec((d, de)), w_spec((de, d))],
            out_specs=pl.BlockSpec((tile * spr, LANES_V7X), lambda j, te, nu: (used(j, nu), 0)),
            scratch_shapes=[pltpu.VMEM((d, de), BF16), pltpu.VMEM((d, de), BF16),
                            pltpu.VMEM((de, d), BF16)]),
        out_shape=jax.ShapeDtypeStruct(xs.shape, xs.dtype),
        compiler_params=_cparams(1), name="moe_experts",
    )(tile_expert, n_used, xs, wg, wu, wd)


def _combine_kernel(dest_ref, meta_ref, x_ref, gate_ref, ys_ref, o_ref, buf, sem, *, spr):
    i = pl.program_id(0)
    tm = x_ref.shape[0]
    part = lambda slot, k: pl.multiple_of((slot * 2 + k) * (tm * spr), spr)

    def gather(tile, slot):
        base = tile * (2 * tm)

        def issue(r, carry):
            for k in range(2):
                row = dest_ref[base + 2 * r + k]
                pltpu.make_async_copy(ys_ref.at[pl.ds(pl.multiple_of(row * spr, spr), spr)],
                                      buf.at[pl.ds(part(slot, k) + r * spr, spr)], sem.at[slot]).start(priority=k)
            return carry

        lax.fori_loop(0, tm, issue, 0, unroll=8)

    @pl.when(i == 0)
    def _():
        gather(0, 0)

    @pl.when(i + 1 < pl.num_programs(0))
    def _():
        gather(i + 1, (i + 1) % 2)

    slot = i % 2
    for k in range(2):
        pltpu.make_async_copy(ys_ref.at[pl.ds(0, tm * spr)], buf.at[pl.ds(part(slot, k), tm * spr)],
                              sem.at[slot]).wait()
    meta = meta_ref[...]
    w1 = meta[:, ROUTE_W:ROUTE_W + 1]
    w2 = meta[:, ROUTE_W + 1:ROUTE_W + 2]
    y1_hi, y1_lo = _unpack_bf16_pairs(_load_tile_rows(buf, part(slot, 0), tm, spr))
    y2_hi, y2_lo = _unpack_bf16_pairs(_load_tile_rows(buf, part(slot, 1), tm, spr))
    half = y1_hi.shape[1]
    gate = gate_ref[0, 0]
    o_ref[:, :half] = x_ref[:, :half] + gate[:, :half] * (w1 * y1_hi + w2 * y2_hi)
    o_ref[:, half:] = x_ref[:, half:] + gate[:, half:] * (w1 * y1_lo + w2 * y2_lo)


def moe_combine(dest, meta, x, mods, which_gate, ys, n_lat, batch, tm):
    t, d = x.shape
    spr = d // 2 // LANES_V7X
    return pl.pallas_call(
        functools.partial(_combine_kernel, spr=spr),
        grid_spec=pltpu.PrefetchScalarGridSpec(
            num_scalar_prefetch=1, grid=(dest.shape[0] // 2 // tm,),
            in_specs=[pl.BlockSpec((tm, LANES_V7X), lambda i, dest: (i, 0)),
                      pl.BlockSpec((tm, d), lambda i, dest: (i, 0)),
                      _mod_spec(which_gate, tm, n_lat, batch, d),
                      pl.BlockSpec(memory_space=pl.ANY)],
            out_specs=pl.BlockSpec((tm, d), lambda i, dest: (i, 0)),
            scratch_shapes=[pltpu.VMEM((2 * 2 * tm * spr, LANES_V7X), ys.dtype),
                            pltpu.SemaphoreType.DMA((2,))]),
        out_shape=jax.ShapeDtypeStruct((t, d), F32),
        compiler_params=_cparams(1), name="moe_combine",
    )(dest, meta, x, mods, ys)


def _rope_table(n_tokens):
    rows = n_tokens // GRID_W
    row = jnp.repeat(jnp.arange(rows), GRID_W).astype(F32)
    col = jnp.tile(jnp.arange(GRID_W), rows).astype(F32)
    pairs = HEAD_DIM // 4
    inv = ROPE_THETA ** (-jnp.arange(pairs, dtype=F32) / pairs)
    ang = jnp.concatenate([row[:, None] * inv, col[:, None] * inv], axis=-1)
    cos, sin = jnp.cos(ang), jnp.sin(ang)
    return jnp.concatenate([cos, cos, -sin, sin], axis=-1)


def _trig_kernel(ca_ref, sa_ref, cr_ref, sr_ref, o_ref, *, sign, inverse):
    tm = cr_ref.shape[0]
    n_lanes = ca_ref.shape[1] * LANES_V7X
    cr = cr_ref[...]
    sr = sr_ref[...]
    row = pl.program_id(0) * tm + lax.broadcasted_iota(jnp.int32, (tm, 1), 0)
    alt_row = sign * (1 - 2 * (row & 1)).astype(F32)
    for a in range(n_lanes // LANES_V7X):
        blk = slice(a * LANES_V7X, (a + 1) * LANES_V7X)
        ca = ca_ref[:, a:a + 1]
        sa = sa_ref[:, a:a + 1]
        cos_blk = ca * cr - sa * sr
        nsin_blk = -(sa * cr + ca * sr)
        col = a * LANES_V7X + lax.broadcasted_iota(jnp.int32, (1, LANES_V7X), 1)
        if inverse:
            o_ref[:, blk] = cos_blk.astype(o_ref.dtype)
            o_ref[:, n_lanes + a * LANES_V7X:n_lanes + (a + 1) * LANES_V7X] = (
                jnp.where(col == 0, alt_row, nsin_blk).astype(o_ref.dtype))
        else:
            alt_col = sign * (1 - 2 * (col & 1)).astype(F32)
            o_ref[0, :, blk] = cos_blk.astype(o_ref.dtype)
            o_ref[1, :, blk] = jnp.where(row == 0, alt_col, nsin_blk).astype(o_ref.dtype)


def _dft_tables(length):
    half = length // 2
    n_a = half // LANES_V7X
    tm = _pick(half, 256)
    idx = jnp.arange(half, dtype=jnp.int32)[:, None]
    a_hi = jnp.arange(n_a, dtype=jnp.int32)[None, :] * LANES_V7X
    r_lo = jnp.arange(LANES_V7X, dtype=jnp.int32)[None, :]
    ang = lambda prod: (prod % (2 * length)).astype(F32) * (math.pi / length)

    def table(row_val, lane_hi, lane_lo, sign, inverse):
        out_shape = (half, 2 * half) if inverse else (2, half, half)
        out_block = (tm, 2 * half) if inverse else (2, tm, half)
        out_index = (lambda i: (i, 0)) if inverse else (lambda i: (0, i, 0))
        small = [f(ang(row_val * v)) for v in (lane_hi, lane_lo) for f in (jnp.cos, jnp.sin)]
        return pl.pallas_call(
            functools.partial(_trig_kernel, sign=sign, inverse=inverse),
            grid=(half // tm,),
            in_specs=[pl.BlockSpec((tm, n_a), lambda i: (i, 0)), pl.BlockSpec((tm, n_a), lambda i: (i, 0)),
                      pl.BlockSpec((tm, LANES_V7X), lambda i: (i, 0)),
                      pl.BlockSpec((tm, LANES_V7X), lambda i: (i, 0))],
            out_specs=pl.BlockSpec(out_block, out_index),
            out_shape=jax.ShapeDtypeStruct(out_shape, BF16),
            compiler_params=_cparams(1), name="dft_tables",
        )(*small)

    tables = []
    for par, sign in ((0, 1.0), (1, -1.0)):
        tables.append(table(idx, 2 * a_hi, 2 * r_lo + par, sign, False))
        tables.append(table(2 * idx + par, a_hi, r_lo, sign, True))
    fe, ie, fo, io = tables
    return fe, fo, ie, io


def _hyena_features(length, k_pad):
    t = jnp.linspace(0.0, 1.0, length, dtype=F32)[:, None]
    w = 2.0 * math.pi * jnp.arange(length, dtype=F32)[:, None] / length
    f = jnp.linspace(1e-4, HYENA_BANDS - 1, HYENA_BANDS, dtype=F32)[None, :]
    z = jnp.concatenate([t, jnp.cos(f * w), -jnp.sin(f * w)], axis=-1)
    return jnp.pad(z, ((0, 0), (0, k_pad - z.shape[1])))


def _blockdiag_dense(w):
    nb, bs = w.shape[-3], w.shape[-2]
    eye = jnp.eye(nb, dtype=w.dtype)
    dense = jnp.einsum('...nde,nm->...ndme', w, eye)
    return dense.reshape(w.shape[:-3] + (nb * bs, nb * bs))


def kernel(x, c, ctx, c_ctx, w_ada, b_ada, g_mix, g_ffn, w_in, attn_sink, hy_short_w, hy_short_b, hy_w1, hy_b1, hy_freq, hy_w2, hy_b2, hy_w3, hy_skip, lru_conv_w, lru_conv_b, lru_wa, lru_ba, lru_wx, lru_bx, lru_lambda, w_br_attn, w_br_hy, w_br_lru, w_out, router_w, router_b, exp_w_gate, exp_w_up, exp_w_down, final_g):
    batch, n_lat, d = x.shape
    n_ctx = ctx.shape[1]
    depth = w_ada.shape[0]
    t_lat, t_ctx = batch * n_lat, batch * n_ctx
    hw = hy_skip.shape[2]
    lw = lru_lambda.shape[2]
    fh = hy_w2.shape[1]
    aw = N_Q_HEADS * HEAD_DIM
    kvw = N_KV_HEADS * HEAD_DIM
    q_off = 2 * kvw + lw
    hy_off = q_off + aw
    lru_y_off = hy_off + 3 * hw
    gate_off = lru_y_off + lw
    assert batch + 1 <= MOD_ROWS and t_lat % n_ctx == 0 and n_lat % ATTN_BLOCK == 0
    tile_base = math.gcd(n_lat, t_ctx)
    tm = _pick(tile_base, 512)
    tm_big = _pick(tile_base, 1024)

    xu = jnp.concatenate([x.reshape(t_lat, d), ctx.reshape(t_ctx, d)], axis=0)
    cc = jnp.concatenate([c, c_ctx[None, :], jnp.zeros((MOD_ROWS - batch - 1, d), F32)], axis=0)
    mods_all = ada_tables(cc, w_ada, b_ada).reshape(depth, MOD_ROWS, N_MOD, 1, d)

    rope_tab = _rope_table(n_lat)
    band_bias = _band_bias(n_lat)
    deltas = jnp.abs(jnp.linspace(math.log(HYENA_DECAY_TARGET) / HYENA_FAST_DECAY,
                                  math.log(HYENA_DECAY_TARGET) / HYENA_SLOW_DECAY, hw, dtype=F32))[None, :]
    seqs = []
    for length, row0 in ((n_lat, 0), (n_ctx, t_lat)):
        seqs.append((length, row0, _dft_tables(length), _hyena_features(length, fh)))
    hy_w1p = jnp.pad(hy_w1, ((0, 0), (0, fh - hy_w1.shape[1]), (0, 0)))

    sp = softplus_neg(lru_lambda)
    lru_w = jnp.concatenate([_blockdiag_dense(lru_wa), _blockdiag_dense(lru_wx)], axis=-1).astype(BF16)
    lru_b = jnp.concatenate([lru_ba, lru_bx], axis=-1)[:, :, None, :]
    wa_b, wh_b, wl_b, wo_b = (w.astype(BF16) for w in (w_br_attn, w_br_hy, w_br_lru, w_out))
    rw = jnp.pad(router_w, ((0, 0), (0, LANES_V7X - router_w.shape[1])))
    rb = jnp.pad(router_b, (0, LANES_V7X - router_b.shape[0]))[None, :]
    h_zero = jnp.zeros((batch, 2, lw), F32)
    t_all = t_lat + t_ctx
    n_exp = exp_w_gate.shape[1]
    exp_tile = _pick(2 * t_all, 512)
    n_exp_tiles = (2 * t_all) // exp_tile + n_exp
    tm_cmb = _pick(tile_base, 256)

    for l in range(depth):
        last = l == depth - 1
        rows = t_lat if last else t_all
        mods = mods_all[l]
        h = normmod(xu, g_mix[l], mods, 0, n_lat, batch, tm)
        p = in_proj(h, w_in, l, tm_big, 0, gate_off, BF16)
        gates = in_proj(h, w_in, l, tm_big, gate_off, w_in.shape[2] - gate_off, BF16, rows=rows)

        o_att = latent_attention(p, attn_sink[l], rope_tab, band_bias, batch, n_lat, n_ctx, q_off)
        if not last:
            o_att = context_attention(p, attn_sink[l], o_att, batch, n_lat, n_ctx, q_off)

        xc_c = dwconv(p, lru_conv_w[l], lru_conv_b[l], batch, n_ctx, t_lat, 2 * kvw, lw)
        hf, hb, h_end = rg_lru(xc_c, lru_w, l, lru_b[l], sp[l], h_zero, batch, n_ctx, t_all, t_lat)
        xc_l = dwconv(p, lru_conv_w[l], lru_conv_b[l], batch, n_lat, 0, 2 * kvw, lw)
        hf, hb, _ = rg_lru(xc_l, lru_w, l, lru_b[l], sp[l], h_end, batch, n_lat, t_all, 0, prev=(hf, hb))

        o_hy = None
        for length, row0, (fe, fo, ie, io), feats in seqs[:1] if last else seqs:
            u = dwconv(p, hy_short_w[l], hy_short_b[l], batch, length, row0, hy_off, 3 * hw, parity_order=True)
            hid = hyena_hidden(feats, hy_w1p[l], hy_b1[l], hy_freq[l], hy_w2[l], hy_b2[l])
            taps, kh = hyena_taps(hid, hy_w3[l], deltas)
            kspec = hyena_spectrum(fe, fo, taps)
            pq = conv_forward(fe, fo, u, 0, kspec, kh, 0, batch, length, hw)
            z1 = conv_inverse(ie, io, pq, u, 0, u, hw, hy_skip[l], 0, batch, length, BF16)
            pq = conv_forward(fe, fo, z1, 0, kspec, kh, 1, batch, length, hw)
            o_hy = conv_inverse(ie, io, pq, z1, 0, u, 2 * hw, hy_skip[l], 1, batch, length, BF16,
                                token_order=True, total_rows=t_all, row0=row0, prev=o_hy)

        m = merge_branches(o_att, o_hy, p, gates, hf, hb, wa_b, wh_b, wl_b, l, lru_y_off, tm, rows=rows)
        xu = out_proj_residual(m, wo_b, l, xu, mods, 2, n_lat, batch, tm, rows=rows)

        fl, route, counts = router(xu, g_ffn[l], mods, 3, rw, rb, n_lat, batch, tm, rows=rows)
        dest, tile_expert, n_used, zero_start = dispatch_plan(route[:rows], counts, n_exp, exp_tile, n_exp_tiles)
        xs = moe_dispatch(fl, dest, zero_start, n_exp_tiles * exp_tile, exp_tile, tm, d // 2 // LANES_V7X)
        ys = moe_experts(tile_expert, n_used, xs, exp_w_gate, exp_w_up, exp_w_down, l, exp_tile)
        xu = moe_combine(dest, route, xu, mods, 5, ys, n_lat, batch, tm_cmb)

    return final_norm(xu, final_g, t_lat, tm).reshape(batch, n_lat, d)
```
